```python
import jax, jax.numpy as jnp
from jax import lax
import numpy as np

D_MODEL = 4096
BATCH = 4
SEQ = 2048
DEPTH = 2

GRID_W = 64
CTX_LEN = 256
HEAD_DIM = 128
NA_HEADS = D_MODEL // HEAD_DIM
NA_WIN_H = 8
NA_WIN_W = 16
GQA_Q_HEADS = D_MODEL // HEAD_DIM
GQA_KV_HEADS = GQA_Q_HEADS // 4
Q_BLOCK = 128
ROPE_THETA = 10000.0
N_EXPERTS = 16
N_GROUPS = 4
EXPERTS_PER_GROUP = N_EXPERTS // N_GROUPS
TOP_K = 2
D_FF = 1536
N_MIXERS = 2
N_MOD = 6
EPS = 1e-6
NEG_INF = -1e30

kernel_name = 'hybrid_natten_gqa_grouped_moe_dit'


def rms_norm(x, gain):
    x32 = x.astype(jnp.float32)
    y = x32 * lax.rsqrt(jnp.mean(x32 * x32, axis=-1, keepdims=True) + EPS)
    return (y * gain.astype(jnp.float32)).astype(x.dtype)


def modulate(h, shift, scale):
    return h * (1 + scale) + shift


def axial_rope_angles(n_tokens):
    t = jnp.arange(n_tokens, dtype=jnp.int32)
    row = (t // GRID_W).astype(jnp.float32)
    col = (t % GRID_W).astype(jnp.float32)
    axis_dim = HEAD_DIM // 2
    inv_freq = 1.0 / (ROPE_THETA ** (jnp.arange(0, axis_dim, 2, dtype=jnp.float32) / axis_dim))
    return jnp.concatenate([row[:, None] * inv_freq, col[:, None] * inv_freq], axis=-1)


def apply_rope(x, ang):
    xf = x.astype(jnp.float32).reshape(x.shape[:-1] + (HEAD_DIM // 2, 2))
    cos = jnp.cos(ang)[None, :, None, :]
    sin = jnp.sin(ang)[None, :, None, :]
    x1, x2 = xf[..., 0], xf[..., 1]
    out = jnp.stack([x1 * cos - x2 * sin, x1 * sin + x2 * cos], axis=-1)
    return out.reshape(x.shape).astype(x.dtype)


def neighbourhood_attention(h_lat, h_ctx, w_qkv, rpb, w_o, with_ctx_out):
    bsz, n_lat, d = h_lat.shape
    rows = n_lat // GRID_W
    kh = min(NA_WIN_H, rows)
    scale = HEAD_DIM ** -0.5
    q, k, v = jnp.split((h_lat @ w_qkv).reshape(bsz, rows, GRID_W, 3 * NA_HEADS, HEAD_DIM), 3, axis=3)
    qc, kc, vc = jnp.split((h_ctx @ w_qkv).reshape(bsz, h_ctx.shape[1], 3 * NA_HEADS, HEAD_DIM), 3, axis=2)
    qcol = jnp.arange(GRID_W)
    col_start = jnp.clip(qcol - NA_WIN_W // 2, 0, GRID_W - NA_WIN_W)
    col_valid = (qcol[None, :] >= col_start[:, None]) & (qcol[None, :] < col_start[:, None] + NA_WIN_W)
    dc_idx = jnp.clip(qcol[None, :] - qcol[:, None], -(NA_WIN_W - 1), NA_WIN_W - 1) + NA_WIN_W - 1
    n_loc = kh * GRID_W

    def row_block(r):
        rs = jnp.clip(r - kh // 2, 0, rows - kh)
        q_row = lax.dynamic_index_in_dim(q, r, axis=1, keepdims=False) * scale
        k_band = lax.dynamic_slice_in_dim(k, rs, kh, axis=1)
        v_band = lax.dynamic_slice_in_dim(v, rs, kh, axis=1)
        dr_idx = rs + jnp.arange(kh) - r + NA_WIN_H - 1
        bias = rpb[:, dr_idx[None, :, None], dc_idx[:, None, :]].astype(jnp.float32)
        s_loc = jnp.einsum('bqhd,bjkhd->bhqjk', q_row, k_band, preferred_element_type=jnp.float32) + bias[None]
        s_loc = jnp.where(col_valid[:, None, :], s_loc, NEG_INF)
        s_ctx = jnp.einsum('bqhd,bchd->bhqc', q_row, kc, preferred_element_type=jnp.float32)
        s = jnp.concatenate([s_loc.reshape(bsz, NA_HEADS, GRID_W, n_loc), s_ctx], axis=-1)
        p = jax.nn.softmax(s, axis=-1).astype(v.dtype)
        p_loc = p[..., :n_loc].reshape(bsz, NA_HEADS, GRID_W, kh, GRID_W)
        return (jnp.einsum('bhqjk,bjkhd->bqhd', p_loc, v_band)
                + jnp.einsum('bhqc,bchd->bqhd', p[..., n_loc:], vc))

    o = lax.map(row_block, jnp.arange(rows))
    o_lat = jnp.moveaxis(o, 0, 1).reshape(bsz, n_lat, NA_HEADS * HEAD_DIM) @ w_o
    o_ctx = None
    if with_ctx_out:
        s = jnp.einsum('bqhd,bkhd->bhqk', qc * scale, kc, preferred_element_type=jnp.float32)
        p = jax.nn.softmax(s, axis=-1).astype(vc.dtype)
        o_ctx = jnp.einsum('bhqk,bkhd->bqhd', p, vc).reshape(bsz, h_ctx.shape[1], NA_HEADS * HEAD_DIM) @ w_o
    return o_lat, o_ctx


def grouped_query_attention(h_lat, h_ctx, w_qkv, q_gain, k_gain, w_o, with_ctx_out):
    bsz, n_lat, d = h_lat.shape
    n_rep = GQA_Q_HEADS // GQA_KV_HEADS
    scale = HEAD_DIM ** -0.5

    def project(h):
        qkv = (h @ w_qkv).reshape(h.shape[0], h.shape[1], GQA_Q_HEADS + 2 * GQA_KV_HEADS, HEAD_DIM)
        qh = rms_norm(qkv[:, :, :GQA_Q_HEADS], q_gain)
        kh = rms_norm(qkv[:, :, GQA_Q_HEADS:GQA_Q_HEADS + GQA_KV_HEADS], k_gain)
        return qh, kh, qkv[:, :, GQA_Q_HEADS + GQA_KV_HEADS:]

    q, k, v = project(h_lat)
    ang = axial_rope_angles(n_lat)
    q = apply_rope(q, ang) * scale
    k = apply_rope(k, ang)
    qc, kc, vc = project(h_ctx)
    n_blocks = n_lat // Q_BLOCK
    q_blocks = jnp.moveaxis(q.reshape(bsz, n_blocks, Q_BLOCK, GQA_KV_HEADS, n_rep, HEAD_DIM), 1, 0)

    def query_block(qb):
        s_lat = jnp.einsum('bqkrd,bskd->bkrqs', qb, k, preferred_element_type=jnp.float32)
        s_ctx = jnp.einsum('bqkrd,bskd->bkrqs', qb, kc, preferred_element_type=jnp.float32)
        p = jax.nn.softmax(jnp.concatenate([s_lat, s_ctx], axis=-1), axis=-1).astype(v.dtype)
        return (jnp.einsum('bkrqs,bskd->bqkrd', p[..., :n_lat], v)
                + jnp.einsum('bkrqs,bskd->bqkrd', p[..., n_lat:], vc))

    o = lax.map(query_block, q_blocks)
    o_lat = jnp.moveaxis(o, 0, 1).reshape(bsz, n_lat, GQA_Q_HEADS * HEAD_DIM) @ w_o
    o_ctx = None
    if with_ctx_out:
        qcb = (qc * scale).reshape(bsz, h_ctx.shape[1], GQA_KV_HEADS, n_rep, HEAD_DIM)
        s = jnp.einsum('bqkrd,bskd->bkrqs', qcb, kc, preferred_element_type=jnp.float32)
        p = jax.nn.softmax(s, axis=-1).astype(vc.dtype)
        o_ctx = jnp.einsum('bkrqs,bskd->bqkrd', p, vc).reshape(bsz, h_ctx.shape[1], GQA_Q_HEADS * HEAD_DIM) @ w_o
    return o_lat, o_ctx


def grouped_moe(h, router_w, router_b, w_gate, w_up, w_down):
    logits = jnp.dot(h, router_w, preferred_element_type=jnp.float32) + router_b.astype(jnp.float32)
    probs = jax.nn.softmax(logits, axis=-1)
    grouped = probs.reshape(-1, N_GROUPS, EXPERTS_PER_GROUP)
    group_score = lax.top_k(grouped, TOP_K)[0].sum(axis=-1)
    group = jnp.argmax(group_score, axis=-1)
    in_group = jnp.take_along_axis(grouped, group[:, None, None], axis=1)[:, 0]
    top_p, top_i = lax.top_k(in_group, TOP_K)
    weights = top_p / jnp.sum(top_p, axis=-1, keepdims=True)
    expert_id = group[:, None] * EXPERTS_PER_GROUP + top_i
    gates = jnp.einsum('nk,nke->ne', weights,
                       jax.nn.one_hot(expert_id, N_EXPERTS, dtype=jnp.float32)).astype(h.dtype)
    out = jnp.zeros_like(h)
    for e in range(N_EXPERTS):
        hidden = jax.nn.silu(h @ w_gate[e]) * (h @ w_up[e])
        out = out + gates[:, e:e + 1] * (hidden @ w_down[e])
    return out


def setup_inputs(seed: int = 0) -> dict:
    key = jax.random.key(seed)
    ks = iter(jax.random.split(key, 64))
    f32 = jnp.float32

    def normal(shape, scale):
        return jax.random.normal(next(ks), shape, f32) * scale

    def gain(n):
        return 1.0 + normal((n,), 0.02)

    d = D_MODEL
    inp = {}
    inp['x'] = normal((BATCH, SEQ, d), 1.0)
    inp['c'] = normal((BATCH, d), 1.0)
    inp['ctx'] = normal((BATCH, CTX_LEN, d), 1.0)
    inp['c_ctx'] = normal((d,), 1.0)
    inp['router_w'] = normal((d, N_EXPERTS), d ** -0.5)
    inp['router_b'] = normal((N_EXPERTS,), 0.01)
    for i in range(DEPTH):
        pfx = 'l%d_' % i
        inp[pfx + 'ada_w'] = normal((d, N_MOD * d), 0.5 * d ** -0.5)
        inp[pfx + 'ada_b'] = normal((N_MOD * d,), 0.01)
        inp[pfx + 'norm_pre_mix'] = gain(d)
        inp[pfx + 'norm_post_mix'] = gain(d)
        inp[pfx + 'norm_pre_ffn'] = gain(d)
        inp[pfx + 'norm_post_ffn'] = gain(d)
        if i % N_MIXERS == 0:
            inp[pfx + 'na_w_qkv'] = normal((d, 3 * NA_HEADS * HEAD_DIM), d ** -0.5)
            inp[pfx + 'na_rpb'] = normal((NA_HEADS, 2 * NA_WIN_H - 1, 2 * NA_WIN_W - 1), 0.1)
            inp[pfx + 'na_w_o'] = normal((NA_HEADS * HEAD_DIM, d), (NA_HEADS * HEAD_DIM) ** -0.5)
        else:
            inp[pfx + 'gqa_w_qkv'] = normal((d, (GQA_Q_HEADS + 2 * GQA_KV_HEADS) * HEAD_DIM), d ** -0.5)
            inp[pfx + 'gqa_q_gain'] = gain(HEAD_DIM)
            inp[pfx + 'gqa_k_gain'] = gain(HEAD_DIM)
            inp[pfx + 'gqa_w_o'] = normal((GQA_Q_HEADS * HEAD_DIM, d), (GQA_Q_HEADS * HEAD_DIM) ** -0.5)
        inp[pfx + 'moe_w_gate'] = normal((N_EXPERTS, d, D_FF), d ** -0.5)
        inp[pfx + 'moe_w_up'] = normal((N_EXPERTS, d, D_FF), d ** -0.5)
        inp[pfx + 'moe_w_down'] = normal((N_EXPERTS, D_FF, d), D_FF ** -0.5)
    return inp


def reference(x, c, ctx, c_ctx, router_w, router_b,
              l0_ada_w, l0_ada_b, l0_norm_pre_mix, l0_norm_post_mix, l0_norm_pre_ffn, l0_norm_post_ffn,
              l0_na_w_qkv, l0_na_rpb, l0_na_w_o, l0_moe_w_gate, l0_moe_w_up, l0_moe_w_down,
              l1_ada_w, l1_ada_b, l1_norm_pre_mix, l1_norm_post_mix, l1_norm_pre_ffn, l1_norm_post_ffn,
              l1_gqa_w_qkv, l1_gqa_q_gain, l1_gqa_k_gain, l1_gqa_w_o, l1_moe_w_gate, l1_moe_w_up, l1_moe_w_down):
    layers = (
        dict(ada_w=l0_ada_w, ada_b=l0_ada_b, norm_pre_mix=l0_norm_pre_mix, norm_post_mix=l0_norm_post_mix,
             norm_pre_ffn=l0_norm_pre_ffn, norm_post_ffn=l0_norm_post_ffn,
             mixer=(l0_na_w_qkv, l0_na_rpb, l0_na_w_o), moe=(l0_moe_w_gate, l0_moe_w_up, l0_moe_w_down)),
        dict(ada_w=l1_ada_w, ada_b=l1_ada_b, norm_pre_mix=l1_norm_pre_mix, norm_post_mix=l1_norm_post_mix,
             norm_pre_ffn=l1_norm_pre_ffn, norm_post_ffn=l1_norm_post_ffn,
             mixer=(l1_gqa_w_qkv, l1_gqa_q_gain, l1_gqa_k_gain, l1_gqa_w_o),
             moe=(l1_moe_w_gate, l1_moe_w_up, l1_moe_w_down)),
    )
    d = x.shape[-1]
    silu_c = jax.nn.silu(c)
    silu_cc = jax.nn.silu(c_ctx)
    xc = ctx
    for i in range(DEPTH):
        p = layers[i]
        ctx_out = i < DEPTH - 1
        mod = silu_c @ p['ada_w'] + p['ada_b']
        mod_c = silu_cc @ p['ada_w'] + p['ada_b']
        sh_m, sc_m, g_m, sh_f, sc_f, g_f = [m[:, None, :] for m in jnp.split(mod, N_MOD, axis=-1)]
        csh_m, csc_m, cg_m, csh_f, csc_f, cg_f = jnp.split(mod_c, N_MOD, axis=-1)
        h = modulate(rms_norm(x, p['norm_pre_mix']), sh_m, sc_m)
        hc = modulate(rms_norm(xc, p['norm_pre_mix']), csh_m, csc_m)
        if i % N_MIXERS == 0:
            y, yc = neighbourhood_attention(h, hc, *p['mixer'], ctx_out)
        else:
            y, yc = grouped_query_attention(h, hc, *p['mixer'], ctx_out)
        x = x + g_m * rms_norm(y, p['norm_post_mix'])
        if ctx_out:
            xc = xc + cg_m * rms_norm(yc, p['norm_post_mix'])
        h = modulate(rms_norm(x, p['norm_pre_ffn']), sh_f, sc_f).reshape(-1, d)
        if ctx_out:
            hc = modulate(rms_norm(xc, p['norm_pre_ffn']), csh_f, csc_f).reshape(-1, d)
            yy = grouped_moe(jnp.concatenate([h, hc], axis=0), router_w, router_b, *p['moe'])
            y, yc = yy[:h.shape[0]], yy[h.shape[0]:]
            xc = xc + cg_f * rms_norm(yc.reshape(xc.shape), p['norm_post_ffn'])
        else:
            y = grouped_moe(h, router_w, router_b, *p['moe'])
        x = x + g_f * rms_norm(y.reshape(x.shape), p['norm_post_ffn'])
    return x
```

```python
import functools

import jax
import jax.numpy as jnp
from jax import lax
from jax.experimental import pallas as pl
from jax.experimental.pallas import tpu as pltpu

GRID_W = 64
HEAD_DIM = 128
NA_WIN_H = 8
NA_WIN_W = 16
GQA_REP = 4
ROPE_THETA = 10000.0
N_GROUPS = 4
TOP_K = 2
EPS = 1e-6
NEG_INF = -1e30

F32 = jnp.float32
BF16 = jnp.bfloat16

ROW_TILE = 256
MM_TM = 512
MM_TN = 512
MOE_TM = 256
MOE_FC = 512
MOE_NC = 2048
NA_RB = 4
NA_HG = 8
GQA_QB = 256
ADA_TN = 512
VMEM_LIMIT = 56 * 1024 * 1024


def _params(sem):
    return pltpu.CompilerParams(dimension_semantics=sem, vmem_limit_bytes=VMEM_LIMIT)


def _adaln_body(c_ref, w_ref, b_ref, o_ref):
    c = c_ref[...]
    s = (c * jax.nn.sigmoid(c)).astype(BF16)
    o_ref[...] = jnp.dot(s, w_ref[...].astype(BF16), preferred_element_type=F32) + b_ref[...]


def _adaln(cvec, ada_w, ada_b):
    m, d = cvec.shape
    n = ada_w.shape[1]
    return pl.pallas_call(
        _adaln_body,
        grid=(n // ADA_TN,),
        in_specs=[pl.BlockSpec((m, d), lambda j: (0, 0)),
                  pl.BlockSpec((d, ADA_TN), lambda j: (0, j)),
                  pl.BlockSpec((1, ADA_TN), lambda j: (0, j))],
        out_specs=pl.BlockSpec((m, ADA_TN), lambda j: (0, j)),
        out_shape=jax.ShapeDtypeStruct((m, n), F32),
        compiler_params=_params(("arbitrary",)),
        name="adaln",
    )(cvec, ada_w, ada_b.reshape(1, n))


def _rms(x, gain):
    return x * lax.rsqrt(jnp.mean(x * x, axis=-1, keepdims=True) + EPS) * gain


def _norm_mod_body(x_ref, g_ref, sh_ref, sc_ref, o_ref):
    y = _rms(x_ref[...], g_ref[...])
    o_ref[...] = (y * (1.0 + sc_ref[0]) + sh_ref[0]).astype(o_ref.dtype)


def _mod_row_map(n_lat_tiles, tiles_per_batch, n_batch, chunk):
    def index_map(i):
        return (jnp.where(i < n_lat_tiles, i // tiles_per_batch, n_batch), 0, chunk)
    return index_map


def _norm_mod(x, gain, mod, shift_chunk, scale_chunk, n_lat, seq, out_dtype):
    n, d = x.shape
    n_batch = mod.shape[0] - 1
    mk = functools.partial(_mod_row_map, n_lat // ROW_TILE, seq // ROW_TILE, n_batch)
    return pl.pallas_call(
        _norm_mod_body,
        grid=(n // ROW_TILE,),
        in_specs=[pl.BlockSpec((ROW_TILE, d), lambda i: (i, 0)),
                  pl.BlockSpec((1, d), lambda i: (0, 0)),
                  pl.BlockSpec((1, 1, d), mk(shift_chunk)),
                  pl.BlockSpec((1, 1, d), mk(scale_chunk))],
        out_specs=pl.BlockSpec((ROW_TILE, d), lambda i: (i, 0)),
        out_shape=jax.ShapeDtypeStruct((n, d), out_dtype),
        compiler_params=_params(("arbitrary",)),
        name="norm_mod",
    )(x, gain.reshape(1, d), mod, mod)


def _residual_body(has_next, x_ref, y_ref, g_ref, gate_ref, *rest):
    if has_next:
        g2_ref, sh_ref, sc_ref, xo_ref, ho_ref = rest
    else:
        (xo_ref,) = rest
    xn = x_ref[...] + gate_ref[0] * _rms(y_ref[...], g_ref[...])
    xo_ref[...] = xn
    if has_next:
        h = _rms(xn, g2_ref[...])
        ho_ref[...] = (h * (1.0 + sc_ref[0]) + sh_ref[0]).astype(ho_ref.dtype)


def _residual(x, y, gain, mod, gate_chunk, n_lat, seq, nxt=None):
    n, d = y.shape
    n_batch = mod.shape[0] - 1
    mk = functools.partial(_mod_row_map, n_lat // ROW_TILE, seq // ROW_TILE, n_batch)
    row = pl.BlockSpec((ROW_TILE, d), lambda i: (i, 0))
    vec = pl.BlockSpec((1, d), lambda i: (0, 0))
    in_specs = [row, row, vec, pl.BlockSpec((1, 1, d), mk(gate_chunk))]
    args = [x, y, gain.reshape(1, d), mod]
    out_specs = [row]
    out_shape = [jax.ShapeDtypeStruct((n, d), F32)]
    if nxt is not None:
        gain2, mod2, shift_chunk, scale_chunk, h_dtype = nxt
        in_specs += [vec, pl.BlockSpec((1, 1, d), mk(shift_chunk)), pl.BlockSpec((1, 1, d), mk(scale_chunk))]
        args += [gain2.reshape(1, d), mod2, mod2]
        out_specs.append(row)
        out_shape.append(jax.ShapeDtypeStruct((n, d), h_dtype))
    out = pl.pallas_call(
        functools.partial(_residual_body, nxt is not None),
        grid=(n // ROW_TILE,),
        in_specs=in_specs,
        out_specs=out_specs,
        out_shape=out_shape,
        compiler_params=_params(("arbitrary",)),
        name="residual",
    )(*args)
    return out if nxt is not None else out[0]


def _matmul_body(a_ref, w_ref, s_ref, o_ref, wb_ref):
    @pl.when(pl.program_id(1) == 0)
    def _():
        wb_ref[...] = w_ref[...].astype(BF16)

    acc = jnp.dot(a_ref[...], wb_ref[...], preferred_element_type=F32)
    o_ref[...] = (acc * s_ref[...]).astype(o_ref.dtype)


def _matmul(a, w, col_scale, out_dtype):
    m, k = a.shape
    n = w.shape[1]
    return pl.pallas_call(
        _matmul_body,
        grid=(n // MM_TN, m // MM_TM),
        in_specs=[pl.BlockSpec((MM_TM, k), lambda j, i: (i, 0)),
                  pl.BlockSpec((k, MM_TN), lambda j, i: (0, j)),
                  pl.BlockSpec((1, MM_TN), lambda j, i: (0, j))],
        out_specs=pl.BlockSpec((MM_TM, MM_TN), lambda j, i: (i, j)),
        out_shape=jax.ShapeDtypeStruct((m, n), out_dtype),
        scratch_shapes=[pltpu.VMEM((k, MM_TN), BF16)],
        compiler_params=_params(("arbitrary", "arbitrary")),
        name="matmul",
    )(a, w, col_scale)


def _softmax_pv(s_parts, v_parts):
    m = s_parts[0].max(axis=-1, keepdims=True)
    for s in s_parts[1:]:
        m = jnp.maximum(m, s.max(axis=-1, keepdims=True))
    den = None
    acc = None
    for s, v in zip(s_parts, v_parts):
        p = jnp.exp(s - m)
        l = p.sum(axis=-1, keepdims=True)
        o = jnp.dot(p.astype(BF16), v, preferred_element_type=F32)
        den = l if den is None else den + l
        acc = o if acc is None else acc + o
    return acc / den


def _qkt(q, k):
    return lax.dot_general(q, k, (((1,), (1,)), ((), ())), preferred_element_type=F32)


def _na_body(n_row_blocks, rows, q_ref, k_ref, v_ref, kc_ref, vc_ref, b_ref, o_ref):
    rb = pl.program_id(2)
    kh = min(NA_WIN_H, rows)
    band = kh * GRID_W

    @pl.when(rb < n_row_blocks)
    def _():
        def one_row(i, carry):
            r = rb * NA_RB + i
            rs = jnp.clip(r - kh // 2, 0, rows - kh)
            didx = rs - r + NA_WIN_H - 1
            k0 = pl.multiple_of(rs * GRID_W, GRID_W)
            q0 = pl.multiple_of(i * GRID_W, GRID_W)
            for g in range(NA_HG):
                cols = slice(g * HEAD_DIM, (g + 1) * HEAD_DIM)
                q = q_ref[pl.ds(q0, GRID_W), cols]
                s_loc = _qkt(q, k_ref[pl.ds(k0, band), cols]) + b_ref[didx, g]
                s_ctx = _qkt(q, kc_ref[:, cols])
                o = _softmax_pv([s_loc, s_ctx], [v_ref[pl.ds(k0, band), cols], vc_ref[:, cols]])
                o_ref[pl.ds(q0, GRID_W), cols] = o.astype(o_ref.dtype)
            return carry

        lax.fori_loop(0, NA_RB, one_row, 0)

    @pl.when(rb >= n_row_blocks)
    def _():
        for g in range(NA_HG):
            cols = slice(g * HEAD_DIM, (g + 1) * HEAD_DIM)
            s = _qkt(q_ref[:, cols], kc_ref[:, cols])
            o_ref[:, cols] = _softmax_pv([s], [vc_ref[:, cols]]).astype(o_ref.dtype)


def _na_bias_table(rpb, rows):
    kh = min(NA_WIN_H, rows)
    qcol = jnp.arange(GRID_W)
    col_start = jnp.clip(qcol - NA_WIN_W // 2, 0, GRID_W - NA_WIN_W)
    col_valid = (qcol[None, :] >= col_start[:, None]) & (qcol[None, :] < col_start[:, None] + NA_WIN_W)
    dc_idx = jnp.clip(qcol[None, :] - qcol[:, None], -(NA_WIN_W - 1), NA_WIN_W - 1) + NA_WIN_W - 1
    dr = jnp.arange(NA_WIN_H)[:, None] + jnp.arange(kh)[None, :]
    bias = rpb[:, dr[:, None, :, None], dc_idx[None, :, None, :]].astype(F32)
    bias = jnp.where(col_valid[None, None, :, None, :], bias, NEG_INF)
    bias = jnp.moveaxis(bias, 0, 1)
    return bias.reshape(NA_WIN_H, rpb.shape[0], GRID_W, kh * GRID_W)


def _na_attention(qkv, bias, n_batch, seq, ctx_len):
    n, d3 = qkv.shape
    d = d3 // 3
    heads = d // HEAD_DIM
    rows = seq // GRID_W
    qb = NA_RB * GRID_W
    assert seq % qb == 0 and ctx_len % qb == 0 and heads % NA_HG == 0
    n_row_blocks = seq // qb
    n_ctx_blocks = ctx_len // qb
    hgs = heads // NA_HG
    w = NA_HG * HEAD_DIM
    lat_blocks = n_batch * n_row_blocks

    def q_map(hg, b, rb):
        blk = jnp.where(rb < n_row_blocks, b * n_row_blocks + rb, lat_blocks + b * n_ctx_blocks + rb - n_row_blocks)
        return (blk, hg)

    ctx_blk0 = (n_batch * seq) // ctx_len
    return pl.pallas_call(
        functools.partial(_na_body, n_row_blocks, rows),
        grid=(hgs, n_batch, n_row_blocks + n_ctx_blocks),
        in_specs=[pl.BlockSpec((qb, w), q_map),
                  pl.BlockSpec((seq, w), lambda hg, b, rb: (b, hgs + hg)),
                  pl.BlockSpec((seq, w), lambda hg, b, rb: (b, 2 * hgs + hg)),
                  pl.BlockSpec((ctx_len, w), lambda hg, b, rb: (ctx_blk0 + b, hgs + hg)),
                  pl.BlockSpec((ctx_len, w), lambda hg, b, rb: (ctx_blk0 + b, 2 * hgs + hg)),
                  pl.BlockSpec((NA_WIN_H, NA_HG, GRID_W, bias.shape[-1]), lambda hg, b, rb: (0, hg, 0, 0))],
        out_specs=pl.BlockSpec((qb, w), q_map),
        out_shape=jax.ShapeDtypeStruct((n, d), BF16),
        compiler_params=_params(("arbitrary", "arbitrary", "arbitrary")),
        name="na_attention",
    )(qkv, qkv, qkv, qkv, qkv, bias)


def _rope_tables(seq, n_rows):
    t = jnp.arange(seq, dtype=jnp.int32)
    row = (t // GRID_W).astype(F32)
    col = (t % GRID_W).astype(F32)
    axis_dim = HEAD_DIM // 2
    inv_freq = 1.0 / (ROPE_THETA ** (jnp.arange(0, axis_dim, 2, dtype=F32) / axis_dim))
    ang = jnp.concatenate([row[:, None] * inv_freq, col[:, None] * inv_freq], axis=-1)
    cos = jnp.repeat(jnp.cos(ang), 2, axis=-1)
    sin = jnp.repeat(jnp.sin(ang), 2, axis=-1) * jnp.tile(jnp.array([-1.0, 1.0], F32), HEAD_DIM // 2)
    reps = n_rows // seq
    return jnp.tile(cos, (reps, 1)), jnp.tile(sin, (reps, 1))


def _qk_prep_body(n_heads_blk, x_ref, cos_ref, sin_ref, g_ref, s_ref, o_ref):
    cos = cos_ref[...]
    sin = sin_ref[...]
    gain = g_ref[0]
    post = s_ref[0]
    even = (lax.broadcasted_iota(jnp.int32, cos.shape, 1) % 2) == 0
    for h in range(n_heads_blk):
        cols = slice(h * HEAD_DIM, (h + 1) * HEAD_DIM)
        y = _rms(x_ref[:, cols].astype(F32), gain)
        partner = jnp.where(even, pltpu.roll(y, HEAD_DIM - 1, 1), pltpu.roll(y, 1, 1))
        o_ref[:, cols] = ((y * cos + partner * sin) * post).astype(o_ref.dtype)


def _qk_prep(qkv, cos, sin, gains, posts, n_qk_cols, q_cols):
    n = qkv.shape[0]
    blk_heads = min(8, (n_qk_cols - q_cols) // HEAD_DIM)
    w = blk_heads * HEAD_DIM
    assert q_cols % w == 0 and n_qk_cols % w == 0
    q_blocks = q_cols // w

    def sel(i, j):
        return (jnp.where(j < q_blocks, 0, 1), 0, 0)

    return pl.pallas_call(
        functools.partial(_qk_prep_body, blk_heads),
        grid=(n // ROW_TILE, n_qk_cols // w),
        in_specs=[pl.BlockSpec((ROW_TILE, w), lambda i, j: (i, j)),
                  pl.BlockSpec((ROW_TILE, HEAD_DIM), lambda i, j: (i, 0)),
                  pl.BlockSpec((ROW_TILE, HEAD_DIM), lambda i, j: (i, 0)),
                  pl.BlockSpec((1, 1, HEAD_DIM), sel),
                  pl.BlockSpec((1, 1, HEAD_DIM), sel)],
        out_specs=pl.BlockSpec((ROW_TILE, w), lambda i, j: (i, j)),
        out_shape=jax.ShapeDtypeStruct((n, n_qk_cols), BF16),
        compiler_params=_params(("arbitrary", "arbitrary")),
        name="qk_prep",
    )(qkv, cos, sin, gains, posts)


def _gqa_body(q_ref, k_ref, kc_ref, v_ref, vc_ref, o_ref):
    k = k_ref[...]
    kc = kc_ref[...]
    v = v_ref[...]
    vc = vc_ref[...]
    for r in range(GQA_REP):
        cols = slice(r * HEAD_DIM, (r + 1) * HEAD_DIM)
        q = q_ref[:, cols]
        o = _softmax_pv([_qkt(q, k), _qkt(q, kc)], [v, vc])
        o_ref[:, cols] = o.astype(o_ref.dtype)


def _gqa_attention(qk, qkv, n_batch, seq, ctx_len, q_cols):
    kv_heads = q_cols // HEAD_DIM // GQA_REP
    assert seq % GQA_QB == 0 and (n_batch * seq) % ctx_len == 0
    w = GQA_REP * HEAD_DIM
    qblocks = seq // GQA_QB
    k_blk0 = q_cols // HEAD_DIM
    v_blk0 = k_blk0 + kv_heads
    ctx_blk0 = (n_batch * seq) // ctx_len
    return pl.pallas_call(
        _gqa_body,
        grid=(n_batch, kv_heads, qblocks),
        in_specs=[pl.BlockSpec((GQA_QB, w), lambda b, h, i: (b * qblocks + i, h)),
                  pl.BlockSpec((seq, HEAD_DIM), lambda b, h, i: (b, k_blk0 + h)),
                  pl.BlockSpec((ctx_len, HEAD_DIM), lambda b, h, i: (ctx_blk0 + b, k_blk0 + h)),
                  pl.BlockSpec((seq, HEAD_DIM), lambda b, h, i: (b, v_blk0 + h)),
                  pl.BlockSpec((ctx_len, HEAD_DIM), lambda b, h, i: (ctx_blk0 + b, v_blk0 + h))],
        out_specs=pl.BlockSpec((GQA_QB, w), lambda b, h, i: (b * qblocks + i, h)),
        out_shape=jax.ShapeDtypeStruct((n_batch * seq, q_cols), BF16),
        compiler_params=_params(("arbitrary", "arbitrary", "arbitrary")),
        name="gqa_attention",
    )(qk, qk, qk, qkv, qkv)


def _split_bf16(x):
    hi = x.astype(BF16)
    lo = (x - hi.astype(F32)).astype(BF16)
    return hi, lo


def _router_body(n_experts, h_ref, w_ref, b_ref, id_ref, wt_ref):
    h_hi, h_lo = _split_bf16(h_ref[...])
    w_hi, w_lo = _split_bf16(w_ref[...])
    logits = (jnp.dot(h_hi, w_hi, preferred_element_type=F32)
              + jnp.dot(h_lo, w_hi, preferred_element_type=F32)
              + jnp.dot(h_hi, w_lo, preferred_element_type=F32))
    lt = logits.T[:n_experts] + b_ref[...]
    rows = [lt[e:e + 1] for e in range(n_experts)]
    mx = functools.reduce(jnp.maximum, rows)
    ex = [jnp.exp(r - mx) for r in rows]
    den = functools.reduce(jnp.add, ex)
    probs = [e / den for e in ex]

    epg = n_experts // N_GROUPS
    best = None
    for g in range(N_GROUPS):
        p = probs[g * epg:(g + 1) * epg]
        top1 = functools.reduce(jnp.maximum, p)
        i1 = jnp.full(top1.shape, epg, jnp.int32)
        for j in reversed(range(epg)):
            i1 = jnp.where(p[j] == top1, j, i1)
        rest = [jnp.where(i1 == j, -1.0, p[j]) for j in range(epg)]
        top2 = functools.reduce(jnp.maximum, rest)
        i2 = jnp.full(top1.shape, epg, jnp.int32)
        for j in reversed(range(epg)):
            i2 = jnp.where((rest[j] == top2) & (i1 != j), j, i2)
        cand = (top1 + top2, top1, top2, i1 + g * epg, i2 + g * epg)
        if best is None:
            best = cand
        else:
            take = cand[0] > best[0]
            best = tuple(jnp.where(take, c, b) for c, b in zip(cand, best))
    score, top1, top2, e1, e2 = best
    id_ref[...] = jnp.zeros(id_ref.shape, jnp.int32)
    wt_ref[...] = jnp.zeros(wt_ref.shape, F32)
    id_ref[0:1, :] = e1
    id_ref[1:2, :] = e2
    wt_ref[0:1, :] = top1 / score
    wt_ref[1:2, :] = top2 / score


def _router(h, router_w, router_b):
    n, d = h.shape
    e = router_w.shape[1]
    tm = 512
    w_pad = jnp.pad(router_w, ((0, 0), (0, 128 - e)))
    return pl.pallas_call(
        functools.partial(_router_body, e),
        grid=(n // tm,),
        in_specs=[pl.BlockSpec((tm, d), lambda i: (i, 0)),
                  pl.BlockSpec((d, 128), lambda i: (0, 0)),
                  pl.BlockSpec((e, 1), lambda i: (0, 0))],
        out_specs=[pl.BlockSpec((8, tm), lambda i: (0, i)), pl.BlockSpec((8, tm), lambda i: (0, i))],
        out_shape=[jax.ShapeDtypeStruct((8, n), jnp.int32), jax.ShapeDtypeStruct((8, n), F32)],
        compiler_params=_params(("arbitrary",)),
        name="router",
    )(h, w_pad, router_b.reshape(e, 1).astype(F32))


def _dispatch_plan(ids, n_experts, n_tiles):
    e0, e1 = ids[0], ids[1]
    n = e0.shape[0]
    ar = jnp.arange(n_experts, dtype=jnp.int32)[:, None]
    oh0 = (e0[None, :] == ar).astype(jnp.int32)
    oh1 = (e1[None, :] == ar).astype(jnp.int32)
    sel = oh0 + oh1
    csum = jnp.cumsum(sel, axis=1)
    counts = csum[:, -1]
    padded = ((counts + MOE_TM - 1) // MOE_TM) * MOE_TM
    ends = jnp.cumsum(padded)
    offs = ends - padded
    slot = offs[:, None] + csum - 1
    pos0 = jnp.sum(oh0 * slot, axis=0)
    pos1 = jnp.sum(oh1 * slot, axis=0)
    tok = jnp.arange(n, dtype=jnp.int32)
    src = jnp.zeros((n_tiles * MOE_TM,), jnp.int32)
    src = src.at[jnp.concatenate([pos0, pos1])].set(jnp.concatenate([tok, tok]), unique_indices=True)
    tile_start = jnp.arange(n_tiles, dtype=jnp.int32) * MOE_TM
    tile_expert = jnp.sum((ends[None, :] <= tile_start[:, None]).astype(jnp.int32), axis=1)
    tile_expert = jnp.minimum(tile_expert, n_experts - 1)
    n_used = (ends[-1] // MOE_TM).astype(jnp.int32).reshape(1)
    return src, jnp.stack([pos0, pos1]), tile_expert, n_used


def _row_copy(src_hbm, row, dst, k, sem):
    return pltpu.make_async_copy(src_hbm.at[pl.ds(row, 1), :], dst.at[pl.ds(k, 1), :], sem)


def _gather_body(idx_ref, h_hbm, o_ref, buf, sem):
    i = pl.program_id(0)
    n = pl.num_programs(0)

    def issue(step, slot):
        def one(k, carry):
            _row_copy(h_hbm, idx_ref[step * ROW_TILE + k], buf.at[slot], k, sem.at[slot]).start()
            return carry
        lax.fori_loop(0, ROW_TILE, one, 0, unroll=8)

    @pl.when(i == 0)
    def _():
        issue(0, 0)

    @pl.when(i + 1 < n)
    def _():
        issue(i + 1, (i + 1) % 2)

    slot = i % 2

    def wait_one(k, carry):
        _row_copy(h_hbm, 0, buf.at[slot], k, sem.at[slot]).wait()
        return carry
    lax.fori_loop(0, ROW_TILE, wait_one, 0, unroll=8)
    o_ref[...] = buf[slot].astype(o_ref.dtype)


def _gather_rows(h, src):
    n, d = h.shape
    r = src.shape[0]
    return pl.pallas_call(
        _gather_body,
        grid_spec=pltpu.PrefetchScalarGridSpec(
            num_scalar_prefetch=1,
            grid=(r // ROW_TILE,),
            in_specs=[pl.BlockSpec(memory_space=pl.ANY)],
            out_specs=pl.BlockSpec((ROW_TILE, d), lambda i, idx: (i, 0)),
            scratch_shapes=[pltpu.VMEM((2, ROW_TILE, d), F32), pltpu.SemaphoreType.DMA((2,))]),
        out_shape=jax.ShapeDtypeStruct((r, d), BF16),
        compiler_params=_params(("arbitrary",)),
        name="moe_gather",
    )(src, h)


def _expert_changed(te_ref, t):
    return (t == 0) | (te_ref[t] != te_ref[jnp.maximum(t - 1, 0)])


def _moe_up_body(te_ref, nu_ref, x_ref, wg_ref, wu_ref, o_ref, wgb, wub):
    t = pl.program_id(1)

    @pl.when(_expert_changed(te_ref, t))
    def _():
        wgb[...] = wg_ref[0].astype(BF16)
        wub[...] = wu_ref[0].astype(BF16)

    @pl.when(t < nu_ref[0])
    def _():
        x = x_ref[...]
        g = jnp.dot(x, wgb[...], preferred_element_type=F32)
        u = jnp.dot(x, wub[...], preferred_element_type=F32)
        o_ref[...] = (g * jax.nn.sigmoid(g) * u).astype(o_ref.dtype)

    @pl.when(t >= nu_ref[0])
    def _():
        o_ref[...] = jnp.zeros(o_ref.shape, o_ref.dtype)


def _moe_up(xs, w_gate, w_up, tile_expert, n_used):
    r, d = xs.shape
    f = w_gate.shape[2]
    n_tiles = r // MOE_TM

    def x_map(j, t, te, nu):
        return (jnp.minimum(t, nu[0] - 1), 0)

    def w_map(j, t, te, nu):
        return (te[t], 0, j)

    return pl.pallas_call(
        _moe_up_body,
        grid_spec=pltpu.PrefetchScalarGridSpec(
            num_scalar_prefetch=2,
            grid=(f // MOE_FC, n_tiles),
            in_specs=[pl.BlockSpec((MOE_TM, d), x_map),
                      pl.BlockSpec((1, d, MOE_FC), w_map),
                      pl.BlockSpec((1, d, MOE_FC), w_map)],
            out_specs=pl.BlockSpec((MOE_TM, MOE_FC), lambda j, t, te, nu: (t, j)),
            scratch_shapes=[pltpu.VMEM((d, MOE_FC), BF16), pltpu.VMEM((d, MOE_FC), BF16)]),
        out_shape=jax.ShapeDtypeStruct((r, f), BF16),
        compiler_params=_params(("arbitrary", "arbitrary")),
        name="moe_up",
    )(tile_expert, n_used, xs, w_gate, w_up)


def _moe_down_body(te_ref, nu_ref, h_ref, w_ref, o_ref, wb):
    t = pl.program_id(1)

    @pl.when(_expert_changed(te_ref, t))
    def _():
        wb[...] = w_ref[0].astype(BF16)

    @pl.when(t < nu_ref[0])
    def _():
        o_ref[...] = jnp.dot(h_ref[...], wb[...], preferred_element_type=F32)

    @pl.when(t >= nu_ref[0])
    def _():
        o_ref[...] = jnp.zeros(o_ref.shape, o_ref.dtype)


def _moe_down(hs, w_down, tile_expert, n_used):
    r, f = hs.shape
    d = w_down.shape[2]
    nc = min(MOE_NC, d)
    n_tiles = r // MOE_TM
    return pl.pallas_call(
        _moe_down_body,
        grid_spec=pltpu.PrefetchScalarGridSpec(
            num_scalar_prefetch=2,
            grid=(d // nc, n_tiles),
            in_specs=[pl.BlockSpec((MOE_TM, f), lambda j, t, te, nu: (jnp.minimum(t, nu[0] - 1), 0)),
                      pl.BlockSpec((1, f, nc), lambda j, t, te, nu: (te[t], 0, j))],
            out_specs=pl.BlockSpec((MOE_TM, nc), lambda j, t, te, nu: (t, j)),
            scratch_shapes=[pltpu.VMEM((f, nc), BF16)]),
        out_shape=jax.ShapeDtypeStruct((r, d), F32),
        compiler_params=_params(("arbitrary", "arbitrary")),
        name="moe_down",
    )(tile_expert, n_used, hs, w_down)


def _combine_body(has_next, pos_ref, y_hbm, wt_ref, x_ref, g_ref, gate_ref, *rest):
    if has_next:
        g2_ref, sh_ref, sc_ref, xo_ref, ho_ref, buf, sem = rest
    else:
        xo_ref, buf, sem = rest
    i = pl.program_id(0)
    n = pl.num_programs(0)
    n_tok = n * ROW_TILE

    def issue(step, slot):
        def one(k, carry):
            tok = step * ROW_TILE + k
            _row_copy(y_hbm, pos_ref[tok], buf.at[slot, 0], k, sem.at[slot]).start()
            _row_copy(y_hbm, pos_ref[n_tok + tok], buf.at[slot, 1], k, sem.at[slot]).start()
            return carry
        lax.fori_loop(0, ROW_TILE, one, 0, unroll=8)

    @pl.when(i == 0)
    def _():
        issue(0, 0)

    @pl.when(i + 1 < n)
    def _():
        issue(i + 1, (i + 1) % 2)

    slot = i % 2

    def wait_one(k, carry):
        _row_copy(y_hbm, 0, buf.at[slot, 0], k, sem.at[slot]).wait()
        _row_copy(y_hbm, 0, buf.at[slot, 1], k, sem.at[slot]).wait()
        return carry
    lax.fori_loop(0, ROW_TILE, wait_one, 0, unroll=8)

    wt = wt_ref[...]
    y = wt[:, 0:1] * buf[slot, 0] + wt[:, 1:2] * buf[slot, 1]
    xn = x_ref[...] + gate_ref[0] * _rms(y, g_ref[...])
    xo_ref[...] = xn
    if has_next:
        h = _rms(xn, g2_ref[...])
        ho_ref[...] = (h * (1.0 + sc_ref[0]) + sh_ref[0]).astype(ho_ref.dtype)


def _combine(ys, pos, wts, x, gain, mod, gate_chunk, n_lat, seq, nxt=None):
    n = wts.shape[0]
    d = ys.shape[1]
    n_batch = mod.shape[0] - 1
    n_lat_tiles = min(n_lat, n) // ROW_TILE

    def mk(chunk):
        def index_map(i, pos_ref):
            return (jnp.where(i < n_lat_tiles, i // (seq // ROW_TILE), n_batch), 0, chunk)
        return index_map

    row = pl.BlockSpec((ROW_TILE, d), lambda i, p: (i, 0))
    vec = pl.BlockSpec((1, d), lambda i, p: (0, 0))
    in_specs = [pl.BlockSpec(memory_space=pl.ANY),
                pl.BlockSpec((ROW_TILE, 2), lambda i, p: (i, 0)),
                row, vec, pl.BlockSpec((1, 1, d), mk(gate_chunk))]
    args = [ys, wts, x, gain.reshape(1, d), mod]
    out_specs = [row]
    out_shape = [jax.ShapeDtypeStruct((n, d), F32)]
    if nxt is not None:
        gain2, mod2, shift_chunk, scale_chunk, h_dtype = nxt
        in_specs += [vec, pl.BlockSpec((1, 1, d), mk(shift_chunk)), pl.BlockSpec((1, 1, d), mk(scale_chunk))]
        args += [gain2.reshape(1, d), mod2, mod2]
        out_specs.append(row)
        out_shape.append(jax.ShapeDtypeStruct((n, d), h_dtype))
    out = pl.pallas_call(
        functools.partial(_combine_body, nxt is not None),
        grid_spec=pltpu.PrefetchScalarGridSpec(
            num_scalar_prefetch=1,
            grid=(n // ROW_TILE,),
            in_specs=in_specs,
            out_specs=out_specs,
            scratch_shapes=[pltpu.VMEM((2, 2, ROW_TILE, d), F32), pltpu.SemaphoreType.DMA((2,))]),
        out_shape=out_shape,
        compiler_params=_params(("arbitrary",)),
        name="moe_combine",
    )(pos.reshape(-1), *args)
    return out if nxt is not None else out[0]


def _moe(h, router_w, router_b, w_gate, w_up, w_down):
    n = h.shape[0]
    n_experts = router_w.shape[1]
    n_tiles = (TOP_K * n) // MOE_TM + n_experts
    ids, wts = _router(h, router_w, router_b)
    src, pos, tile_expert, n_used = _dispatch_plan(ids, n_experts, n_tiles)
    xs = _gather_rows(h, src)
    hs = _moe_up(xs, w_gate, w_up, tile_expert, n_used)
    ys = _moe_down(hs, w_down, tile_expert, n_used)
    return ys, pos, wts[:2].T


def kernel(x, c, ctx, c_ctx, router_w, router_b, l0_ada_w, l0_ada_b, l0_norm_pre_mix, l0_norm_post_mix, l0_norm_pre_ffn, l0_norm_post_ffn, l0_na_w_qkv, l0_na_rpb, l0_na_w_o, l0_moe_w_gate, l0_moe_w_up, l0_moe_w_down, l1_ada_w, l1_ada_b, l1_norm_pre_mix, l1_norm_post_mix, l1_norm_pre_ffn, l1_norm_post_ffn, l1_gqa_w_qkv, l1_gqa_q_gain, l1_gqa_k_gain, l1_gqa_w_o, l1_moe_w_gate, l1_moe_w_up, l1_moe_w_down):
    n_batch, seq, d = x.shape
    ctx_len = ctx.shape[1]
    n_lat = n_batch * seq
    n_all = n_lat + n_batch * ctx_len
    scale = HEAD_DIM ** -0.5
    SH_M, SC_M, G_M, SH_F, SC_F, G_F = range(6)

    cvec = jnp.zeros((8, d), F32).at[:n_batch].set(c).at[n_batch].set(c_ctx)
    mod0 = _adaln(cvec, l0_ada_w, l0_ada_b)[:n_batch + 1].reshape(n_batch + 1, 1, 6 * d)
    mod1 = _adaln(cvec, l1_ada_w, l1_ada_b)[:n_batch + 1].reshape(n_batch + 1, 1, 6 * d)

    xa = jnp.concatenate([x.reshape(n_lat, d), ctx.reshape(n_batch * ctx_len, d)], axis=0)

    h = _norm_mod(xa, l0_norm_pre_mix, mod0, SH_M, SC_M, n_lat, seq, BF16)
    qscale = jnp.concatenate([jnp.full((1, d), scale, F32), jnp.ones((1, 2 * d), F32)], axis=1)
    qkv = _matmul(h, l0_na_w_qkv, qscale, BF16)
    bias = _na_bias_table(l0_na_rpb, seq // GRID_W)
    att = _na_attention(qkv, bias, n_batch, seq, ctx_len)
    y = _matmul(att, l0_na_w_o, jnp.ones((1, d), F32), F32)
    xa, h = _residual(xa, y, l0_norm_post_mix, mod0, G_M, n_lat, seq,
                      nxt=(l0_norm_pre_ffn, mod0, SH_F, SC_F, F32))
    ys, pos, wts = _moe(h, router_w, router_b, l0_moe_w_gate, l0_moe_w_up, l0_moe_w_down)
    xa, h = _combine(ys, pos, wts, xa, l0_norm_post_ffn, mod0, G_F, n_lat, seq,
                     nxt=(l1_norm_pre_mix, mod1, SH_M, SC_M, BF16))

    q_cols = d
    kv_cols = d // GQA_REP
    qkv = _matmul(h, l1_gqa_w_qkv, jnp.ones((1, q_cols + 2 * kv_cols), F32), BF16)
    cos, sin = _rope_tables(seq, n_lat)
    ident = (jnp.ones((n_all - n_lat, HEAD_DIM), F32), jnp.zeros((n_all - n_lat, HEAD_DIM), F32))
    cos = jnp.concatenate([cos, ident[0]], axis=0)
    sin = jnp.concatenate([sin, ident[1]], axis=0)
    gains = jnp.stack([l1_gqa_q_gain, l1_gqa_k_gain]).reshape(2, 1, HEAD_DIM).astype(F32)
    posts = jnp.stack([jnp.full((HEAD_DIM,), scale, F32), jnp.ones((HEAD_DIM,), F32)]).reshape(2, 1, HEAD_DIM)
    qk = _qk_prep(qkv, cos, sin, gains, posts, q_cols + kv_cols, q_cols)
    att = _gqa_attention(qk, qkv, n_batch, seq, ctx_len, q_cols)
    y = _matmul(att, l1_gqa_w_o, jnp.ones((1, d), F32), F32)
    xl, h = _residual(xa, y, l1_norm_post_mix, mod1, G_M, n_lat, seq,
                      nxt=(l1_norm_pre_ffn, mod1, SH_F, SC_F, F32))
    ys, pos, wts = _moe(h, router_w, router_b, l1_moe_w_gate, l1_moe_w_up, l1_moe_w_down)
    xl = _combine(ys, pos, wts, xl, l1_norm_post_ffn, mod1, G_F, n_lat, seq)
    return xl.reshape(n_batch, seq, d)
```

```python
import functools

import jax
import jax.numpy as jnp
from jax import lax
from jax.experimental import pallas as pl
from jax.experimental.pallas import tpu as pltpu

LANES = 128
SUBLANES = 8
BF16_SUBLANES = 16
GRID_W = 64
HEAD_DIM = 128
NA_WIN_H = 8
NA_WIN_W = 16
GQA_REP = 4
ROPE_THETA = 10000.0
N_GROUPS = 4
TOP_K = 2
EPS = 1e-6
NEG_INF = -1e30

F32 = jnp.float32
BF16 = jnp.bfloat16

ROW_TILE = 256
MM_TM = 512
MM_TN = 512
MOE_TM = 256
MOE_FC = 512
MOE_NC = 2048
NA_RB = 4
NA_HG = 8
GQA_QB = 256
ADA_TN = 512
VMEM_LIMIT = 56 * 1024 * 1024


def _params(sem):
    return pltpu.CompilerParams(dimension_semantics=sem, vmem_limit_bytes=VMEM_LIMIT)


def _adaln_body(c_ref, w_ref, b_ref, o_ref):
    c = c_ref[...]
    s = (c * jax.nn.sigmoid(c)).astype(BF16)
    o_ref[...] = jnp.dot(s, w_ref[...].astype(BF16), preferred_element_type=F32) + b_ref[...]


def _adaln(cvec, ada_w, ada_b):
    m, d = cvec.shape
    n = ada_w.shape[1]
    return pl.pallas_call(
        _adaln_body,
        grid=(n // ADA_TN,),
        in_specs=[pl.BlockSpec((m, d), lambda j: (0, 0)),
                  pl.BlockSpec((d, ADA_TN), lambda j: (0, j)),
                  pl.BlockSpec((1, ADA_TN), lambda j: (0, j))],
        out_specs=pl.BlockSpec((m, ADA_TN), lambda j: (0, j)),
        out_shape=jax.ShapeDtypeStruct((m, n), F32),
        compiler_params=_params(("arbitrary",)),
        name="adaln",
    )(cvec, ada_w, ada_b.reshape(1, n))


def _rms(x, gain):
    return x * lax.rsqrt(jnp.mean(x * x, axis=-1, keepdims=True) + EPS) * gain


def _norm_mod_body(x_ref, g_ref, sh_ref, sc_ref, o_ref):
    y = _rms(x_ref[...], g_ref[...])
    o_ref[...] = (y * (1.0 + sc_ref[0]) + sh_ref[0]).astype(o_ref.dtype)


def _mod_row_map(n_lat_tiles, tiles_per_batch, n_batch, chunk):
    def index_map(i):
        return (jnp.where(i < n_lat_tiles, i // tiles_per_batch, n_batch), 0, chunk)
    return index_map


def _norm_mod(x, gain, mod, shift_chunk, scale_chunk, n_lat, seq, out_dtype):
    n, d = x.shape
    n_batch = mod.shape[0] - 1
    mk = functools.partial(_mod_row_map, n_lat // ROW_TILE, seq // ROW_TILE, n_batch)
    return pl.pallas_call(
        _norm_mod_body,
        grid=(n // ROW_TILE,),
        in_specs=[pl.BlockSpec((ROW_TILE, d), lambda i: (i, 0)),
                  pl.BlockSpec((1, d), lambda i: (0, 0)),
                  pl.BlockSpec((1, 1, d), mk(shift_chunk)),
                  pl.BlockSpec((1, 1, d), mk(scale_chunk))],
        out_specs=pl.BlockSpec((ROW_TILE, d), lambda i: (i, 0)),
        out_shape=jax.ShapeDtypeStruct((n, d), out_dtype),
        compiler_params=_params(("arbitrary",)),
        name="norm_mod",
    )(x, gain.reshape(1, d), mod, mod)


def _split_bf16(x):
    hi = x.astype(BF16)
    lo = (x - hi.astype(F32)).astype(BF16)
    return hi, lo


def _tile_to_vregs(tile_ref, value):
    groups, pieces = tile_ref.shape[0], tile_ref.shape[1]
    for s in range(pieces):
        tile_ref[:, s] = value[:, s * LANES:(s + 1) * LANES].reshape(groups, SUBLANES, LANES)


def _residual_body(x_ref, y_ref, g_ref, gate_ref, g2_ref, sh_ref, sc_ref, rw_ref, xo_ref, hr_hbm, lg_ref,
                   hbuf, sem):
    i = pl.program_id(0)
    n = pl.num_programs(0)
    groups = hbuf.shape[1]
    slot = i % 2

    def copies(step, slot):
        return [pltpu.make_async_copy(hbuf.at[slot, :, :, r, :], hr_hbm.at[pl.ds(step * groups, groups), r],
                                      sem.at[slot]) for r in range(SUBLANES)]

    @pl.when(i >= 2)
    def _():
        for c in copies(i - 2, slot):
            c.wait()

    xn = x_ref[...] + gate_ref[0] * _rms(y_ref[...], g_ref[...])
    xo_ref[...] = xn
    h = _rms(xn, g2_ref[...]) * (1.0 + sc_ref[0]) + sh_ref[0]
    _tile_to_vregs(hbuf.at[slot], h)
    for c in copies(i, slot):
        c.start()
    h_hi, h_lo = _split_bf16(h)
    w_hi, w_lo = _split_bf16(rw_ref[...])
    lg_ref[...] = (jnp.dot(h_hi, w_hi, preferred_element_type=F32)
                   + jnp.dot(h_lo, w_hi, preferred_element_type=F32)
                   + jnp.dot(h_hi, w_lo, preferred_element_type=F32))

    @pl.when((i == n - 1) & (i >= 1))
    def _():
        for c in copies(i - 1, 1 - slot):
            c.wait()

    @pl.when(i == n - 1)
    def _():
        for c in copies(i, slot):
            c.wait()


def _residual(x, y, gain, mod, gate_chunk, n_lat, seq, gain2, shift_chunk, scale_chunk, router_w):
    n, d = y.shape
    n_batch = mod.shape[0] - 1
    pieces = d // LANES
    groups = ROW_TILE // SUBLANES
    mk = functools.partial(_mod_row_map, n_lat // ROW_TILE, seq // ROW_TILE, n_batch)
    row = pl.BlockSpec((ROW_TILE, d), lambda i: (i, 0))
    vec = pl.BlockSpec((1, d), lambda i: (0, 0))
    e = router_w.shape[1]
    w_pad = jnp.pad(router_w, ((0, 0), (0, LANES - e)))
    xn, h_rows, logits = pl.pallas_call(
        _residual_body,
        grid=(n // ROW_TILE,),
        in_specs=[row, row, vec, pl.BlockSpec((1, 1, d), mk(gate_chunk)), vec,
                  pl.BlockSpec((1, 1, d), mk(shift_chunk)), pl.BlockSpec((1, 1, d), mk(scale_chunk)),
                  pl.BlockSpec((d, LANES), lambda i: (0, 0))],
        out_specs=[row, pl.BlockSpec(memory_space=pl.ANY), pl.BlockSpec((ROW_TILE, LANES), lambda i: (i, 0))],
        out_shape=[jax.ShapeDtypeStruct((n, d), F32),
                   jax.ShapeDtypeStruct((n // SUBLANES, SUBLANES, pieces, LANES), F32),
                   jax.ShapeDtypeStruct((n, LANES), F32)],
        scratch_shapes=[pltpu.VMEM((2, groups, pieces, SUBLANES, LANES), F32), pltpu.SemaphoreType.DMA((2,))],
        compiler_params=_params(("arbitrary",)),
        name="residual",
    )(x, y, gain.reshape(1, d), mod, gain2.reshape(1, d), mod, mod, w_pad)
    return xn, h_rows.reshape(n, pieces, LANES), logits


def _matmul_body(a_ref, w_ref, s_ref, o_ref, wb_ref):
    @pl.when(pl.program_id(1) == 0)
    def _():
        wb_ref[...] = w_ref[...].astype(BF16)

    acc = jnp.dot(a_ref[...], wb_ref[...], preferred_element_type=F32)
    o_ref[...] = (acc * s_ref[...]).astype(o_ref.dtype)


def _matmul(a, w, col_scale, out_dtype):
    m, k = a.shape
    n = w.shape[1]
    return pl.pallas_call(
        _matmul_body,
        grid=(n // MM_TN, m // MM_TM),
        in_specs=[pl.BlockSpec((MM_TM, k), lambda j, i: (i, 0)),
                  pl.BlockSpec((k, MM_TN), lambda j, i: (0, j)),
                  pl.BlockSpec((1, MM_TN), lambda j, i: (0, j))],
        out_specs=pl.BlockSpec((MM_TM, MM_TN), lambda j, i: (i, j)),
        out_shape=jax.ShapeDtypeStruct((m, n), out_dtype),
        scratch_shapes=[pltpu.VMEM((k, MM_TN), BF16)],
        compiler_params=_params(("arbitrary", "arbitrary")),
        name="matmul",
    )(a, w, col_scale)


def _softmax_pv(s_parts, v_parts):
    m = s_parts[0].max(axis=-1, keepdims=True)
    for s in s_parts[1:]:
        m = jnp.maximum(m, s.max(axis=-1, keepdims=True))
    den = None
    acc = None
    for s, v in zip(s_parts, v_parts):
        p = jnp.exp(s - m)
        l = p.sum(axis=-1, keepdims=True)
        o = jnp.dot(p.astype(BF16), v, preferred_element_type=F32)
        den = l if den is None else den + l
        acc = o if acc is None else acc + o
    return acc / den


def _qkt(q, k):
    return lax.dot_general(q, k, (((1,), (1,)), ((), ())), preferred_element_type=F32)


def _na_plan(rows):
    kh = min(NA_WIN_H, rows)
    uw = NA_RB + kh - 1
    uw += uw % 2
    assert rows >= uw and rows % NA_RB == 0
    sigs, cls_of_rb = [], []
    for rb in range(rows // NA_RB):
        rs_blk = min(max(rb * NA_RB - kh // 2, 0), rows - uw)
        offs = []
        for r in range(rb * NA_RB, (rb + 1) * NA_RB):
            rs = min(max(r - kh // 2, 0), rows - kh)
            assert rs_blk <= rs and rs + kh <= rs_blk + uw
            offs.append(rs - r)
        sig = (rs_blk - rb * NA_RB, tuple(offs))
        if sig not in sigs:
            sigs.append(sig)
        cls_of_rb.append(sigs.index(sig))
    return kh, uw, sigs, cls_of_rb


def _na_body(n_row_blocks, rows, q_ref, k_ref, v_ref, kc_ref, vc_ref, b_ref, o_ref):
    rb = pl.program_id(2)
    kh, uw, _, cls_of_rb = _na_plan(rows)
    span = uw * GRID_W

    @pl.when(rb < n_row_blocks)
    def _():
        rs = jnp.clip(rb * NA_RB - kh // 2, 0, rows - uw)
        k0 = pl.multiple_of(rs * GRID_W, GRID_W)
        cls = jnp.int32(0)
        for i, c in enumerate(cls_of_rb):
            cls = jnp.where(rb == i, c, cls)
        for g in range(NA_HG):
            cols = slice(g * HEAD_DIM, (g + 1) * HEAD_DIM)
            q = q_ref[:, cols]
            s_loc = _qkt(q, k_ref[pl.ds(k0, span), cols]) + b_ref[cls, g]
            s_ctx = _qkt(q, kc_ref[:, cols])
            o = _softmax_pv([s_loc, s_ctx], [v_ref[pl.ds(k0, span), cols], vc_ref[:, cols]])
            o_ref[:, cols] = o.astype(o_ref.dtype)

    @pl.when(rb >= n_row_blocks)
    def _():
        for g in range(NA_HG):
            cols = slice(g * HEAD_DIM, (g + 1) * HEAD_DIM)
            s = _qkt(q_ref[:, cols], kc_ref[:, cols])
            o_ref[:, cols] = _softmax_pv([s], [vc_ref[:, cols]]).astype(o_ref.dtype)


def _na_bias_table(rpb, rows):
    kh, uw, sigs, _ = _na_plan(rows)
    heads = rpb.shape[0]
    qcol = jnp.arange(GRID_W)
    col_start = jnp.clip(qcol - NA_WIN_W // 2, 0, GRID_W - NA_WIN_W)
    col_valid = (qcol[None, :] >= col_start[:, None]) & (qcol[None, :] < col_start[:, None] + NA_WIN_W)
    dc_idx = jnp.clip(qcol[None, :] - qcol[:, None], -(NA_WIN_W - 1), NA_WIN_W - 1) + NA_WIN_W - 1
    onehot = (dc_idx[None] == jnp.arange(2 * NA_WIN_W - 1)[:, None, None]).astype(F32)
    band = jnp.einsum('hrc,cqk->hrqk', rpb.astype(F32), onehot, precision=lax.Precision.HIGHEST)
    band = jnp.where(col_valid[None, None], band, NEG_INF)
    neg = jnp.full((heads, GRID_W, GRID_W), NEG_INF, F32)
    tables = []
    for off, band_offs in sigs:
        per_row = []
        for i in range(NA_RB):
            tiles = []
            for j in range(uw):
                inside = 0 <= off + j - i - band_offs[i] < kh
                tiles.append(band[:, off + j - i + NA_WIN_H - 1] if inside else neg)
            per_row.append(jnp.concatenate(tiles, axis=-1))
        tables.append(jnp.concatenate(per_row, axis=1))
    return jnp.stack(tables)


def _na_attention(qkv, bias, n_batch, seq, ctx_len):
    n, d3 = qkv.shape
    d = d3 // 3
    heads = d // HEAD_DIM
    rows = seq // GRID_W
    qb = NA_RB * GRID_W
    assert seq % qb == 0 and ctx_len % qb == 0 and heads % NA_HG == 0
    n_row_blocks = seq // qb
    n_ctx_blocks = ctx_len // qb
    hgs = heads // NA_HG
    w = NA_HG * HEAD_DIM
    lat_blocks = n_batch * n_row_blocks

    def q_map(hg, b, rb):
        blk = jnp.where(rb < n_row_blocks, b * n_row_blocks + rb, lat_blocks + b * n_ctx_blocks + rb - n_row_blocks)
        return (blk, hg)

    ctx_blk0 = (n_batch * seq) // ctx_len
    n_cls, _, bq, bk = bias.shape
    return pl.pallas_call(
        functools.partial(_na_body, n_row_blocks, rows),
        grid=(hgs, n_batch, n_row_blocks + n_ctx_blocks),
        in_specs=[pl.BlockSpec((qb, w), q_map),
                  pl.BlockSpec((seq, w), lambda hg, b, rb: (b, hgs + hg)),
                  pl.BlockSpec((seq, w), lambda hg, b, rb: (b, 2 * hgs + hg)),
                  pl.BlockSpec((ctx_len, w), lambda hg, b, rb: (ctx_blk0 + b, hgs + hg)),
                  pl.BlockSpec((ctx_len, w), lambda hg, b, rb: (ctx_blk0 + b, 2 * hgs + hg)),
                  pl.BlockSpec((n_cls, NA_HG, bq, bk), lambda hg, b, rb: (0, hg, 0, 0),
                               pipeline_mode=pl.Buffered(1))],
        out_specs=pl.BlockSpec((qb, w), q_map),
        out_shape=jax.ShapeDtypeStruct((n, d), BF16),
        compiler_params=_params(("arbitrary", "arbitrary", "arbitrary")),
        name="na_attention",
    )(qkv, qkv, qkv, qkv, qkv, bias)


def _rope_tables(seq, n_rows):
    t = jnp.arange(seq, dtype=jnp.int32)
    row = (t // GRID_W).astype(F32)
    col = (t % GRID_W).astype(F32)
    axis_dim = HEAD_DIM // 2
    inv_freq = 1.0 / (ROPE_THETA ** (jnp.arange(0, axis_dim, 2, dtype=F32) / axis_dim))
    ang = jnp.concatenate([row[:, None] * inv_freq, col[:, None] * inv_freq], axis=-1)
    cos = jnp.repeat(jnp.cos(ang), 2, axis=-1)
    sin = jnp.repeat(jnp.sin(ang), 2, axis=-1) * jnp.tile(jnp.array([-1.0, 1.0], F32), HEAD_DIM // 2)
    reps = n_rows // seq
    return jnp.tile(cos, (reps, 1)), jnp.tile(sin, (reps, 1))


def _qk_prep_body(n_heads_blk, x_ref, cos_ref, sin_ref, g_ref, s_ref, o_ref):
    cos = cos_ref[...]
    sin = sin_ref[...]
    gain = g_ref[0]
    post = s_ref[0]
    even = (lax.broadcasted_iota(jnp.int32, cos.shape, 1) % 2) == 0
    for h in range(n_heads_blk):
        cols = slice(h * HEAD_DIM, (h + 1) * HEAD_DIM)
        y = _rms(x_ref[:, cols].astype(F32), gain)
        partner = jnp.where(even, pltpu.roll(y, HEAD_DIM - 1, 1), pltpu.roll(y, 1, 1))
        o_ref[:, cols] = ((y * cos + partner * sin) * post).astype(o_ref.dtype)


def _qk_prep(qkv, cos, sin, gains, posts, n_qk_cols, q_cols):
    n = qkv.shape[0]
    blk_heads = min(8, (n_qk_cols - q_cols) // HEAD_DIM)
    w = blk_heads * HEAD_DIM
    assert q_cols % w == 0 and n_qk_cols % w == 0
    q_blocks = q_cols // w

    def sel(i, j):
        return (jnp.where(j < q_blocks, 0, 1), 0, 0)

    return pl.pallas_call(
        functools.partial(_qk_prep_body, blk_heads),
        grid=(n // ROW_TILE, n_qk_cols // w),
        in_specs=[pl.BlockSpec((ROW_TILE, w), lambda i, j: (i, j)),
                  pl.BlockSpec((ROW_TILE, HEAD_DIM), lambda i, j: (i, 0)),
                  pl.BlockSpec((ROW_TILE, HEAD_DIM), lambda i, j: (i, 0)),
                  pl.BlockSpec((1, 1, HEAD_DIM), sel),
                  pl.BlockSpec((1, 1, HEAD_DIM), sel)],
        out_specs=pl.BlockSpec((ROW_TILE, w), lambda i, j: (i, j)),
        out_shape=jax.ShapeDtypeStruct((n, n_qk_cols), BF16),
        compiler_params=_params(("arbitrary", "arbitrary")),
        name="qk_prep",
    )(qkv, cos, sin, gains, posts)


def _gqa_body(q_ref, k_ref, kc_ref, v_ref, vc_ref, o_ref, ka_ref, va_ref):
    seq = k_ref.shape[0]

    @pl.when(pl.program_id(2) == 0)
    def _():
        ka_ref[0:seq, :] = k_ref[...]
        ka_ref[seq:, :] = kc_ref[...]
        va_ref[0:seq, 0:HEAD_DIM] = v_ref[...]
        va_ref[seq:, 0:HEAD_DIM] = vc_ref[...]
        va_ref[:, HEAD_DIM:] = jnp.ones((va_ref.shape[0], va_ref.shape[1] - HEAD_DIM), BF16)

    for r in range(GQA_REP):
        cols = slice(r * HEAD_DIM, (r + 1) * HEAD_DIM)
        s = _qkt(q_ref[:, cols], ka_ref[...])
        p = jnp.exp(s - s.max(axis=-1, keepdims=True)).astype(BF16)
        ox = jnp.dot(p, va_ref[...], preferred_element_type=F32)
        o_ref[:, cols] = (ox[:, 0:HEAD_DIM] / ox[:, HEAD_DIM:HEAD_DIM + 1]).astype(o_ref.dtype)


def _gqa_attention(qk, qkv, n_batch, seq, ctx_len, q_cols):
    kv_heads = q_cols // HEAD_DIM // GQA_REP
    assert seq % GQA_QB == 0 and (n_batch * seq) % ctx_len == 0
    w = GQA_REP * HEAD_DIM
    qblocks = seq // GQA_QB
    k_blk0 = q_cols // HEAD_DIM
    v_blk0 = k_blk0 + kv_heads
    ctx_blk0 = (n_batch * seq) // ctx_len
    return pl.pallas_call(
        _gqa_body,
        grid=(n_batch, kv_heads, qblocks),
        in_specs=[pl.BlockSpec((GQA_QB, w), lambda b, h, i: (b * qblocks + i, h)),
                  pl.BlockSpec((seq, HEAD_DIM), lambda b, h, i: (b, k_blk0 + h)),
                  pl.BlockSpec((ctx_len, HEAD_DIM), lambda b, h, i: (ctx_blk0 + b, k_blk0 + h)),
                  pl.BlockSpec((seq, HEAD_DIM), lambda b, h, i: (b, v_blk0 + h)),
                  pl.BlockSpec((ctx_len, HEAD_DIM), lambda b, h, i: (ctx_blk0 + b, v_blk0 + h))],
        out_specs=pl.BlockSpec((GQA_QB, w), lambda b, h, i: (b * qblocks + i, h)),
        out_shape=jax.ShapeDtypeStruct((n_batch * seq, q_cols), BF16),
        scratch_shapes=[pltpu.VMEM((seq + ctx_len, HEAD_DIM), BF16),
                        pltpu.VMEM((seq + ctx_len, 2 * HEAD_DIM), BF16)],
        compiler_params=_params(("arbitrary", "arbitrary", "arbitrary")),
        name="gqa_attention",
    )(qk, qk, qk, qkv, qkv)


def _router_body(n_experts, lg_ref, b_ref, id_ref, wt_ref):
    lt = lg_ref[...].T[:n_experts] + b_ref[...]
    rows = [lt[e:e + 1] for e in range(n_experts)]
    mx = functools.reduce(jnp.maximum, rows)
    ex = [jnp.exp(r - mx) for r in rows]
    den = functools.reduce(jnp.add, ex)
    probs = [e / den for e in ex]

    epg = n_experts // N_GROUPS
    best = None
    for g in range(N_GROUPS):
        p = probs[g * epg:(g + 1) * epg]
        top1 = functools.reduce(jnp.maximum, p)
        i1 = jnp.full(top1.shape, epg, jnp.int32)
        for j in reversed(range(epg)):
            i1 = jnp.where(p[j] == top1, j, i1)
        rest = [jnp.where(i1 == j, -1.0, p[j]) for j in range(epg)]
        top2 = functools.reduce(jnp.maximum, rest)
        i2 = jnp.full(top1.shape, epg, jnp.int32)
        for j in reversed(range(epg)):
            i2 = jnp.where((rest[j] == top2) & (i1 != j), j, i2)
        cand = (top1 + top2, top1, top2, i1 + g * epg, i2 + g * epg)
        if best is None:
            best = cand
        else:
            take = cand[0] > best[0]
            best = tuple(jnp.where(take, c, b) for c, b in zip(cand, best))
    score, top1, top2, e1, e2 = best
    id_ref[...] = jnp.zeros(id_ref.shape, jnp.int32)
    wt_ref[...] = jnp.zeros(wt_ref.shape, F32)
    id_ref[0:1, :] = e1
    id_ref[1:2, :] = e2
    wt_ref[0:1, :] = top1 / score
    wt_ref[1:2, :] = top2 / score


def _router(logits, router_b):
    n = logits.shape[0]
    e = router_b.shape[0]
    tm = 512
    return pl.pallas_call(
        functools.partial(_router_body, e),
        grid=(n // tm,),
        in_specs=[pl.BlockSpec((tm, LANES), lambda i: (i, 0)),
                  pl.BlockSpec((e, 1), lambda i: (0, 0))],
        out_specs=[pl.BlockSpec((8, tm), lambda i: (0, i)), pl.BlockSpec((8, tm), lambda i: (0, i))],
        out_shape=[jax.ShapeDtypeStruct((8, n), jnp.int32), jax.ShapeDtypeStruct((8, n), F32)],
        compiler_params=_params(("arbitrary",)),
        name="router",
    )(logits, router_b.reshape(e, 1).astype(F32))


def _dispatch_plan(ids, n_experts, n_tiles):
    e0, e1 = ids[0], ids[1]
    n = e0.shape[0]
    ar = jnp.arange(n_experts, dtype=jnp.int32)[:, None]
    oh0 = (e0[None, :] == ar).astype(jnp.int32)
    oh1 = (e1[None, :] == ar).astype(jnp.int32)
    sel = oh0 + oh1
    csum = jnp.cumsum(sel, axis=1)
    counts = csum[:, -1]
    padded = ((counts + MOE_TM - 1) // MOE_TM) * MOE_TM
    ends = jnp.cumsum(padded)
    offs = ends - padded
    slot = offs[:, None] + csum - 1
    pos0 = jnp.sum(oh0 * slot, axis=0)
    pos1 = jnp.sum(oh1 * slot, axis=0)
    tok = jnp.arange(n, dtype=jnp.int32)
    src = jnp.zeros((n_tiles * MOE_TM,), jnp.int32)
    src = src.at[jnp.concatenate([pos0, pos1])].set(jnp.concatenate([tok, tok]), unique_indices=True)
    tile_start = jnp.arange(n_tiles, dtype=jnp.int32) * MOE_TM
    tile_expert = jnp.sum((ends[None, :] <= tile_start[:, None]).astype(jnp.int32), axis=1)
    tile_expert = jnp.minimum(tile_expert, n_experts - 1)
    n_used = (ends[-1] // MOE_TM).astype(jnp.int32).reshape(1)
    return src, jnp.stack([pos0, pos1]), tile_expert, n_used


def _row_copy(src_hbm, row, dst, k, sem):
    return pltpu.make_async_copy(src_hbm.at[pl.ds(row, 1), :], dst.at[pl.ds(k, 1), :], sem)


def _token_copy(src_hbm, tok, tile, g, r, sem):
    return pltpu.make_async_copy(src_hbm.at[tok], tile.at[g, :, r, :], sem)


def _gather_body(idx_ref, h_hbm, o_ref, buf, sem):
    i = pl.program_id(0)
    n = pl.num_programs(0)
    groups, pieces = buf.shape[1], buf.shape[2]

    def issue(step, slot):
        def one(g, carry):
            for r in range(SUBLANES):
                tok = idx_ref[step * ROW_TILE + g * SUBLANES + r]
                _token_copy(h_hbm, tok, buf.at[slot], g, r, sem.at[slot]).start()
            return carry
        lax.fori_loop(0, groups, one, 0)

    @pl.when(i == 0)
    def _():
        issue(0, 0)

    @pl.when(i + 1 < n)
    def _():
        issue(i + 1, (i + 1) % 2)

    slot = i % 2

    def wait_one(g, carry):
        for r in range(SUBLANES):
            _token_copy(h_hbm, 0, buf.at[slot], g, r, sem.at[slot]).wait()
        return carry
    lax.fori_loop(0, groups, wait_one, 0)
    for s in range(pieces):
        o_ref[:, s * LANES:(s + 1) * LANES] = buf[slot, :, s].reshape(ROW_TILE, LANES).astype(o_ref.dtype)


def _gather_rows(h_rows, src):
    _, pieces, _ = h_rows.shape
    d = pieces * LANES
    r = src.shape[0]
    return pl.pallas_call(
        _gather_body,
        grid_spec=pltpu.PrefetchScalarGridSpec(
            num_scalar_prefetch=1,
            grid=(r // ROW_TILE,),
            in_specs=[pl.BlockSpec(memory_space=pl.ANY)],
            out_specs=pl.BlockSpec((ROW_TILE, d), lambda i, idx: (i, 0)),
            scratch_shapes=[pltpu.VMEM((2, ROW_TILE // SUBLANES, pieces, SUBLANES, LANES), F32),
                            pltpu.SemaphoreType.DMA((2,))]),
        out_shape=jax.ShapeDtypeStruct((r, d), BF16),
        compiler_params=_params(("arbitrary",)),
        name="moe_gather",
    )(src, h_rows)


def _expert_changed(te_ref, t):
    return (t == 0) | (te_ref[t] != te_ref[jnp.maximum(t - 1, 0)])


def _moe_up_body(te_ref, nu_ref, x_ref, wg_ref, wu_ref, o_ref, wgb, wub):
    t = pl.program_id(1)

    @pl.when(_expert_changed(te_ref, t))
    def _():
        wgb[...] = wg_ref[0].astype(BF16)
        wub[...] = wu_ref[0].astype(BF16)

    @pl.when(t < nu_ref[0])
    def _():
        x = x_ref[...]
        g = jnp.dot(x, wgb[...], preferred_element_type=F32)
        u = jnp.dot(x, wub[...], preferred_element_type=F32)
        o_ref[...] = (g * jax.nn.sigmoid(g) * u).astype(o_ref.dtype)

    @pl.when(t >= nu_ref[0])
    def _():
        o_ref[...] = jnp.zeros(o_ref.shape, o_ref.dtype)


def _moe_up(xs, w_gate, w_up, tile_expert, n_used):
    r, d = xs.shape
    f = w_gate.shape[2]
    n_tiles = r // MOE_TM

    def x_map(j, t, te, nu):
        return (jnp.minimum(t, nu[0] - 1), 0)

    def w_map(j, t, te, nu):
        return (te[t], 0, j)

    return pl.pallas_call(
        _moe_up_body,
        grid_spec=pltpu.PrefetchScalarGridSpec(
            num_scalar_prefetch=2,
            grid=(f // MOE_FC, n_tiles),
            in_specs=[pl.BlockSpec((MOE_TM, d), x_map),
                      pl.BlockSpec((1, d, MOE_FC), w_map),
                      pl.BlockSpec((1, d, MOE_FC), w_map)],
            out_specs=pl.BlockSpec((MOE_TM, MOE_FC), lambda j, t, te, nu: (t, j)),
            scratch_shapes=[pltpu.VMEM((d, MOE_FC), BF16), pltpu.VMEM((d, MOE_FC), BF16)]),
        out_shape=jax.ShapeDtypeStruct((r, f), BF16),
        compiler_params=_params(("arbitrary", "arbitrary")),
        name="moe_up",
    )(tile_expert, n_used, xs, w_gate, w_up)


def _moe_down_body(te_ref, nu_ref, h_ref, w_ref, o_ref, wb):
    t = pl.program_id(1)

    @pl.when(_expert_changed(te_ref, t))
    def _():
        wb[...] = w_ref[0].astype(BF16)

    @pl.when(t < nu_ref[0])
    def _():
        o_ref[...] = jnp.dot(h_ref[...], wb[...], preferred_element_type=F32)

    @pl.when(t >= nu_ref[0])
    def _():
        o_ref[...] = jnp.zeros(o_ref.shape, o_ref.dtype)


def _moe_down(hs, w_down, tile_expert, n_used):
    r, f = hs.shape
    d = w_down.shape[2]
    nc = min(MOE_NC, d)
    n_tiles = r // MOE_TM
    return pl.pallas_call(
        _moe_down_body,
        grid_spec=pltpu.PrefetchScalarGridSpec(
            num_scalar_prefetch=2,
            grid=(d // nc, n_tiles),
            in_specs=[pl.BlockSpec((MOE_TM, f), lambda j, t, te, nu: (jnp.minimum(t, nu[0] - 1), 0)),
                      pl.BlockSpec((1, f, nc), lambda j, t, te, nu: (te[t], 0, j))],
            out_specs=pl.BlockSpec((MOE_TM, nc), lambda j, t, te, nu: (t, j)),
            scratch_shapes=[pltpu.VMEM((f, nc), BF16)]),
        out_shape=jax.ShapeDtypeStruct((r, d), F32),
        compiler_params=_params(("arbitrary", "arbitrary")),
        name="moe_down",
    )(tile_expert, n_used, hs, w_down)


def _combine_body(has_next, pos_ref, y_hbm, wt_ref, x_ref, g_ref, gate_ref, *rest):
    if has_next:
        g2_ref, sh_ref, sc_ref, xo_ref, ho_ref, buf, sem = rest
    else:
        xo_ref, buf, sem = rest
    i = pl.program_id(0)
    n = pl.num_programs(0)
    n_tok = n * ROW_TILE

    def issue(step, slot):
        def one(k, carry):
            tok = step * ROW_TILE + k
            _row_copy(y_hbm, pos_ref[tok], buf.at[slot, 0], k, sem.at[slot]).start()
            _row_copy(y_hbm, pos_ref[n_tok + tok], buf.at[slot, 1], k, sem.at[slot]).start()
            return carry
        lax.fori_loop(0, ROW_TILE, one, 0, unroll=8)

    @pl.when(i == 0)
    def _():
        issue(0, 0)

    @pl.when(i + 1 < n)
    def _():
        issue(i + 1, (i + 1) % 2)

    slot = i % 2

    def wait_one(k, carry):
        _row_copy(y_hbm, 0, buf.at[slot, 0], k, sem.at[slot]).wait()
        _row_copy(y_hbm, 0, buf.at[slot, 1], k, sem.at[slot]).wait()
        return carry
    lax.fori_loop(0, ROW_TILE, wait_one, 0, unroll=8)

    wt = wt_ref[...]
    y = wt[:, 0:1] * buf[slot, 0] + wt[:, 1:2] * buf[slot, 1]
    xn = x_ref[...] + gate_ref[0] * _rms(y, g_ref[...])
    xo_ref[...] = xn
    if has_next:
        h = _rms(xn, g2_ref[...])
        ho_ref[...] = (h * (1.0 + sc_ref[0]) + sh_ref[0]).astype(ho_ref.dtype)


def _combine(ys, pos, wts, x, gain, mod, gate_chunk, n_lat, seq, nxt=None):
    n = wts.shape[0]
    d = ys.shape[1]
    n_batch = mod.shape[0] - 1
    n_lat_tiles = min(n_lat, n) // ROW_TILE

    def mk(chunk):
        def index_map(i, pos_ref):
            return (jnp.where(i < n_lat_tiles, i // (seq // ROW_TILE), n_batch), 0, chunk)
        return index_map

    row = pl.BlockSpec((ROW_TILE, d), lambda i, p: (i, 0))
    vec = pl.BlockSpec((1, d), lambda i, p: (0, 0))
    in_specs = [pl.BlockSpec(memory_space=pl.ANY),
                pl.BlockSpec((ROW_TILE, 2), lambda i, p: (i, 0)),
                row, vec, pl.BlockSpec((1, 1, d), mk(gate_chunk))]
    args = [ys, wts, x, gain.reshape(1, d), mod]
    out_specs = [row]
    out_shape = [jax.ShapeDtypeStruct((n, d), F32)]
    if nxt is not None:
        gain2, mod2, shift_chunk, scale_chunk, h_dtype = nxt
        in_specs += [vec, pl.BlockSpec((1, 1, d), mk(shift_chunk)), pl.BlockSpec((1, 1, d), mk(scale_chunk))]
        args += [gain2.reshape(1, d), mod2, mod2]
        out_specs.append(row)
        out_shape.append(jax.ShapeDtypeStruct((n, d), h_dtype))
    out = pl.pallas_call(
        functools.partial(_combine_body, nxt is not None),
        grid_spec=pltpu.PrefetchScalarGridSpec(
            num_scalar_prefetch=1,
            grid=(n // ROW_TILE,),
            in_specs=in_specs,
            out_specs=out_specs,
            scratch_shapes=[pltpu.VMEM((2, 2, ROW_TILE, d), F32), pltpu.SemaphoreType.DMA((2,))]),
        out_shape=out_shape,
        compiler_params=_params(("arbitrary",)),
        name="moe_combine",
    )(pos.reshape(-1), *args)
    return out if nxt is not None else out[0]


def _moe(h_rows, logits, router_b, w_gate, w_up, w_down):
    n = logits.shape[0]
    n_experts = router_b.shape[0]
    n_tiles = (TOP_K * n) // MOE_TM + n_experts
    ids, wts = _router(logits, router_b)
    src, pos, tile_expert, n_used = _dispatch_plan(ids, n_experts, n_tiles)
    xs = _gather_rows(h_rows, src)
    hs = _moe_up(xs, w_gate, w_up, tile_expert, n_used)
    ys = _moe_down(hs, w_down, tile_expert, n_used)
    return ys, pos, wts[:2].T


def kernel(x, c, ctx, c_ctx, router_w, router_b, l0_ada_w, l0_ada_b, l0_norm_pre_mix, l0_norm_post_mix, l0_norm_pre_ffn, l0_norm_post_ffn, l0_na_w_qkv, l0_na_rpb, l0_na_w_o, l0_moe_w_gate, l0_moe_w_up, l0_moe_w_down, l1_ada_w, l1_ada_b, l1_norm_pre_mix, l1_norm_post_mix, l1_norm_pre_ffn, l1_norm_post_ffn, l1_gqa_w_qkv, l1_gqa_q_gain, l1_gqa_k_gain, l1_gqa_w_o, l1_moe_w_gate, l1_moe_w_up, l1_moe_w_down):
    n_batch, seq, d = x.shape
    ctx_len = ctx.shape[1]
    n_lat = n_batch * seq
    n_all = n_lat + n_batch * ctx_len
    scale = HEAD_DIM ** -0.5
    SH_M, SC_M, G_M, SH_F, SC_F, G_F = range(6)

    cvec = jnp.zeros((8, d), F32).at[:n_batch].set(c).at[n_batch].set(c_ctx)
    mod0 = _adaln(cvec, l0_ada_w, l0_ada_b)[:n_batch + 1].reshape(n_batch + 1, 1, 6 * d)
    mod1 = _adaln(cvec, l1_ada_w, l1_ada_b)[:n_batch + 1].reshape(n_batch + 1, 1, 6 * d)

    xa = jnp.concatenate([x.reshape(n_lat, d), ctx.reshape(n_batch * ctx_len, d)], axis=0)

    h = _norm_mod(xa, l0_norm_pre_mix, mod0, SH_M, SC_M, n_lat, seq, BF16)
    qscale = jnp.concatenate([jnp.full((1, d), scale, F32), jnp.ones((1, 2 * d), F32)], axis=1)
    qkv = _matmul(h, l0_na_w_qkv, qscale, BF16)
    bias = _na_bias_table(l0_na_rpb, seq // GRID_W)
    att = _na_attention(qkv, bias, n_batch, seq, ctx_len)
    y = _matmul(att, l0_na_w_o, jnp.ones((1, d), F32), F32)
    xa, h_rows, logits = _residual(xa, y, l0_norm_post_mix, mod0, G_M, n_lat, seq,
                                   l0_norm_pre_ffn, SH_F, SC_F, router_w)
    ys, pos, wts = _moe(h_rows, logits, router_b, l0_moe_w_gate, l0_moe_w_up, l0_moe_w_down)
    xa, h = _combine(ys, pos, wts, xa, l0_norm_post_ffn, mod0, G_F, n_lat, seq,
                     nxt=(l1_norm_pre_mix, mod1, SH_M, SC_M, BF16))

    q_cols = d
    kv_cols = d // GQA_REP
    qkv = _matmul(h, l1_gqa_w_qkv, jnp.ones((1, q_cols + 2 * kv_cols), F32), BF16)
    cos, sin = _rope_tables(seq, n_lat)
    ident = (jnp.ones((n_all - n_lat, HEAD_DIM), F32), jnp.zeros((n_all - n_lat, HEAD_DIM), F32))
    cos = jnp.concatenate([cos, ident[0]], axis=0)
    sin = jnp.concatenate([sin, ident[1]], axis=0)
    gains = jnp.stack([l1_gqa_q_gain, l1_gqa_k_gain]).reshape(2, 1, HEAD_DIM).astype(F32)
    posts = jnp.stack([jnp.full((HEAD_DIM,), scale, F32), jnp.ones((HEAD_DIM,), F32)]).reshape(2, 1, HEAD_DIM)
    qk = _qk_prep(qkv, cos, sin, gains, posts, q_cols + kv_cols, q_cols)
    att = _gqa_attention(qk, qkv, n_batch, seq, ctx_len, q_cols)
    y = _matmul(att, l1_gqa_w_o, jnp.ones((1, d), F32), F32)
    xl, h_rows, logits = _residual(xa, y, l1_norm_post_mix, mod1, G_M, n_lat, seq,
                                   l1_norm_pre_ffn, SH_F, SC_F, router_w)
    ys, pos, wts = _moe(h_rows, logits, router_b, l1_moe_w_gate, l1_moe_w_up, l1_moe_w_down)
    xl = _combine(ys, pos, wts, xl, l1_norm_post_ffn, mod1, G_F, n_lat, seq)
    return xl.reshape(n_batch, seq, d)
```

```python
import functools

import jax
import jax.numpy as jnp
from jax import lax
from jax.experimental import pallas as pl
from jax.experimental.pallas import tpu as pltpu

LANES = 128
SUBLANES = 8
BF16_SUBLANES = 16
GRID_W = 64
HEAD_DIM = 128
NA_WIN_H = 8
NA_WIN_W = 16
GQA_REP = 4
ROPE_THETA = 10000.0
N_GROUPS = 4
TOP_K = 2
EPS = 1e-6
NEG_INF = -1e30

F32 = jnp.float32
BF16 = jnp.bfloat16

ROW_TILE = 256
MM_TM = 1024
MM_TN = 512
MOE_TM = 256
MOE_FC = 768
MOE_NC = 4096
NA_RB = 4
NA_HG = 8
GQA_QB = 512
ADA_TN = 512
VMEM_LIMIT = 56 * 1024 * 1024


def _params(sem):
    return pltpu.CompilerParams(dimension_semantics=sem, vmem_limit_bytes=VMEM_LIMIT)


def _adaln_body(c_ref, w_ref, b_ref, o_ref):
    c = c_ref[...]
    s = (c * jax.nn.sigmoid(c)).astype(BF16)
    o_ref[...] = jnp.dot(s, w_ref[...].astype(BF16), preferred_element_type=F32) + b_ref[...]


def _adaln(cvec, ada_w, ada_b):
    m, d = cvec.shape
    n = ada_w.shape[1]
    return pl.pallas_call(
        _adaln_body,
        grid=(n // ADA_TN,),
        in_specs=[pl.BlockSpec((m, d), lambda j: (0, 0)),
                  pl.BlockSpec((d, ADA_TN), lambda j: (0, j)),
                  pl.BlockSpec((1, ADA_TN), lambda j: (0, j))],
        out_specs=pl.BlockSpec((m, ADA_TN), lambda j: (0, j)),
        out_shape=jax.ShapeDtypeStruct((m, n), F32),
        compiler_params=_params(("arbitrary",)),
        name="adaln",
    )(cvec, ada_w, ada_b.reshape(1, n))


def _convert_rows(dst_ref, src_ref):
    rows = src_ref.shape[0]
    band = next(b for b in (256, 128, 64, 32, 16) if rows % b == 0)

    def one(i, carry):
        r0 = pl.multiple_of(i * band, band)
        dst_ref[pl.ds(r0, band), :] = src_ref[pl.ds(r0, band), :].astype(dst_ref.dtype)
        return carry
    lax.fori_loop(0, rows // band, one, 0)


def _rms(x, gain):
    return x * lax.rsqrt(jnp.mean(x * x, axis=-1, keepdims=True) + EPS) * gain


def _stream_rows(n_lat_tiles, xl_ref, xc_ref):
    return jnp.where(pl.program_id(0) < n_lat_tiles, xl_ref[...], xc_ref[...])


def _stream_specs(n_lat_tiles, d):
    return [pl.BlockSpec((ROW_TILE, d), lambda i: (jnp.minimum(i, n_lat_tiles - 1), 0)),
            pl.BlockSpec((ROW_TILE, d), lambda i: (jnp.maximum(i - n_lat_tiles, 0), 0))]


def _norm_mod_body(n_lat_tiles, xl_ref, xc_ref, g_ref, sh_ref, sc_ref, o_ref):
    y = _rms(_stream_rows(n_lat_tiles, xl_ref, xc_ref), g_ref[...])
    o_ref[...] = (y * (1.0 + sc_ref[0]) + sh_ref[0]).astype(o_ref.dtype)


def _mod_row_map(n_lat_tiles, tiles_per_batch, n_batch, chunk):
    def index_map(i):
        return (jnp.where(i < n_lat_tiles, i // tiles_per_batch, n_batch), 0, chunk)
    return index_map


def _norm_mod(x_lat, x_ctx, gain, mod, shift_chunk, scale_chunk, seq, out_dtype):
    n_lat, d = x_lat.shape
    n = n_lat + x_ctx.shape[0]
    n_batch = mod.shape[0] - 1
    n_lat_tiles = n_lat // ROW_TILE
    mk = functools.partial(_mod_row_map, n_lat_tiles, seq // ROW_TILE, n_batch)
    return pl.pallas_call(
        functools.partial(_norm_mod_body, n_lat_tiles),
        grid=(n // ROW_TILE,),
        in_specs=_stream_specs(n_lat_tiles, d) + [
            pl.BlockSpec((1, d), lambda i: (0, 0)),
            pl.BlockSpec((1, 1, d), mk(shift_chunk)),
            pl.BlockSpec((1, 1, d), mk(scale_chunk))],
        out_specs=pl.BlockSpec((ROW_TILE, d), lambda i: (i, 0)),
        out_shape=jax.ShapeDtypeStruct((n, d), out_dtype),
        compiler_params=_params(("arbitrary",)),
        name="norm_mod",
    )(x_lat, x_ctx, gain.reshape(1, d), mod, mod)


def _split_bf16(x):
    hi = x.astype(BF16)
    lo = (x - hi.astype(F32)).astype(BF16)
    return hi, lo


def _tile_to_vregs(tile_ref, value):
    groups, pieces = tile_ref.shape[0], tile_ref.shape[1]
    for s in range(pieces):
        tile_ref[:, s] = value[:, s * LANES:(s + 1) * LANES].reshape(groups, SUBLANES, LANES)


def _residual_body(n_lat_tiles, xl_ref, xc_ref, y_ref, g_ref, gate_ref, g2_ref, sh_ref, sc_ref, rw_ref,
                   xo_ref, hr_hbm, lg_ref, hbuf, sem):
    i = pl.program_id(0)
    n = pl.num_programs(0)
    groups = hbuf.shape[1]
    slot = i % 2

    def copies(step, slot):
        return [pltpu.make_async_copy(hbuf.at[slot, :, :, r, :], hr_hbm.at[pl.ds(step * groups, groups), r],
                                      sem.at[slot]) for r in range(SUBLANES)]

    @pl.when(i >= 2)
    def _():
        for c in copies(i - 2, slot):
            c.wait()

    xn = _stream_rows(n_lat_tiles, xl_ref, xc_ref) + gate_ref[0] * _rms(y_ref[...], g_ref[...])
    xo_ref[...] = xn
    h = _rms(xn, g2_ref[...]) * (1.0 + sc_ref[0]) + sh_ref[0]
    _tile_to_vregs(hbuf.at[slot], h)
    for c in copies(i, slot):
        c.start()
    h_hi, h_lo = _split_bf16(h)
    w_hi, w_lo = _split_bf16(rw_ref[...])
    lg_ref[...] = (jnp.dot(h_hi, w_hi, preferred_element_type=F32)
                   + jnp.dot(h_lo, w_hi, preferred_element_type=F32)
                   + jnp.dot(h_hi, w_lo, preferred_element_type=F32))

    @pl.when((i == n - 1) & (i >= 1))
    def _():
        for c in copies(i - 1, 1 - slot):
            c.wait()

    @pl.when(i == n - 1)
    def _():
        for c in copies(i, slot):
            c.wait()


def _residual(x_lat, x_ctx, y, gain, mod, gate_chunk, seq, gain2, shift_chunk, scale_chunk, router_w):
    n, d = y.shape
    n_batch = mod.shape[0] - 1
    pieces = d // LANES
    groups = ROW_TILE // SUBLANES
    n_lat_tiles = min(x_lat.shape[0], n) // ROW_TILE
    mk = functools.partial(_mod_row_map, n_lat_tiles, seq // ROW_TILE, n_batch)
    row = pl.BlockSpec((ROW_TILE, d), lambda i: (i, 0))
    vec = pl.BlockSpec((1, d), lambda i: (0, 0))
    e = router_w.shape[1]
    w_pad = jnp.pad(router_w, ((0, 0), (0, LANES - e)))
    xn, h_rows, logits = pl.pallas_call(
        functools.partial(_residual_body, n_lat_tiles),
        grid=(n // ROW_TILE,),
        in_specs=_stream_specs(n_lat_tiles, d) + [
            row, vec, pl.BlockSpec((1, 1, d), mk(gate_chunk)), vec,
            pl.BlockSpec((1, 1, d), mk(shift_chunk)), pl.BlockSpec((1, 1, d), mk(scale_chunk)),
            pl.BlockSpec((d, LANES), lambda i: (0, 0))],
        out_specs=[row, pl.BlockSpec(memory_space=pl.ANY), pl.BlockSpec((ROW_TILE, LANES), lambda i: (i, 0))],
        out_shape=[jax.ShapeDtypeStruct((n, d), F32),
                   jax.ShapeDtypeStruct((n // SUBLANES, SUBLANES, pieces, LANES), F32),
                   jax.ShapeDtypeStruct((n, LANES), F32)],
        scratch_shapes=[pltpu.VMEM((2, groups, pieces, SUBLANES, LANES), F32), pltpu.SemaphoreType.DMA((2,))],
        compiler_params=_params(("arbitrary",)),
        name="residual",
    )(x_lat, x_ctx, y, gain.reshape(1, d), mod, gain2.reshape(1, d), mod, mod, w_pad)
    return xn, h_rows.reshape(n, pieces, LANES), logits


def _matmul_body(a_ref, w_ref, s_ref, o_ref, wb_ref):
    @pl.when(pl.program_id(1) == 0)
    def _():
        _convert_rows(wb_ref, w_ref)

    acc = jnp.dot(a_ref[...], wb_ref[...], preferred_element_type=F32)
    o_ref[...] = (acc * s_ref[...]).astype(o_ref.dtype)


def _row_tile(m):
    return next(t for t in (MM_TM, MM_TM // 2, MM_TM // 4) if m % t == 0)


def _matmul(a, w, col_scale, out_dtype):
    m, k = a.shape
    n = w.shape[1]
    tm = _row_tile(m)
    return pl.pallas_call(
        _matmul_body,
        grid=(n // MM_TN, m // tm),
        in_specs=[pl.BlockSpec((tm, k), lambda j, i: (i, 0)),
                  pl.BlockSpec((k, MM_TN), lambda j, i: (0, j)),
                  pl.BlockSpec((1, MM_TN), lambda j, i: (0, j))],
        out_specs=pl.BlockSpec((tm, MM_TN), lambda j, i: (i, j)),
        out_shape=jax.ShapeDtypeStruct((m, n), out_dtype),
        scratch_shapes=[pltpu.VMEM((k, MM_TN), BF16)],
        compiler_params=_params(("arbitrary", "arbitrary")),
        name="matmul",
    )(a, w, col_scale)


def _rope_head(y, cos, sin, even):
    partner = jnp.where(even, pltpu.roll(y, HEAD_DIM - 1, 1), pltpu.roll(y, 1, 1))
    return y * cos + partner * sin


def _qkv_gqa_body(n_qk_tiles, a_ref, w_ref, cos_ref, sin_ref, gp_ref, o_ref, wb_ref):
    j = pl.program_id(0)

    @pl.when(pl.program_id(1) == 0)
    def _():
        _convert_rows(wb_ref, w_ref)

    @pl.when(j < n_qk_tiles)
    def _():
        gain = gp_ref[0, 0:1, :]
        post = gp_ref[0, 1:2, :]
        band = min(ROW_TILE, a_ref.shape[0])
        even = (lax.broadcasted_iota(jnp.int32, (band, HEAD_DIM), 1) % 2) == 0
        for r0 in range(0, a_ref.shape[0], band):
            rows = slice(r0, r0 + band)
            acc = jnp.dot(a_ref[rows, :], wb_ref[...], preferred_element_type=F32)
            cos = cos_ref[rows, :]
            sin = sin_ref[rows, :]
            for h in range(acc.shape[1] // HEAD_DIM):
                cols = slice(h * HEAD_DIM, (h + 1) * HEAD_DIM)
                y = _rope_head(_rms(acc[:, cols], gain), cos, sin, even)
                o_ref[rows, cols] = (y * post).astype(o_ref.dtype)

    @pl.when(j >= n_qk_tiles)
    def _():
        o_ref[...] = jnp.dot(a_ref[...], wb_ref[...], preferred_element_type=F32).astype(o_ref.dtype)


def _qkv_gqa(a, w, cos, sin, gain_post, q_cols, kv_cols):
    m, k = a.shape
    n = w.shape[1]
    tm = _row_tile(m)
    tn = min(MM_TN, kv_cols)
    assert q_cols % tn == 0 and kv_cols % tn == 0
    n_q_tiles = q_cols // tn
    n_qk_tiles = (q_cols + kv_cols) // tn
    return pl.pallas_call(
        functools.partial(_qkv_gqa_body, n_qk_tiles),
        grid=(n // tn, m // tm),
        in_specs=[pl.BlockSpec((tm, k), lambda j, i: (i, 0)),
                  pl.BlockSpec((k, tn), lambda j, i: (0, j)),
                  pl.BlockSpec((tm, HEAD_DIM), lambda j, i: (i, 0)),
                  pl.BlockSpec((tm, HEAD_DIM), lambda j, i: (i, 0)),
                  pl.BlockSpec((1, 2, HEAD_DIM), lambda j, i: (jnp.where(j < n_q_tiles, 0, 1), 0, 0))],
        out_specs=pl.BlockSpec((tm, tn), lambda j, i: (i, j)),
        out_shape=jax.ShapeDtypeStruct((m, n), BF16),
        scratch_shapes=[pltpu.VMEM((k, tn), BF16)],
        compiler_params=_params(("arbitrary", "arbitrary")),
        name="qkv_gqa",
    )(a, w, cos, sin, gain_post)


def _softmax_pv(s_parts, v_parts):
    m = s_parts[0].max(axis=-1, keepdims=True)
    for s in s_parts[1:]:
        m = jnp.maximum(m, s.max(axis=-1, keepdims=True))
    den = None
    acc = None
    for s, v in zip(s_parts, v_parts):
        p = jnp.exp(s - m)
        l = p.sum(axis=-1, keepdims=True)
        o = jnp.dot(p.astype(BF16), v, preferred_element_type=F32)
        den = l if den is None else den + l
        acc = o if acc is None else acc + o
    return acc / den


def _qkt(q, k):
    return lax.dot_general(q, k, (((1,), (1,)), ((), ())), preferred_element_type=F32)


def _na_plan(rows):
    kh = min(NA_WIN_H, rows)
    uw = NA_RB + kh - 1
    uw += uw % 2
    assert rows >= uw and rows % NA_RB == 0
    sigs, cls_of_rb = [], []
    for rb in range(rows // NA_RB):
        rs_blk = min(max(rb * NA_RB - kh // 2, 0), rows - uw)
        offs = []
        for r in range(rb * NA_RB, (rb + 1) * NA_RB):
            rs = min(max(r - kh // 2, 0), rows - kh)
            assert rs_blk <= rs and rs + kh <= rs_blk + uw
            offs.append(rs - r)
        sig = (rs_blk - rb * NA_RB, tuple(offs))
        if sig not in sigs:
            sigs.append(sig)
        cls_of_rb.append(sigs.index(sig))
    return kh, uw, sigs, cls_of_rb


def _na_body(n_row_blocks, rows, q_ref, k_ref, v_ref, kc_ref, vc_ref, b_ref, o_ref):
    rb = pl.program_id(2)
    kh, uw, _, cls_of_rb = _na_plan(rows)
    span = uw * GRID_W

    @pl.when(rb < n_row_blocks)
    def _():
        rs = jnp.clip(rb * NA_RB - kh // 2, 0, rows - uw)
        k0 = pl.multiple_of(rs * GRID_W, GRID_W)
        cls = jnp.int32(0)
        for i, c in enumerate(cls_of_rb):
            cls = jnp.where(rb == i, c, cls)
        for g in range(NA_HG):
            cols = slice(g * HEAD_DIM, (g + 1) * HEAD_DIM)
            q = q_ref[:, cols]
            s_loc = _qkt(q, k_ref[pl.ds(k0, span), cols]) + b_ref[cls, g]
            s_ctx = _qkt(q, kc_ref[:, cols])
            o = _softmax_pv([s_loc, s_ctx], [v_ref[pl.ds(k0, span), cols], vc_ref[:, cols]])
            o_ref[:, cols] = o.astype(o_ref.dtype)

    @pl.when(rb >= n_row_blocks)
    def _():
        for g in range(NA_HG):
            cols = slice(g * HEAD_DIM, (g + 1) * HEAD_DIM)
            s = _qkt(q_ref[:, cols], kc_ref[:, cols])
            o_ref[:, cols] = _softmax_pv([s], [vc_ref[:, cols]]).astype(o_ref.dtype)


def _na_bias_table(rpb, rows):
    kh, uw, sigs, _ = _na_plan(rows)
    heads = rpb.shape[0]
    qcol = jnp.arange(GRID_W)
    col_start = jnp.clip(qcol - NA_WIN_W // 2, 0, GRID_W - NA_WIN_W)
    col_valid = (qcol[None, :] >= col_start[:, None]) & (qcol[None, :] < col_start[:, None] + NA_WIN_W)
    dc_idx = jnp.clip(qcol[None, :] - qcol[:, None], -(NA_WIN_W - 1), NA_WIN_W - 1) + NA_WIN_W - 1
    onehot = (dc_idx[None] == jnp.arange(2 * NA_WIN_W - 1)[:, None, None]).astype(F32)
    band = jnp.einsum('hrc,cqk->hrqk', rpb.astype(F32), onehot, precision=lax.Precision.HIGHEST)
    band = jnp.where(col_valid[None, None], band, NEG_INF)
    neg = jnp.full((heads, GRID_W, GRID_W), NEG_INF, F32)
    tables = []
    for off, band_offs in sigs:
        per_row = []
        for i in range(NA_RB):
            tiles = []
            for j in range(uw):
                inside = 0 <= off + j - i - band_offs[i] < kh
                tiles.append(band[:, off + j - i + NA_WIN_H - 1] if inside else neg)
            per_row.append(jnp.concatenate(tiles, axis=-1))
        tables.append(jnp.concatenate(per_row, axis=1))
    return jnp.stack(tables)


def _na_attention(qkv, bias, n_batch, seq, ctx_len):
    n, d3 = qkv.shape
    d = d3 // 3
    heads = d // HEAD_DIM
    rows = seq // GRID_W
    qb = NA_RB * GRID_W
    assert seq % qb == 0 and ctx_len % qb == 0 and heads % NA_HG == 0
    n_row_blocks = seq // qb
    n_ctx_blocks = ctx_len // qb
    hgs = heads // NA_HG
    w = NA_HG * HEAD_DIM
    lat_blocks = n_batch * n_row_blocks

    def q_map(hg, b, rb):
        blk = jnp.where(rb < n_row_blocks, b * n_row_blocks + rb, lat_blocks + b * n_ctx_blocks + rb - n_row_blocks)
        return (blk, hg)

    ctx_blk0 = (n_batch * seq) // ctx_len
    n_cls, _, bq, bk = bias.shape
    return pl.pallas_call(
        functools.partial(_na_body, n_row_blocks, rows),
        grid=(hgs, n_batch, n_row_blocks + n_ctx_blocks),
        in_specs=[pl.BlockSpec((qb, w), q_map),
                  pl.BlockSpec((seq, w), lambda hg, b, rb: (b, hgs + hg)),
                  pl.BlockSpec((seq, w), lambda hg, b, rb: (b, 2 * hgs + hg)),
                  pl.BlockSpec((ctx_len, w), lambda hg, b, rb: (ctx_blk0 + b, hgs + hg)),
                  pl.BlockSpec((ctx_len, w), lambda hg, b, rb: (ctx_blk0 + b, 2 * hgs + hg)),
                  pl.BlockSpec((n_cls, NA_HG, bq, bk), lambda hg, b, rb: (0, hg, 0, 0),
                               pipeline_mode=pl.Buffered(1))],
        out_specs=pl.BlockSpec((qb, w), q_map),
        out_shape=jax.ShapeDtypeStruct((n, d), BF16),
        compiler_params=_params(("arbitrary", "arbitrary", "arbitrary")),
        name="na_attention",
    )(qkv, qkv, qkv, qkv, qkv, bias)


def _rope_tables(seq, n_rows):
    t = jnp.arange(seq, dtype=jnp.int32)
    row = (t // GRID_W).astype(F32)
    col = (t % GRID_W).astype(F32)
    axis_dim = HEAD_DIM // 2
    inv_freq = 1.0 / (ROPE_THETA ** (jnp.arange(0, axis_dim, 2, dtype=F32) / axis_dim))
    ang = jnp.concatenate([row[:, None] * inv_freq, col[:, None] * inv_freq], axis=-1)
    cos = jnp.repeat(jnp.cos(ang), 2, axis=-1)
    sin = jnp.repeat(jnp.sin(ang), 2, axis=-1) * jnp.tile(jnp.array([-1.0, 1.0], F32), HEAD_DIM // 2)
    reps = n_rows // seq
    return jnp.tile(cos, (reps, 1)), jnp.tile(sin, (reps, 1))


def _gqa_body(q_ref, k_ref, kc_ref, v_ref, vc_ref, o_ref, ka_ref, va_ref):
    seq = k_ref.shape[0]

    @pl.when(pl.program_id(2) == 0)
    def _():
        ka_ref[0:seq, :] = k_ref[...]
        ka_ref[seq:, :] = kc_ref[...]
        va_ref[0:seq, 0:HEAD_DIM] = v_ref[...]
        va_ref[seq:, 0:HEAD_DIM] = vc_ref[...]
        va_ref[:, HEAD_DIM:] = jnp.ones((va_ref.shape[0], va_ref.shape[1] - HEAD_DIM), BF16)

    for r in range(GQA_REP):
        cols = slice(r * HEAD_DIM, (r + 1) * HEAD_DIM)
        s = _qkt(q_ref[:, cols], ka_ref[...])
        p = jnp.exp(s - s.max(axis=-1, keepdims=True)).astype(BF16)
        ox = jnp.dot(p, va_ref[...], preferred_element_type=F32)
        o_ref[:, cols] = (ox[:, 0:HEAD_DIM] / ox[:, HEAD_DIM:HEAD_DIM + 1]).astype(o_ref.dtype)


def _gqa_attention(qk, qkv, n_batch, seq, ctx_len, q_cols):
    kv_heads = q_cols // HEAD_DIM // GQA_REP
    assert seq % GQA_QB == 0 and (n_batch * seq) % ctx_len == 0
    w = GQA_REP * HEAD_DIM
    qblocks = seq // GQA_QB
    k_blk0 = q_cols // HEAD_DIM
    v_blk0 = k_blk0 + kv_heads
    ctx_blk0 = (n_batch * seq) // ctx_len
    return pl.pallas_call(
        _gqa_body,
        grid=(n_batch, kv_heads, qblocks),
        in_specs=[pl.BlockSpec((GQA_QB, w), lambda b, h, i: (b * qblocks + i, h)),
                  pl.BlockSpec((seq, HEAD_DIM), lambda b, h, i: (b, k_blk0 + h)),
                  pl.BlockSpec((ctx_len, HEAD_DIM), lambda b, h, i: (ctx_blk0 + b, k_blk0 + h)),
                  pl.BlockSpec((seq, HEAD_DIM), lambda b, h, i: (b, v_blk0 + h)),
                  pl.BlockSpec((ctx_len, HEAD_DIM), lambda b, h, i: (ctx_blk0 + b, v_blk0 + h))],
        out_specs=pl.BlockSpec((GQA_QB, w), lambda b, h, i: (b * qblocks + i, h)),
        out_shape=jax.ShapeDtypeStruct((n_batch * seq, q_cols), BF16),
        scratch_shapes=[pltpu.VMEM((seq + ctx_len, HEAD_DIM), BF16),
                        pltpu.VMEM((seq + ctx_len, 2 * HEAD_DIM), BF16)],
        compiler_params=_params(("arbitrary", "arbitrary", "arbitrary")),
        name="gqa_attention",
    )(qk, qk, qk, qkv, qkv)


def _router_body(n_experts, lg_ref, b_ref, id_ref, wt_ref):
    lt = lg_ref[...].T[:n_experts] + b_ref[...]
    rows = [lt[e:e + 1] for e in range(n_experts)]
    mx = functools.reduce(jnp.maximum, rows)
    ex = [jnp.exp(r - mx) for r in rows]
    den = functools.reduce(jnp.add, ex)
    probs = [e / den for e in ex]

    epg = n_experts // N_GROUPS
    best = None
    for g in range(N_GROUPS):
        p = probs[g * epg:(g + 1) * epg]
        top1 = functools.reduce(jnp.maximum, p)
        i1 = jnp.full(top1.shape, epg, jnp.int32)
        for j in reversed(range(epg)):
            i1 = jnp.where(p[j] == top1, j, i1)
        rest = [jnp.where(i1 == j, -1.0, p[j]) for j in range(epg)]
        top2 = functools.reduce(jnp.maximum, rest)
        i2 = jnp.full(top1.shape, epg, jnp.int32)
        for j in reversed(range(epg)):
            i2 = jnp.where((rest[j] == top2) & (i1 != j), j, i2)
        cand = (top1 + top2, top1, top2, i1 + g * epg, i2 + g * epg)
        if best is None:
            best = cand
        else:
            take = cand[0] > best[0]
            best = tuple(jnp.where(take, c, b) for c, b in zip(cand, best))
    score, top1, top2, e1, e2 = best
    id_ref[...] = jnp.zeros(id_ref.shape, jnp.int32)
    wt_ref[...] = jnp.zeros(wt_ref.shape, F32)
    id_ref[0:1, :] = e1
    id_ref[1:2, :] = e2
    wt_ref[0:1, :] = top1 / score
    wt_ref[1:2, :] = top2 / score


def _router(logits, router_b):
    n = logits.shape[0]
    e = router_b.shape[0]
    tm = 512
    return pl.pallas_call(
        functools.partial(_router_body, e),
        grid=(n // tm,),
        in_specs=[pl.BlockSpec((tm, LANES), lambda i: (i, 0)),
                  pl.BlockSpec((e, 1), lambda i: (0, 0))],
        out_specs=[pl.BlockSpec((8, tm), lambda i: (0, i)), pl.BlockSpec((8, tm), lambda i: (0, i))],
        out_shape=[jax.ShapeDtypeStruct((8, n), jnp.int32), jax.ShapeDtypeStruct((8, n), F32)],
        compiler_params=_params(("arbitrary",)),
        name="router",
    )(logits, router_b.reshape(e, 1).astype(F32))


def _dispatch_plan(ids, n_experts, n_tiles):
    e0, e1 = ids[0], ids[1]
    n = e0.shape[0]
    ar = jnp.arange(n_experts, dtype=jnp.int32)[:, None]
    oh0 = (e0[None, :] == ar).astype(jnp.int32)
    oh1 = (e1[None, :] == ar).astype(jnp.int32)
    sel = oh0 + oh1
    csum = jnp.cumsum(sel, axis=1)
    counts = csum[:, -1]
    padded = ((counts + MOE_TM - 1) // MOE_TM) * MOE_TM
    ends = jnp.cumsum(padded)
    offs = ends - padded
    slot = offs[:, None] + csum - 1
    pos0 = jnp.sum(oh0 * slot, axis=0)
    pos1 = jnp.sum(oh1 * slot, axis=0)
    tok = jnp.arange(n, dtype=jnp.int32)
    src = jnp.zeros((n_tiles * MOE_TM,), jnp.int32)
    src = src.at[jnp.concatenate([pos0, pos1])].set(jnp.concatenate([tok, tok]), unique_indices=True)
    n_used = (ends[-1] // MOE_TM).astype(jnp.int32)
    tile_start = jnp.minimum(jnp.arange(n_tiles, dtype=jnp.int32), n_used - 1) * MOE_TM
    nonempty = (counts > 0).astype(jnp.int32)
    tile_seg = jnp.sum((ends[None, :] <= tile_start[:, None]).astype(jnp.int32) * nonempty[None, :], axis=1)
    seg_of_expert = jnp.cumsum(nonempty) - 1
    seg_expert = jnp.sum(jnp.where((seg_of_expert[None, :] == ar) & (nonempty[None, :] > 0), ar.T, 0), axis=1)
    meta = jnp.stack([n_used, jnp.sum(nonempty)]).astype(jnp.int32)
    return src, jnp.stack([pos0, pos1]), (tile_seg.astype(jnp.int32), seg_expert.astype(jnp.int32), meta)


def _row_copy(src_hbm, row, dst, k, sem):
    return pltpu.make_async_copy(src_hbm.at[pl.ds(row, 1), :], dst.at[pl.ds(k, 1), :], sem)


def _token_copy(src_hbm, tok, tile, g, r, sem):
    return pltpu.make_async_copy(src_hbm.at[tok], tile.at[g, :, r, :], sem)


def _gather_body(idx_ref, h_hbm, o_ref, buf, sem):
    i = pl.program_id(0)
    n = pl.num_programs(0)
    groups, pieces = buf.shape[1], buf.shape[2]

    def issue(step, slot):
        def one(g, carry):
            for r in range(SUBLANES):
                tok = idx_ref[step * ROW_TILE + g * SUBLANES + r]
                _token_copy(h_hbm, tok, buf.at[slot], g, r, sem.at[slot]).start(priority=r % 2)
            return carry
        lax.fori_loop(0, groups, one, 0)

    @pl.when(i == 0)
    def _():
        issue(0, 0)

    @pl.when(i + 1 < n)
    def _():
        issue(i + 1, (i + 1) % 2)

    slot = i % 2

    def wait_one(g, carry):
        for r in range(SUBLANES):
            _token_copy(h_hbm, 0, buf.at[slot], g, r, sem.at[slot]).wait()
        return carry
    lax.fori_loop(0, groups, wait_one, 0)
    for s in range(pieces):
        o_ref[:, s * LANES:(s + 1) * LANES] = buf[slot, :, s].reshape(ROW_TILE, LANES).astype(o_ref.dtype)


def _gather_rows(h_rows, src):
    _, pieces, _ = h_rows.shape
    d = pieces * LANES
    r = src.shape[0]
    return pl.pallas_call(
        _gather_body,
        grid_spec=pltpu.PrefetchScalarGridSpec(
            num_scalar_prefetch=1,
            grid=(r // ROW_TILE,),
            in_specs=[pl.BlockSpec(memory_space=pl.ANY)],
            out_specs=pl.BlockSpec((ROW_TILE, d), lambda i, idx: (i, 0)),
            scratch_shapes=[pltpu.VMEM((2, ROW_TILE // SUBLANES, pieces, SUBLANES, LANES), F32),
                            pltpu.SemaphoreType.DMA((2,))]),
        out_shape=jax.ShapeDtypeStruct((r, d), BF16),
        compiler_params=_params(("arbitrary",)),
        name="moe_gather",
    )(src, h_rows)


def _stream_expert_weights(seg_ref, sege_ref, meta_ref, w_hbms, chunk, wf, wb, sem):
    j, t = pl.program_id(0), pl.program_id(1)
    n_j = pl.num_programs(0)
    n_seg = meta_ref[1]
    k = seg_ref[t]
    first = (t == 0) | (k != seg_ref[jnp.maximum(t - 1, 0)])

    def copies(jj, kk):
        e = sege_ref[kk]
        return [pltpu.make_async_copy(w.at[e, :, pl.ds(jj * chunk, chunk)], wf.at[m], sem)
                for m, w in enumerate(w_hbms)]

    @pl.when(first)
    def _():
        @pl.when((j == 0) & (t == 0))
        def _():
            for c in copies(j, k):
                c.start()

        for c in copies(j, k):
            c.wait()
        for m in range(len(w_hbms)):
            _convert_rows(wb.at[m], wf.at[m])

        wrap = k + 1 == n_seg
        nj = jnp.where(wrap, j + 1, j)
        nk = jnp.where(wrap, 0, k + 1)

        @pl.when(nj < n_j)
        def _():
            for c in copies(nj, nk):
                c.start()


def _moe_up_body(seg_ref, sege_ref, meta_ref, x_ref, wg_hbm, wu_hbm, o_ref, wf, wb, sem):
    _stream_expert_weights(seg_ref, sege_ref, meta_ref, (wg_hbm, wu_hbm), o_ref.shape[1], wf, wb, sem)
    t = pl.program_id(1)

    @pl.when(t < meta_ref[0])
    def _():
        x = x_ref[...]
        g = jnp.dot(x, wb[0], preferred_element_type=F32)
        u = jnp.dot(x, wb[1], preferred_element_type=F32)
        o_ref[...] = (g * jax.nn.sigmoid(g) * u).astype(o_ref.dtype)

    @pl.when(t >= meta_ref[0])
    def _():
        o_ref[...] = jnp.zeros(o_ref.shape, o_ref.dtype)


def _moe_up(xs, w_gate, w_up, plan):
    r, d = xs.shape
    f = w_gate.shape[2]
    n_tiles = r // MOE_TM
    fc = MOE_FC if f % MOE_FC == 0 else f
    return pl.pallas_call(
        _moe_up_body,
        grid_spec=pltpu.PrefetchScalarGridSpec(
            num_scalar_prefetch=3,
            grid=(f // fc, n_tiles),
            in_specs=[pl.BlockSpec((MOE_TM, d), lambda j, t, sg, se, mt: (jnp.minimum(t, mt[0] - 1), 0)),
                      pl.BlockSpec(memory_space=pl.ANY),
                      pl.BlockSpec(memory_space=pl.ANY)],
            out_specs=pl.BlockSpec((MOE_TM, fc), lambda j, t, sg, se, mt: (t, j)),
            scratch_shapes=[pltpu.VMEM((2, d, fc), F32), pltpu.VMEM((2, d, fc), BF16),
                            pltpu.SemaphoreType.DMA(())]),
        out_shape=jax.ShapeDtypeStruct((r, f), BF16),
        compiler_params=_params(("arbitrary", "arbitrary")),
        name="moe_up",
    )(*plan, xs, w_gate, w_up)


def _moe_down_body(seg_ref, sege_ref, meta_ref, h_ref, w_hbm, o_ref, wf, wb, sem):
    _stream_expert_weights(seg_ref, sege_ref, meta_ref, (w_hbm,), o_ref.shape[1], wf, wb, sem)
    t = pl.program_id(1)

    @pl.when(t < meta_ref[0])
    def _():
        o_ref[...] = jnp.dot(h_ref[...], wb[0], preferred_element_type=F32)

    @pl.when(t >= meta_ref[0])
    def _():
        o_ref[...] = jnp.zeros(o_ref.shape, o_ref.dtype)


def _moe_down(hs, w_down, plan):
    r, f = hs.shape
    d = w_down.shape[2]
    nc = min(MOE_NC, d)
    n_tiles = r // MOE_TM
    return pl.pallas_call(
        _moe_down_body,
        grid_spec=pltpu.PrefetchScalarGridSpec(
            num_scalar_prefetch=3,
            grid=(d // nc, n_tiles),
            in_specs=[pl.BlockSpec((MOE_TM, f), lambda j, t, sg, se, mt: (jnp.minimum(t, mt[0] - 1), 0)),
                      pl.BlockSpec(memory_space=pl.ANY)],
            out_specs=pl.BlockSpec((MOE_TM, nc), lambda j, t, sg, se, mt: (t, j)),
            scratch_shapes=[pltpu.VMEM((1, f, nc), F32), pltpu.VMEM((1, f, nc), BF16),
                            pltpu.SemaphoreType.DMA(())]),
        out_shape=jax.ShapeDtypeStruct((r, d), F32),
        compiler_params=_params(("arbitrary", "arbitrary")),
        name="moe_down",
    )(*plan, hs, w_down)


def _combine_body(has_next, pos_ref, y_hbm, wt_ref, x_ref, g_ref, gate_ref, *rest):
    if has_next:
        g2_ref, sh_ref, sc_ref, xo_ref, ho_ref, buf, sem = rest
    else:
        xo_ref, buf, sem = rest
    i = pl.program_id(0)
    n = pl.num_programs(0)
    n_tok = n * ROW_TILE

    def issue(step, slot):
        def one(k, carry):
            tok = step * ROW_TILE + k
            _row_copy(y_hbm, pos_ref[tok], buf.at[slot, 0], k, sem.at[slot]).start()
            _row_copy(y_hbm, pos_ref[n_tok + tok], buf.at[slot, 1], k, sem.at[slot]).start()
            return carry
        lax.fori_loop(0, ROW_TILE, one, 0, unroll=8)

    @pl.when(i == 0)
    def _():
        issue(0, 0)

    @pl.when(i + 1 < n)
    def _():
        issue(i + 1, (i + 1) % 2)

    slot = i % 2

    def wait_one(k, carry):
        _row_copy(y_hbm, 0, buf.at[slot, 0], k, sem.at[slot]).wait()
        _row_copy(y_hbm, 0, buf.at[slot, 1], k, sem.at[slot]).wait()
        return carry
    lax.fori_loop(0, ROW_TILE, wait_one, 0, unroll=8)

    wt = wt_ref[...]
    y = wt[:, 0:1] * buf[slot, 0] + wt[:, 1:2] * buf[slot, 1]
    xn = x_ref[...] + gate_ref[0] * _rms(y, g_ref[...])
    xo_ref[...] = xn
    if has_next:
        h = _rms(xn, g2_ref[...])
        ho_ref[...] = (h * (1.0 + sc_ref[0]) + sh_ref[0]).astype(ho_ref.dtype)


def _combine(ys, pos, wts, x, gain, mod, gate_chunk, n_lat, seq, nxt=None):
    n = wts.shape[0]
    d = ys.shape[1]
    n_batch = mod.shape[0] - 1
    n_lat_tiles = min(n_lat, n) // ROW_TILE

    def mk(chunk):
        def index_map(i, pos_ref):
            return (jnp.where(i < n_lat_tiles, i // (seq // ROW_TILE), n_batch), 0, chunk)
        return index_map

    row = pl.BlockSpec((ROW_TILE, d), lambda i, p: (i, 0))
    vec = pl.BlockSpec((1, d), lambda i, p: (0, 0))
    in_specs = [pl.BlockSpec(memory_space=pl.ANY),
                pl.BlockSpec((ROW_TILE, 2), lambda i, p: (i, 0)),
                row, vec, pl.BlockSpec((1, 1, d), mk(gate_chunk))]
    args = [ys, wts, x, gain.reshape(1, d), mod]
    out_specs = [row]
    out_shape = [jax.ShapeDtypeStruct((n, d), F32)]
    if nxt is not None:
        gain2, mod2, shift_chunk, scale_chunk, h_dtype = nxt
        in_specs += [vec, pl.BlockSpec((1, 1, d), mk(shift_chunk)), pl.BlockSpec((1, 1, d), mk(scale_chunk))]
        args += [gain2.reshape(1, d), mod2, mod2]
        out_specs.append(row)
        out_shape.append(jax.ShapeDtypeStruct((n, d), h_dtype))
    out = pl.pallas_call(
        functools.partial(_combine_body, nxt is not None),
        grid_spec=pltpu.PrefetchScalarGridSpec(
            num_scalar_prefetch=1,
            grid=(n // ROW_TILE,),
            in_specs=in_specs,
            out_specs=out_specs,
            scratch_shapes=[pltpu.VMEM((2, 2, ROW_TILE, d), F32), pltpu.SemaphoreType.DMA((2,))]),
        out_shape=out_shape,
        compiler_params=_params(("arbitrary",)),
        name="moe_combine",
    )(pos.reshape(-1), *args)
    return out if nxt is not None else out[0]


def _moe(h_rows, logits, router_b, w_gate, w_up, w_down):
    n = logits.shape[0]
    n_experts = router_b.shape[0]
    n_tiles = (TOP_K * n) // MOE_TM + n_experts
    ids, wts = _router(logits, router_b)
    src, pos, plan = _dispatch_plan(ids, n_experts, n_tiles)
    xs = _gather_rows(h_rows, src)
    hs = _moe_up(xs, w_gate, w_up, plan)
    ys = _moe_down(hs, w_down, plan)
    return ys, pos, wts[:2].T


def kernel(x, c, ctx, c_ctx, router_w, router_b, l0_ada_w, l0_ada_b, l0_norm_pre_mix, l0_norm_post_mix, l0_norm_pre_ffn, l0_norm_post_ffn, l0_na_w_qkv, l0_na_rpb, l0_na_w_o, l0_moe_w_gate, l0_moe_w_up, l0_moe_w_down, l1_ada_w, l1_ada_b, l1_norm_pre_mix, l1_norm_post_mix, l1_norm_pre_ffn, l1_norm_post_ffn, l1_gqa_w_qkv, l1_gqa_q_gain, l1_gqa_k_gain, l1_gqa_w_o, l1_moe_w_gate, l1_moe_w_up, l1_moe_w_down):
    n_batch, seq, d = x.shape
    ctx_len = ctx.shape[1]
    n_lat = n_batch * seq
    n_all = n_lat + n_batch * ctx_len
    scale = HEAD_DIM ** -0.5
    SH_M, SC_M, G_M, SH_F, SC_F, G_F = range(6)

    cvec = jnp.zeros((8, d), F32).at[:n_batch].set(c).at[n_batch].set(c_ctx)
    mod0 = _adaln(cvec, l0_ada_w, l0_ada_b)[:n_batch + 1].reshape(n_batch + 1, 1, 6 * d)
    mod1 = _adaln(cvec, l1_ada_w, l1_ada_b)[:n_batch + 1].reshape(n_batch + 1, 1, 6 * d)

    x_lat = x.reshape(n_lat, d)
    x_ctx = ctx.reshape(n_batch * ctx_len, d)

    h = _norm_mod(x_lat, x_ctx, l0_norm_pre_mix, mod0, SH_M, SC_M, seq, BF16)
    qscale = jnp.concatenate([jnp.full((1, d), scale, F32), jnp.ones((1, 2 * d), F32)], axis=1)
    qkv = _matmul(h, l0_na_w_qkv, qscale, BF16)
    bias = _na_bias_table(l0_na_rpb, seq // GRID_W)
    att = _na_attention(qkv, bias, n_batch, seq, ctx_len)
    y = _matmul(att, l0_na_w_o, jnp.ones((1, d), F32), F32)
    xa, h_rows, logits = _residual(x_lat, x_ctx, y, l0_norm_post_mix, mod0, G_M, seq,
                                   l0_norm_pre_ffn, SH_F, SC_F, router_w)
    ys, pos, wts = _moe(h_rows, logits, router_b, l0_moe_w_gate, l0_moe_w_up, l0_moe_w_down)
    xa, h = _combine(ys, pos, wts, xa, l0_norm_post_ffn, mod0, G_F, n_lat, seq,
                     nxt=(l1_norm_pre_mix, mod1, SH_M, SC_M, BF16))

    q_cols = d
    kv_cols = d // GQA_REP
    cos, sin = _rope_tables(seq, n_lat)
    ident = (jnp.ones((n_all - n_lat, HEAD_DIM), F32), jnp.zeros((n_all - n_lat, HEAD_DIM), F32))
    cos = jnp.concatenate([cos, ident[0]], axis=0)
    sin = jnp.concatenate([sin, ident[1]], axis=0)
    gain_post = jnp.stack([jnp.stack([l1_gqa_q_gain.astype(F32), jnp.full((HEAD_DIM,), scale, F32)]),
                           jnp.stack([l1_gqa_k_gain.astype(F32), jnp.ones((HEAD_DIM,), F32)])])
    qkv = _qkv_gqa(h, l1_gqa_w_qkv, cos, sin, gain_post, q_cols, kv_cols)
    att = _gqa_attention(qkv, qkv, n_batch, seq, ctx_len, q_cols)
    y = _matmul(att, l1_gqa_w_o, jnp.ones((1, d), F32), F32)
    xl, h_rows, logits = _residual(xa, xa, y, l1_norm_post_mix, mod1, G_M, seq,
                                   l1_norm_pre_ffn, SH_F, SC_F, router_w)
    ys, pos, wts = _moe(h_rows, logits, router_b, l1_moe_w_gate, l1_moe_w_up, l1_moe_w_down)
    xl = _combine(ys, pos, wts, xl, l1_norm_post_ffn, mod1, G_F, n_lat, seq)
    return xl.reshape(n_batch, seq, d)
```

```python
import functools

import jax
import jax.numpy as jnp
from jax import lax
from jax.experimental import pallas as pl
from jax.experimental.pallas import tpu as pltpu

LANES = 128
SUBLANES = 8
BF16_SUBLANES = 16
GRID_W = 64
HEAD_DIM = 128
NA_WIN_H = 8
NA_WIN_W = 16
GQA_REP = 4
ROPE_THETA = 10000.0
N_GROUPS = 4
TOP_K = 2
EPS = 1e-6
NEG_INF = -1e30

F32 = jnp.float32
BF16 = jnp.bfloat16

ROW_TILE = 256
MM_TM = 1024
MM_TN = 512
MOE_TM = 256
MOE_FC = 768
MOE_NC = 4096
NA_RB = 4
NA_HG = 8
GQA_QB = 512
ADA_TN = 512
VMEM_LIMIT = 56 * 1024 * 1024


def _params(sem):
    return pltpu.CompilerParams(dimension_semantics=sem, vmem_limit_bytes=VMEM_LIMIT)


def _adaln_body(c_ref, w_ref, b_ref, o_ref):
    c = c_ref[...]
    s = (c * jax.nn.sigmoid(c)).astype(BF16)
    o_ref[...] = jnp.dot(s, w_ref[...].astype(BF16), preferred_element_type=F32) + b_ref[...]


def _adaln(cvec, ada_w, ada_b):
    m, d = cvec.shape
    n = ada_w.shape[1]
    return pl.pallas_call(
        _adaln_body,
        grid=(n // ADA_TN,),
        in_specs=[pl.BlockSpec((m, d), lambda j: (0, 0)),
                  pl.BlockSpec((d, ADA_TN), lambda j: (0, j)),
                  pl.BlockSpec((1, ADA_TN), lambda j: (0, j))],
        out_specs=pl.BlockSpec((m, ADA_TN), lambda j: (0, j)),
        out_shape=jax.ShapeDtypeStruct((m, n), F32),
        compiler_params=_params(("arbitrary",)),
        name="adaln",
    )(cvec, ada_w, ada_b.reshape(1, n))


def _convert_rows(dst_ref, src_ref):
    rows = src_ref.shape[0]
    band = next(b for b in (256, 128, 64, 32, 16) if rows % b == 0)

    def one(i, carry):
        r0 = pl.multiple_of(i * band, band)
        dst_ref[pl.ds(r0, band), :] = src_ref[pl.ds(r0, band), :].astype(dst_ref.dtype)
        return carry
    lax.fori_loop(0, rows // band, one, 0)


def _rms(x, gain):
    return x * lax.rsqrt(jnp.mean(x * x, axis=-1, keepdims=True) + EPS) * gain


def _stream_rows(n_lat_tiles, xl_ref, xc_ref):
    return jnp.where(pl.program_id(0) < n_lat_tiles, xl_ref[...], xc_ref[...])


def _stream_specs(n_lat_tiles, d):
    return [pl.BlockSpec((ROW_TILE, d), lambda i: (jnp.minimum(i, n_lat_tiles - 1), 0)),
            pl.BlockSpec((ROW_TILE, d), lambda i: (jnp.maximum(i - n_lat_tiles, 0), 0))]


def _norm_mod_body(n_lat_tiles, xl_ref, xc_ref, g_ref, sh_ref, sc_ref, o_ref):
    y = _rms(_stream_rows(n_lat_tiles, xl_ref, xc_ref), g_ref[...])
    o_ref[...] = (y * (1.0 + sc_ref[0]) + sh_ref[0]).astype(o_ref.dtype)


def _mod_row_map(n_lat_tiles, tiles_per_batch, n_batch, chunk):
    def index_map(i):
        return (jnp.where(i < n_lat_tiles, i // tiles_per_batch, n_batch), 0, chunk)
    return index_map


def _norm_mod(x_lat, x_ctx, gain, mod, shift_chunk, scale_chunk, seq, out_dtype):
    n_lat, d = x_lat.shape
    n = n_lat + x_ctx.shape[0]
    n_batch = mod.shape[0] - 1
    n_lat_tiles = n_lat // ROW_TILE
    mk = functools.partial(_mod_row_map, n_lat_tiles, seq // ROW_TILE, n_batch)
    return pl.pallas_call(
        functools.partial(_norm_mod_body, n_lat_tiles),
        grid=(n // ROW_TILE,),
        in_specs=_stream_specs(n_lat_tiles, d) + [
            pl.BlockSpec((1, d), lambda i: (0, 0)),
            pl.BlockSpec((1, 1, d), mk(shift_chunk)),
            pl.BlockSpec((1, 1, d), mk(scale_chunk))],
        out_specs=pl.BlockSpec((ROW_TILE, d), lambda i: (i, 0)),
        out_shape=jax.ShapeDtypeStruct((n, d), out_dtype),
        compiler_params=_params(("arbitrary",)),
        name="norm_mod",
    )(x_lat, x_ctx, gain.reshape(1, d), mod, mod)


def _split_bf16(x):
    hi = x.astype(BF16)
    lo = (x - hi.astype(F32)).astype(BF16)
    return hi, lo


def _tile_to_vregs(tile_ref, value):
    groups, pieces = tile_ref.shape[0], tile_ref.shape[1]
    for s in range(pieces):
        tile_ref[:, s] = value[:, s * LANES:(s + 1) * LANES].reshape(groups, SUBLANES, LANES)


def _residual_body(n_lat_tiles, xl_ref, xc_ref, y_ref, g_ref, gate_ref, g2_ref, sh_ref, sc_ref, rw_ref,
                   xo_ref, hr_hbm, lg_ref, hbuf, sem):
    i = pl.program_id(0)
    n = pl.num_programs(0)
    groups = hbuf.shape[1]
    slot = i % 2

    def copies(step, slot):
        return [pltpu.make_async_copy(hbuf.at[slot, :, :, r, :], hr_hbm.at[pl.ds(step * groups, groups), r],
                                      sem.at[slot]) for r in range(SUBLANES)]

    @pl.when(i >= 2)
    def _():
        for c in copies(i - 2, slot):
            c.wait()

    xn = _stream_rows(n_lat_tiles, xl_ref, xc_ref) + gate_ref[0] * _rms(y_ref[...], g_ref[...])
    xo_ref[...] = xn
    h = _rms(xn, g2_ref[...]) * (1.0 + sc_ref[0]) + sh_ref[0]
    _tile_to_vregs(hbuf.at[slot], h)
    for c in copies(i, slot):
        c.start()
    h_hi, h_lo = _split_bf16(h)
    w_hi, w_lo = _split_bf16(rw_ref[...])
    lg_ref[...] = (jnp.dot(h_hi, w_hi, preferred_element_type=F32)
                   + jnp.dot(h_lo, w_hi, preferred_element_type=F32)
                   + jnp.dot(h_hi, w_lo, preferred_element_type=F32))

    @pl.when((i == n - 1) & (i >= 1))
    def _():
        for c in copies(i - 1, 1 - slot):
            c.wait()

    @pl.when(i == n - 1)
    def _():
        for c in copies(i, slot):
            c.wait()


def _residual(x_lat, x_ctx, y, gain, mod, gate_chunk, seq, gain2, shift_chunk, scale_chunk, router_w):
    n, d = y.shape
    n_batch = mod.shape[0] - 1
    pieces = d // LANES
    groups = ROW_TILE // SUBLANES
    n_lat_tiles = min(x_lat.shape[0], n) // ROW_TILE
    mk = functools.partial(_mod_row_map, n_lat_tiles, seq // ROW_TILE, n_batch)
    row = pl.BlockSpec((ROW_TILE, d), lambda i: (i, 0))
    vec = pl.BlockSpec((1, d), lambda i: (0, 0))
    e = router_w.shape[1]
    w_pad = jnp.pad(router_w, ((0, 0), (0, LANES - e)))
    xn, h_rows, logits = pl.pallas_call(
        functools.partial(_residual_body, n_lat_tiles),
        grid=(n // ROW_TILE,),
        in_specs=_stream_specs(n_lat_tiles, d) + [
            row, vec, pl.BlockSpec((1, 1, d), mk(gate_chunk)), vec,
            pl.BlockSpec((1, 1, d), mk(shift_chunk)), pl.BlockSpec((1, 1, d), mk(scale_chunk)),
            pl.BlockSpec((d, LANES), lambda i: (0, 0))],
        out_specs=[row, pl.BlockSpec(memory_space=pl.ANY), pl.BlockSpec((ROW_TILE, LANES), lambda i: (i, 0))],
        out_shape=[jax.ShapeDtypeStruct((n, d), F32),
                   jax.ShapeDtypeStruct((n // SUBLANES, SUBLANES, pieces, LANES), F32),
                   jax.ShapeDtypeStruct((n, LANES), F32)],
        scratch_shapes=[pltpu.VMEM((2, groups, pieces, SUBLANES, LANES), F32), pltpu.SemaphoreType.DMA((2,))],
        compiler_params=_params(("arbitrary",)),
        name="residual",
    )(x_lat, x_ctx, y, gain.reshape(1, d), mod, gain2.reshape(1, d), mod, mod, w_pad)
    return xn, h_rows.reshape(n, pieces, LANES), logits


def _matmul_body(a_ref, w_ref, s_ref, o_ref, wb_ref):
    @pl.when(pl.program_id(1) == 0)
    def _():
        _convert_rows(wb_ref, w_ref)

    acc = jnp.dot(a_ref[...], wb_ref[...], preferred_element_type=F32)
    o_ref[...] = (acc * s_ref[...]).astype(o_ref.dtype)


def _row_tile(m):
    return next(t for t in (MM_TM, MM_TM // 2, MM_TM // 4) if m % t == 0)


def _matmul(a, w, col_scale, out_dtype):
    m, k = a.shape
    n = w.shape[1]
    tm = _row_tile(m)
    return pl.pallas_call(
        _matmul_body,
        grid=(n // MM_TN, m // tm),
        in_specs=[pl.BlockSpec((tm, k), lambda j, i: (i, 0)),
                  pl.BlockSpec((k, MM_TN), lambda j, i: (0, j)),
                  pl.BlockSpec((1, MM_TN), lambda j, i: (0, j))],
        out_specs=pl.BlockSpec((tm, MM_TN), lambda j, i: (i, j)),
        out_shape=jax.ShapeDtypeStruct((m, n), out_dtype),
        scratch_shapes=[pltpu.VMEM((k, MM_TN), BF16)],
        compiler_params=_params(("arbitrary", "arbitrary")),
        name="matmul",
    )(a, w, col_scale)


def _rope_head(y, cos, sin, even):
    partner = jnp.where(even, pltpu.roll(y, HEAD_DIM - 1, 1), pltpu.roll(y, 1, 1))
    return y * cos + partner * sin


def _qkv_gqa_body(n_qk_tiles, a_ref, w_ref, cos_ref, sin_ref, gp_ref, o_ref, wb_ref):
    j = pl.program_id(0)

    @pl.when(pl.program_id(1) == 0)
    def _():
        _convert_rows(wb_ref, w_ref)

    @pl.when(j < n_qk_tiles)
    def _():
        gain = gp_ref[0, 0:1, :]
        post = gp_ref[0, 1:2, :]
        band = min(ROW_TILE, a_ref.shape[0])
        even = (lax.broadcasted_iota(jnp.int32, (band, HEAD_DIM), 1) % 2) == 0
        for r0 in range(0, a_ref.shape[0], band):
            rows = slice(r0, r0 + band)
            acc = jnp.dot(a_ref[rows, :], wb_ref[...], preferred_element_type=F32)
            cos = cos_ref[rows, :]
            sin = sin_ref[rows, :]
            for h in range(acc.shape[1] // HEAD_DIM):
                cols = slice(h * HEAD_DIM, (h + 1) * HEAD_DIM)
                y = _rope_head(_rms(acc[:, cols], gain), cos, sin, even)
                o_ref[rows, cols] = (y * post).astype(o_ref.dtype)

    @pl.when(j >= n_qk_tiles)
    def _():
        o_ref[...] = jnp.dot(a_ref[...], wb_ref[...], preferred_element_type=F32).astype(o_ref.dtype)


def _qkv_gqa(a, w, cos, sin, gain_post, q_cols, kv_cols):
    m, k = a.shape
    n = w.shape[1]
    tm = _row_tile(m)
    tn = min(MM_TN, kv_cols)
    assert q_cols % tn == 0 and kv_cols % tn == 0
    n_q_tiles = q_cols // tn
    n_qk_tiles = (q_cols + kv_cols) // tn
    return pl.pallas_call(
        functools.partial(_qkv_gqa_body, n_qk_tiles),
        grid=(n // tn, m // tm),
        in_specs=[pl.BlockSpec((tm, k), lambda j, i: (i, 0)),
                  pl.BlockSpec((k, tn), lambda j, i: (0, j)),
                  pl.BlockSpec((tm, HEAD_DIM), lambda j, i: (i, 0)),
                  pl.BlockSpec((tm, HEAD_DIM), lambda j, i: (i, 0)),
                  pl.BlockSpec((1, 2, HEAD_DIM), lambda j, i: (jnp.where(j < n_q_tiles, 0, 1), 0, 0))],
        out_specs=pl.BlockSpec((tm, tn), lambda j, i: (i, j)),
        out_shape=jax.ShapeDtypeStruct((m, n), BF16),
        scratch_shapes=[pltpu.VMEM((k, tn), BF16)],
        compiler_params=_params(("arbitrary", "arbitrary")),
        name="qkv_gqa",
    )(a, w, cos, sin, gain_post)


def _softmax_pv(s_parts, v_parts):
    m = s_parts[0].max(axis=-1, keepdims=True)
    for s in s_parts[1:]:
        m = jnp.maximum(m, s.max(axis=-1, keepdims=True))
    den = None
    acc = None
    for s, v in zip(s_parts, v_parts):
        p = jnp.exp(s - m)
        l = p.sum(axis=-1, keepdims=True)
        o = jnp.dot(p.astype(BF16), v, preferred_element_type=F32)
        den = l if den is None else den + l
        acc = o if acc is None else acc + o
    return acc / den


def _qkt(q, k):
    return lax.dot_general(q, k, (((1,), (1,)), ((), ())), preferred_element_type=F32)


def _na_plan(rows):
    kh = min(NA_WIN_H, rows)
    uw = NA_RB + kh - 1
    uw += uw % 2
    assert rows >= uw and rows % NA_RB == 0
    sigs, cls_of_rb = [], []
    for rb in range(rows // NA_RB):
        rs_blk = min(max(rb * NA_RB - kh // 2, 0), rows - uw)
        offs = []
        for r in range(rb * NA_RB, (rb + 1) * NA_RB):
            rs = min(max(r - kh // 2, 0), rows - kh)
            assert rs_blk <= rs and rs + kh <= rs_blk + uw
            offs.append(rs - r)
        sig = (rs_blk - rb * NA_RB, tuple(offs))
        if sig not in sigs:
            sigs.append(sig)
        cls_of_rb.append(sigs.index(sig))
    return kh, uw, sigs, cls_of_rb


def _na_body(n_row_blocks, rows, q_ref, k_ref, v_ref, kc_ref, vc_ref, b_ref, o_ref):
    rb = pl.program_id(2)
    kh, uw, _, cls_of_rb = _na_plan(rows)
    span = uw * GRID_W

    @pl.when(rb < n_row_blocks)
    def _():
        rs = jnp.clip(rb * NA_RB - kh // 2, 0, rows - uw)
        k0 = pl.multiple_of(rs * GRID_W, GRID_W)
        cls = jnp.int32(0)
        for i, c in enumerate(cls_of_rb):
            cls = jnp.where(rb == i, c, cls)
        for g in range(NA_HG):
            cols = slice(g * HEAD_DIM, (g + 1) * HEAD_DIM)
            q = q_ref[:, cols]
            s_loc = _qkt(q, k_ref[pl.ds(k0, span), cols]) + b_ref[cls, g]
            s_ctx = _qkt(q, kc_ref[:, cols])
            o = _softmax_pv([s_loc, s_ctx], [v_ref[pl.ds(k0, span), cols], vc_ref[:, cols]])
            o_ref[:, cols] = o.astype(o_ref.dtype)

    @pl.when(rb >= n_row_blocks)
    def _():
        for g in range(NA_HG):
            cols = slice(g * HEAD_DIM, (g + 1) * HEAD_DIM)
            s = _qkt(q_ref[:, cols], kc_ref[:, cols])
            o_ref[:, cols] = _softmax_pv([s], [vc_ref[:, cols]]).astype(o_ref.dtype)


def _na_bias_body(rows, band_ref, o_ref):
    kh, uw, sigs, _ = _na_plan(rows)
    neg = jnp.full((GRID_W, GRID_W), NEG_INF, F32)
    for c, (off, band_offs) in enumerate(sigs):
        @pl.when(pl.program_id(0) == c)
        def _():
            for i in range(NA_RB):
                for j in range(uw):
                    inside = 0 <= off + j - i - band_offs[i] < kh
                    tile = band_ref[0, off + j - i + NA_WIN_H - 1] if inside else neg
                    o_ref[0, 0, i * GRID_W:(i + 1) * GRID_W, j * GRID_W:(j + 1) * GRID_W] = tile


def _na_bias_table(rpb, rows):
    kh, uw, sigs, _ = _na_plan(rows)
    heads = rpb.shape[0]
    qcol = jnp.arange(GRID_W)
    col_start = jnp.clip(qcol - NA_WIN_W // 2, 0, GRID_W - NA_WIN_W)
    col_valid = (qcol[None, :] >= col_start[:, None]) & (qcol[None, :] < col_start[:, None] + NA_WIN_W)
    dc_idx = jnp.clip(qcol[None, :] - qcol[:, None], -(NA_WIN_W - 1), NA_WIN_W - 1) + NA_WIN_W - 1
    onehot = (dc_idx[None] == jnp.arange(2 * NA_WIN_W - 1)[:, None, None]).astype(F32)
    band = jnp.einsum('hrc,cqk->hrqk', rpb.astype(F32), onehot, precision=lax.Precision.HIGHEST)
    band = jnp.where(col_valid[None, None], band, NEG_INF)
    n_dr = band.shape[1]
    return pl.pallas_call(
        functools.partial(_na_bias_body, rows),
        grid=(len(sigs), heads),
        in_specs=[pl.BlockSpec((1, n_dr, GRID_W, GRID_W), lambda c, h: (h, 0, 0, 0))],
        out_specs=pl.BlockSpec((1, 1, NA_RB * GRID_W, uw * GRID_W), lambda c, h: (c, h, 0, 0)),
        out_shape=jax.ShapeDtypeStruct((len(sigs), heads, NA_RB * GRID_W, uw * GRID_W), F32),
        compiler_params=_params(("arbitrary", "arbitrary")),
        name="na_bias",
    )(band)


def _na_attention(qkv, bias, n_batch, seq, ctx_len):
    n, d3 = qkv.shape
    d = d3 // 3
    heads = d // HEAD_DIM
    rows = seq // GRID_W
    qb = NA_RB * GRID_W
    assert seq % qb == 0 and ctx_len % qb == 0 and heads % NA_HG == 0
    n_row_blocks = seq // qb
    n_ctx_blocks = ctx_len // qb
    hgs = heads // NA_HG
    w = NA_HG * HEAD_DIM
    lat_blocks = n_batch * n_row_blocks

    def q_map(hg, b, rb):
        blk = jnp.where(rb < n_row_blocks, b * n_row_blocks + rb, lat_blocks + b * n_ctx_blocks + rb - n_row_blocks)
        return (blk, hg)

    ctx_blk0 = (n_batch * seq) // ctx_len
    n_cls, _, bq, bk = bias.shape
    return pl.pallas_call(
        functools.partial(_na_body, n_row_blocks, rows),
        grid=(hgs, n_batch, n_row_blocks + n_ctx_blocks),
        in_specs=[pl.BlockSpec((qb, w), q_map),
                  pl.BlockSpec((seq, w), lambda hg, b, rb: (b, hgs + hg)),
                  pl.BlockSpec((seq, w), lambda hg, b, rb: (b, 2 * hgs + hg)),
                  pl.BlockSpec((ctx_len, w), lambda hg, b, rb: (ctx_blk0 + b, hgs + hg)),
                  pl.BlockSpec((ctx_len, w), lambda hg, b, rb: (ctx_blk0 + b, 2 * hgs + hg)),
                  pl.BlockSpec((n_cls, NA_HG, bq, bk), lambda hg, b, rb: (0, hg, 0, 0),
                               pipeline_mode=pl.Buffered(1))],
        out_specs=pl.BlockSpec((qb, w), q_map),
        out_shape=jax.ShapeDtypeStruct((n, d), BF16),
        compiler_params=_params(("arbitrary", "arbitrary", "arbitrary")),
        name="na_attention",
    )(qkv, qkv, qkv, qkv, qkv, bias)


def _rope_tables(seq, n_rows):
    t = jnp.arange(seq, dtype=jnp.int32)
    row = (t // GRID_W).astype(F32)
    col = (t % GRID_W).astype(F32)
    axis_dim = HEAD_DIM // 2
    inv_freq = 1.0 / (ROPE_THETA ** (jnp.arange(0, axis_dim, 2, dtype=F32) / axis_dim))
    ang = jnp.concatenate([row[:, None] * inv_freq, col[:, None] * inv_freq], axis=-1)
    cos = jnp.repeat(jnp.cos(ang), 2, axis=-1)
    sin = jnp.repeat(jnp.sin(ang), 2, axis=-1) * jnp.tile(jnp.array([-1.0, 1.0], F32), HEAD_DIM // 2)
    reps = n_rows // seq
    return jnp.tile(cos, (reps, 1)), jnp.tile(sin, (reps, 1))


def _gqa_body(q_ref, k_ref, kc_ref, v_ref, vc_ref, o_ref, ka_ref, va_ref):
    seq = k_ref.shape[0]

    @pl.when(pl.program_id(2) == 0)
    def _():
        ka_ref[0:seq, :] = k_ref[...]
        ka_ref[seq:, :] = kc_ref[...]
        va_ref[0:seq, 0:HEAD_DIM] = v_ref[...]
        va_ref[seq:, 0:HEAD_DIM] = vc_ref[...]
        va_ref[:, HEAD_DIM:] = jnp.ones((va_ref.shape[0], va_ref.shape[1] - HEAD_DIM), BF16)

    for r in range(GQA_REP):
        cols = slice(r * HEAD_DIM, (r + 1) * HEAD_DIM)
        s = _qkt(q_ref[:, cols], ka_ref[...])
        p = jnp.exp(s - s.max(axis=-1, keepdims=True)).astype(BF16)
        ox = jnp.dot(p, va_ref[...], preferred_element_type=F32)
        o_ref[:, cols] = (ox[:, 0:HEAD_DIM] / ox[:, HEAD_DIM:HEAD_DIM + 1]).astype(o_ref.dtype)


def _gqa_attention(qk, qkv, n_batch, seq, ctx_len, q_cols):
    kv_heads = q_cols // HEAD_DIM // GQA_REP
    assert seq % GQA_QB == 0 and (n_batch * seq) % ctx_len == 0
    w = GQA_REP * HEAD_DIM
    qblocks = seq // GQA_QB
    k_blk0 = q_cols // HEAD_DIM
    v_blk0 = k_blk0 + kv_heads
    ctx_blk0 = (n_batch * seq) // ctx_len
    return pl.pallas_call(
        _gqa_body,
        grid=(n_batch, kv_heads, qblocks),
        in_specs=[pl.BlockSpec((GQA_QB, w), lambda b, h, i: (b * qblocks + i, h)),
                  pl.BlockSpec((seq, HEAD_DIM), lambda b, h, i: (b, k_blk0 + h)),
                  pl.BlockSpec((ctx_len, HEAD_DIM), lambda b, h, i: (ctx_blk0 + b, k_blk0 + h)),
                  pl.BlockSpec((seq, HEAD_DIM), lambda b, h, i: (b, v_blk0 + h)),
                  pl.BlockSpec((ctx_len, HEAD_DIM), lambda b, h, i: (ctx_blk0 + b, v_blk0 + h))],
        out_specs=pl.BlockSpec((GQA_QB, w), lambda b, h, i: (b * qblocks + i, h)),
        out_shape=jax.ShapeDtypeStruct((n_batch * seq, q_cols), BF16),
        scratch_shapes=[pltpu.VMEM((seq + ctx_len, HEAD_DIM), BF16),
                        pltpu.VMEM((seq + ctx_len, 2 * HEAD_DIM), BF16)],
        compiler_params=_params(("arbitrary", "arbitrary", "arbitrary")),
        name="gqa_attention",
    )(qk, qk, qk, qkv, qkv)


def _router_body(n_experts, lg_ref, b_ref, id_ref, wt_ref):
    lt = lg_ref[...].T[:n_experts] + b_ref[...]
    rows = [lt[e:e + 1] for e in range(n_experts)]
    mx = functools.reduce(jnp.maximum, rows)
    ex = [jnp.exp(r - mx) for r in rows]
    den = functools.reduce(jnp.add, ex)
    probs = [e / den for e in ex]

    epg = n_experts // N_GROUPS
    best = None
    for g in range(N_GROUPS):
        p = probs[g * epg:(g + 1) * epg]
        top1 = functools.reduce(jnp.maximum, p)
        i1 = jnp.full(top1.shape, epg, jnp.int32)
        for j in reversed(range(epg)):
            i1 = jnp.where(p[j] == top1, j, i1)
        rest = [jnp.where(i1 == j, -1.0, p[j]) for j in range(epg)]
        top2 = functools.reduce(jnp.maximum, rest)
        i2 = jnp.full(top1.shape, epg, jnp.int32)
        for j in reversed(range(epg)):
            i2 = jnp.where((rest[j] == top2) & (i1 != j), j, i2)
        cand = (top1 + top2, top1, top2, i1 + g * epg, i2 + g * epg)
        if best is None:
            best = cand
        else:
            take = cand[0] > best[0]
            best = tuple(jnp.where(take, c, b) for c, b in zip(cand, best))
    score, top1, top2, e1, e2 = best
    id_ref[...] = jnp.zeros(id_ref.shape, jnp.int32)
    wt_ref[...] = jnp.zeros(wt_ref.shape, F32)
    id_ref[0:1, :] = e1
    id_ref[1:2, :] = e2
    wt_ref[0:1, :] = top1 / score
    wt_ref[1:2, :] = top2 / score


def _router(logits, router_b):
    n = logits.shape[0]
    e = router_b.shape[0]
    tm = 512
    return pl.pallas_call(
        functools.partial(_router_body, e),
        grid=(n // tm,),
        in_specs=[pl.BlockSpec((tm, LANES), lambda i: (i, 0)),
                  pl.BlockSpec((e, 1), lambda i: (0, 0))],
        out_specs=[pl.BlockSpec((8, tm), lambda i: (0, i)), pl.BlockSpec((8, tm), lambda i: (0, i))],
        out_shape=[jax.ShapeDtypeStruct((8, n), jnp.int32), jax.ShapeDtypeStruct((8, n), F32)],
        compiler_params=_params(("arbitrary",)),
        name="router",
    )(logits, router_b.reshape(e, 1).astype(F32))


def _dispatch_plan(ids, n_experts, n_tiles):
    e0, e1 = ids[0], ids[1]
    n = e0.shape[0]
    ar = jnp.arange(n_experts, dtype=jnp.int32)[:, None]
    oh0 = (e0[None, :] == ar).astype(jnp.int32)
    oh1 = (e1[None, :] == ar).astype(jnp.int32)
    sel = oh0 + oh1
    csum = jnp.cumsum(sel, axis=1)
    counts = csum[:, -1]
    padded = ((counts + MOE_TM - 1) // MOE_TM) * MOE_TM
    ends = jnp.cumsum(padded)
    offs = ends - padded
    slot = offs[:, None] + csum - 1
    pos0 = jnp.sum(oh0 * slot, axis=0)
    pos1 = jnp.sum(oh1 * slot, axis=0)
    tok = jnp.arange(n, dtype=jnp.int32)
    src = jnp.zeros((n_tiles * MOE_TM,), jnp.int32)
    src = src.at[jnp.concatenate([pos0, pos1])].set(jnp.concatenate([tok, tok]), unique_indices=True)
    n_used = (ends[-1] // MOE_TM).astype(jnp.int32)
    tile_start = jnp.minimum(jnp.arange(n_tiles, dtype=jnp.int32), n_used - 1) * MOE_TM
    nonempty = (counts > 0).astype(jnp.int32)
    tile_seg = jnp.sum((ends[None, :] <= tile_start[:, None]).astype(jnp.int32) * nonempty[None, :], axis=1)
    seg_of_expert = jnp.cumsum(nonempty) - 1
    seg_expert = jnp.sum(jnp.where((seg_of_expert[None, :] == ar) & (nonempty[None, :] > 0), ar.T, 0), axis=1)
    meta = jnp.stack([n_used, jnp.sum(nonempty)]).astype(jnp.int32)
    return src, jnp.stack([pos0, pos1]), (tile_seg.astype(jnp.int32), seg_expert.astype(jnp.int32), meta)


def _row_copy(src_hbm, row, dst, k, sem):
    return pltpu.make_async_copy(src_hbm.at[pl.ds(row, 1), :], dst.at[pl.ds(k, 1), :], sem)


def _token_copy(src_hbm, tok, tile, g, r, sem):
    return pltpu.make_async_copy(src_hbm.at[tok], tile.at[g, :, r, :], sem)


def _issue_token_gather(idx_ref, first, h_hbm, tile, sem):
    def one(g, carry):
        for r in range(SUBLANES):
            _token_copy(h_hbm, idx_ref[first + g * SUBLANES + r], tile, g, r, sem).start(priority=r % 2)
        return carry
    lax.fori_loop(0, tile.shape[0], one, 0)


def _wait_token_gather(h_hbm, tile, sem):
    def one(g, carry):
        for r in range(SUBLANES):
            _token_copy(h_hbm, 0, tile, g, r, sem).wait()
        return carry
    lax.fori_loop(0, tile.shape[0], one, 0)


def _stream_expert_weights(seg_ref, sege_ref, meta_ref, w_hbms, chunk, wf, wb, sem):
    j, t = pl.program_id(0), pl.program_id(1)
    n_j = pl.num_programs(0)
    n_seg = meta_ref[1]
    k = seg_ref[t]
    first = (t == 0) | (k != seg_ref[jnp.maximum(t - 1, 0)])

    def copies(jj, kk):
        e = sege_ref[kk]
        return [pltpu.make_async_copy(w.at[e, :, pl.ds(jj * chunk, chunk)], wf.at[m], sem)
                for m, w in enumerate(w_hbms)]

    @pl.when(first)
    def _():
        @pl.when((j == 0) & (t == 0))
        def _():
            for c in copies(j, k):
                c.start()

        for c in copies(j, k):
            c.wait()
        for m in range(len(w_hbms)):
            _convert_rows(wb.at[m], wf.at[m])

        wrap = k + 1 == n_seg
        nj = jnp.where(wrap, j + 1, j)
        nk = jnp.where(wrap, 0, k + 1)

        @pl.when(nj < n_j)
        def _():
            for c in copies(nj, nk):
                c.start()


def _moe_up_body(seg_ref, sege_ref, meta_ref, src_ref, h_hbm, wg_hbm, wu_hbm, o_ref, xbuf, xb, gsem, wf, wb, wsem):
    j, t = pl.program_id(0), pl.program_id(1)
    n_j, n_t = pl.num_programs(0), pl.num_programs(1)
    n_used = meta_ref[0]
    slot = jnp.bitwise_and(j * n_t + t, 1)

    @pl.when((j == 0) & (t == 0))
    def _():
        _issue_token_gather(src_ref, 0, h_hbm, xbuf.at[0], gsem.at[0])

    nt = jnp.where(t + 1 < n_t, t + 1, 0)

    @pl.when(((t + 1 < n_t) | (j + 1 < n_j)) & (nt < n_used))
    def _():
        _issue_token_gather(src_ref, nt * MOE_TM, h_hbm, xbuf.at[1 - slot], gsem.at[1 - slot])

    _stream_expert_weights(seg_ref, sege_ref, meta_ref, (wg_hbm, wu_hbm), o_ref.shape[1], wf, wb, wsem)

    @pl.when(t < n_used)
    def _():
        _wait_token_gather(h_hbm, xbuf.at[slot], gsem.at[slot])
        for s in range(xbuf.shape[2]):
            xb[:, s * LANES:(s + 1) * LANES] = xbuf[slot, :, s].reshape(MOE_TM, LANES).astype(xb.dtype)
        x = xb[...]
        g = jnp.dot(x, wb[0], preferred_element_type=F32)
        u = jnp.dot(x, wb[1], preferred_element_type=F32)
        o_ref[...] = (g * jax.nn.sigmoid(g) * u).astype(o_ref.dtype)

    @pl.when(t >= n_used)
    def _():
        o_ref[...] = jnp.zeros(o_ref.shape, o_ref.dtype)


def _moe_up(h_rows, src, w_gate, w_up, plan):
    _, pieces, _ = h_rows.shape
    d = pieces * LANES
    r = src.shape[0]
    f = w_gate.shape[2]
    n_tiles = r // MOE_TM
    fc = MOE_FC if f % MOE_FC == 0 else f
    any_spec = pl.BlockSpec(memory_space=pl.ANY)
    return pl.pallas_call(
        _moe_up_body,
        grid_spec=pltpu.PrefetchScalarGridSpec(
            num_scalar_prefetch=4,
            grid=(f // fc, n_tiles),
            in_specs=[any_spec, any_spec, any_spec],
            out_specs=pl.BlockSpec((MOE_TM, fc), lambda j, t, sg, se, mt, sr: (t, j)),
            scratch_shapes=[pltpu.VMEM((2, MOE_TM // SUBLANES, pieces, SUBLANES, LANES), F32),
                            pltpu.VMEM((MOE_TM, d), BF16), pltpu.SemaphoreType.DMA((2,)),
                            pltpu.VMEM((2, d, fc), F32), pltpu.VMEM((2, d, fc), BF16),
                            pltpu.SemaphoreType.DMA(())]),
        out_shape=jax.ShapeDtypeStruct((r, f), BF16),
        compiler_params=_params(("arbitrary", "arbitrary")),
        name="moe_up",
    )(*plan, src, h_rows, w_gate, w_up)


def _moe_down_body(seg_ref, sege_ref, meta_ref, h_ref, w_hbm, o_ref, wf, wb, sem):
    _stream_expert_weights(seg_ref, sege_ref, meta_ref, (w_hbm,), o_ref.shape[1], wf, wb, sem)
    t = pl.program_id(1)

    @pl.when(t < meta_ref[0])
    def _():
        o_ref[...] = jnp.dot(h_ref[...], wb[0], preferred_element_type=F32)

    @pl.when(t >= meta_ref[0])
    def _():
        o_ref[...] = jnp.zeros(o_ref.shape, o_ref.dtype)


def _moe_down(hs, w_down, plan):
    r, f = hs.shape
    d = w_down.shape[2]
    nc = min(MOE_NC, d)
    n_tiles = r // MOE_TM
    return pl.pallas_call(
        _moe_down_body,
        grid_spec=pltpu.PrefetchScalarGridSpec(
            num_scalar_prefetch=3,
            grid=(d // nc, n_tiles),
            in_specs=[pl.BlockSpec((MOE_TM, f), lambda j, t, sg, se, mt: (jnp.minimum(t, mt[0] - 1), 0)),
                      pl.BlockSpec(memory_space=pl.ANY)],
            out_specs=pl.BlockSpec((MOE_TM, nc), lambda j, t, sg, se, mt: (t, j)),
            scratch_shapes=[pltpu.VMEM((1, f, nc), F32), pltpu.VMEM((1, f, nc), BF16),
                            pltpu.SemaphoreType.DMA(())]),
        out_shape=jax.ShapeDtypeStruct((r, d), F32),
        compiler_params=_params(("arbitrary", "arbitrary")),
        name="moe_down",
    )(*plan, hs, w_down)


def _combine_body(has_next, pos_ref, y_hbm, wt_ref, x_ref, g_ref, gate_ref, *rest):
    if has_next:
        g2_ref, sh_ref, sc_ref, xo_ref, ho_ref, buf, sem = rest
    else:
        xo_ref, buf, sem = rest
    i = pl.program_id(0)
    n = pl.num_programs(0)
    n_tok = n * ROW_TILE

    def issue(step, slot):
        def one(k, carry):
            tok = step * ROW_TILE + k
            _row_copy(y_hbm, pos_ref[tok], buf.at[slot, 0], k, sem.at[slot]).start()
            _row_copy(y_hbm, pos_ref[n_tok + tok], buf.at[slot, 1], k, sem.at[slot]).start()
            return carry
        lax.fori_loop(0, ROW_TILE, one, 0, unroll=8)

    @pl.when(i == 0)
    def _():
        issue(0, 0)

    @pl.when(i + 1 < n)
    def _():
        issue(i + 1, (i + 1) % 2)

    slot = i % 2

    def wait_one(k, carry):
        _row_copy(y_hbm, 0, buf.at[slot, 0], k, sem.at[slot]).wait()
        _row_copy(y_hbm, 0, buf.at[slot, 1], k, sem.at[slot]).wait()
        return carry
    lax.fori_loop(0, ROW_TILE, wait_one, 0, unroll=8)

    wt = wt_ref[...]
    y = wt[:, 0:1] * buf[slot, 0] + wt[:, 1:2] * buf[slot, 1]
    xn = x_ref[...] + gate_ref[0] * _rms(y, g_ref[...])
    xo_ref[...] = xn
    if has_next:
        h = _rms(xn, g2_ref[...])
        ho_ref[...] = (h * (1.0 + sc_ref[0]) + sh_ref[0]).astype(ho_ref.dtype)


def _combine(ys, pos, wts, x, gain, mod, gate_chunk, n_lat, seq, nxt=None):
    n = wts.shape[0]
    d = ys.shape[1]
    n_batch = mod.shape[0] - 1
    n_lat_tiles = min(n_lat, n) // ROW_TILE

    def mk(chunk):
        def index_map(i, pos_ref):
            return (jnp.where(i < n_lat_tiles, i // (seq // ROW_TILE), n_batch), 0, chunk)
        return index_map

    row = pl.BlockSpec((ROW_TILE, d), lambda i, p: (i, 0))
    vec = pl.BlockSpec((1, d), lambda i, p: (0, 0))
    in_specs = [pl.BlockSpec(memory_space=pl.ANY),
                pl.BlockSpec((ROW_TILE, 2), lambda i, p: (i, 0)),
                row, vec, pl.BlockSpec((1, 1, d), mk(gate_chunk))]
    args = [ys, wts, x, gain.reshape(1, d), mod]
    out_specs = [row]
    out_shape = [jax.ShapeDtypeStruct((n, d), F32)]
    if nxt is not None:
        gain2, mod2, shift_chunk, scale_chunk, h_dtype = nxt
        in_specs += [vec, pl.BlockSpec((1, 1, d), mk(shift_chunk)), pl.BlockSpec((1, 1, d), mk(scale_chunk))]
        args += [gain2.reshape(1, d), mod2, mod2]
        out_specs.append(row)
        out_shape.append(jax.ShapeDtypeStruct((n, d), h_dtype))
    out = pl.pallas_call(
        functools.partial(_combine_body, nxt is not None),
        grid_spec=pltpu.PrefetchScalarGridSpec(
            num_scalar_prefetch=1,
            grid=(n // ROW_TILE,),
            in_specs=in_specs,
            out_specs=out_specs,
            scratch_shapes=[pltpu.VMEM((2, 2, ROW_TILE, d), F32), pltpu.SemaphoreType.DMA((2,))]),
        out_shape=out_shape,
        compiler_params=_params(("arbitrary",)),
        name="moe_combine",
    )(pos.reshape(-1), *args)
    return out if nxt is not None else out[0]


def _moe(h_rows, logits, router_b, w_gate, w_up, w_down):
    n = logits.shape[0]
    n_experts = router_b.shape[0]
    n_tiles = (TOP_K * n) // MOE_TM + n_experts
    ids, wts = _router(logits, router_b)
    src, pos, plan = _dispatch_plan(ids, n_experts, n_tiles)
    hs = _moe_up(h_rows, src, w_gate, w_up, plan)
    ys = _moe_down(hs, w_down, plan)
    return ys, pos, wts[:2].T


def kernel(x, c, ctx, c_ctx, router_w, router_b, l0_ada_w, l0_ada_b, l0_norm_pre_mix, l0_norm_post_mix, l0_norm_pre_ffn, l0_norm_post_ffn, l0_na_w_qkv, l0_na_rpb, l0_na_w_o, l0_moe_w_gate, l0_moe_w_up, l0_moe_w_down, l1_ada_w, l1_ada_b, l1_norm_pre_mix, l1_norm_post_mix, l1_norm_pre_ffn, l1_norm_post_ffn, l1_gqa_w_qkv, l1_gqa_q_gain, l1_gqa_k_gain, l1_gqa_w_o, l1_moe_w_gate, l1_moe_w_up, l1_moe_w_down):
    n_batch, seq, d = x.shape
    ctx_len = ctx.shape[1]
    n_lat = n_batch * seq
    n_all = n_lat + n_batch * ctx_len
    scale = HEAD_DIM ** -0.5
    SH_M, SC_M, G_M, SH_F, SC_F, G_F = range(6)

    cvec = jnp.zeros((8, d), F32).at[:n_batch].set(c).at[n_batch].set(c_ctx)
    mod0 = _adaln(cvec, l0_ada_w, l0_ada_b)[:n_batch + 1].reshape(n_batch + 1, 1, 6 * d)
    mod1 = _adaln(cvec, l1_ada_w, l1_ada_b)[:n_batch + 1].reshape(n_batch + 1, 1, 6 * d)

    x_lat = x.reshape(n_lat, d)
    x_ctx = ctx.reshape(n_batch * ctx_len, d)

    h = _norm_mod(x_lat, x_ctx, l0_norm_pre_mix, mod0, SH_M, SC_M, seq, BF16)
    qscale = jnp.concatenate([jnp.full((1, d), scale, F32), jnp.ones((1, 2 * d), F32)], axis=1)
    qkv = _matmul(h, l0_na_w_qkv, qscale, BF16)
    bias = _na_bias_table(l0_na_rpb, seq // GRID_W)
    att = _na_attention(qkv, bias, n_batch, seq, ctx_len)
    y = _matmul(att, l0_na_w_o, jnp.ones((1, d), F32), F32)
    xa, h_rows, logits = _residual(x_lat, x_ctx, y, l0_norm_post_mix, mod0, G_M, seq,
                                   l0_norm_pre_ffn, SH_F, SC_F, router_w)
    ys, pos, wts = _moe(h_rows, logits, router_b, l0_moe_w_gate, l0_moe_w_up, l0_moe_w_down)
    xa, h = _combine(ys, pos, wts, xa, l0_norm_post_ffn, mod0, G_F, n_lat, seq,
                     nxt=(l1_norm_pre_mix, mod1, SH_M, SC_M, BF16))

    q_cols = d
    kv_cols = d // GQA_REP
    cos, sin = _rope_tables(seq, n_lat)
    ident = (jnp.ones((n_all - n_lat, HEAD_DIM), F32), jnp.zeros((n_all - n_lat, HEAD_DIM), F32))
    cos = jnp.concatenate([cos, ident[0]], axis=0)
    sin = jnp.concatenate([sin, ident[1]], axis=0)
    gain_post = jnp.stack([jnp.stack([l1_gqa_q_gain.astype(F32), jnp.full((HEAD_DIM,), scale, F32)]),
                           jnp.stack([l1_gqa_k_gain.astype(F32), jnp.ones((HEAD_DIM,), F32)])])
    qkv = _qkv_gqa(h, l1_gqa_w_qkv, cos, sin, gain_post, q_cols, kv_cols)
    att = _gqa_attention(qkv, qkv, n_batch, seq, ctx_len, q_cols)
    y = _matmul(att, l1_gqa_w_o, jnp.ones((1, d), F32), F32)
    xl, h_rows, logits = _residual(xa, xa, y, l1_norm_post_mix, mod1, G_M, seq,
                                   l1_norm_pre_ffn, SH_F, SC_F, router_w)
    ys, pos, wts = _moe(h_rows, logits, router_b, l1_moe_w_gate, l1_moe_w_up, l1_moe_w_down)
    xl = _combine(ys, pos, wts, xl, l1_norm_post_ffn, mod1, G_F, n_lat, seq)
    return xl.reshape(n_batch, seq, d)
```

```python
import functools

import jax
import jax.numpy as jnp
from jax import lax
from jax.experimental import pallas as pl
from jax.experimental.pallas import tpu as pltpu

LANES = 128
SUBLANES = 8
BF16_SUBLANES = 16
GRID_W = 64
HEAD_DIM = 128
NA_WIN_H = 8
NA_WIN_W = 16
GQA_REP = 4
ROPE_THETA = 10000.0
N_GROUPS = 4
TOP_K = 2
EPS = 1e-6
NEG_INF = -1e30

F32 = jnp.float32
BF16 = jnp.bfloat16

ROW_TILE = 256
MM_TM = 1024
MM_TN = 512
MOE_TM = 256
MOE_FC = 768
MOE_NC = 4096
NA_RB = 4
NA_HG = 8
GQA_QB = 512
GQA_BAND = 128
ADA_TN = 512
VMEM_LIMIT = 56 * 1024 * 1024


def _params(sem):
    return pltpu.CompilerParams(dimension_semantics=sem, vmem_limit_bytes=VMEM_LIMIT)


def _adaln_body(c_ref, w_ref, b_ref, o_ref):
    c = c_ref[...]
    s = (c * jax.nn.sigmoid(c)).astype(BF16)
    o_ref[...] = jnp.dot(s, w_ref[...].astype(BF16), preferred_element_type=F32) + b_ref[...]


def _adaln(cvec, ada_w, ada_b):
    m, d = cvec.shape
    n = ada_w.shape[1]
    return pl.pallas_call(
        _adaln_body,
        grid=(n // ADA_TN,),
        in_specs=[pl.BlockSpec((m, d), lambda j: (0, 0)),
                  pl.BlockSpec((d, ADA_TN), lambda j: (0, j)),
                  pl.BlockSpec((1, ADA_TN), lambda j: (0, j))],
        out_specs=pl.BlockSpec((m, ADA_TN), lambda j: (0, j)),
        out_shape=jax.ShapeDtypeStruct((m, n), F32),
        compiler_params=_params(("arbitrary",)),
        name="adaln",
    )(cvec, ada_w, ada_b.reshape(1, n))


def _convert_rows(dst_ref, src_ref):
    rows = src_ref.shape[0]
    band = next(b for b in (256, 128, 64, 32, 16) if rows % b == 0)

    def one(i, carry):
        r0 = pl.multiple_of(i * band, band)
        dst_ref[pl.ds(r0, band), :] = src_ref[pl.ds(r0, band), :].astype(dst_ref.dtype)
        return carry
    lax.fori_loop(0, rows // band, one, 0)


def _rms(x, gain):
    return x * lax.rsqrt(jnp.mean(x * x, axis=-1, keepdims=True) + EPS) * gain


def _stream_rows(n_lat_tiles, xl_ref, xc_ref):
    return jnp.where(pl.program_id(0) < n_lat_tiles, xl_ref[...], xc_ref[...])


def _stream_specs(n_lat_tiles, d):
    return [pl.BlockSpec((ROW_TILE, d), lambda i: (jnp.minimum(i, n_lat_tiles - 1), 0)),
            pl.BlockSpec((ROW_TILE, d), lambda i: (jnp.maximum(i - n_lat_tiles, 0), 0))]


def _norm_mod_body(n_lat_tiles, xl_ref, xc_ref, g_ref, sh_ref, sc_ref, o_ref):
    y = _rms(_stream_rows(n_lat_tiles, xl_ref, xc_ref), g_ref[...])
    o_ref[...] = (y * (1.0 + sc_ref[0]) + sh_ref[0]).astype(o_ref.dtype)


def _mod_row_map(n_lat_tiles, tiles_per_batch, n_batch, chunk):
    def index_map(i):
        return (jnp.where(i < n_lat_tiles, i // tiles_per_batch, n_batch), 0, chunk)
    return index_map


def _norm_mod(x_lat, x_ctx, gain, mod, shift_chunk, scale_chunk, seq, out_dtype):
    n_lat, d = x_lat.shape
    n = n_lat + x_ctx.shape[0]
    n_batch = mod.shape[0] - 1
    n_lat_tiles = n_lat // ROW_TILE
    mk = functools.partial(_mod_row_map, n_lat_tiles, seq // ROW_TILE, n_batch)
    return pl.pallas_call(
        functools.partial(_norm_mod_body, n_lat_tiles),
        grid=(n // ROW_TILE,),
        in_specs=_stream_specs(n_lat_tiles, d) + [
            pl.BlockSpec((1, d), lambda i: (0, 0)),
            pl.BlockSpec((1, 1, d), mk(shift_chunk)),
            pl.BlockSpec((1, 1, d), mk(scale_chunk))],
        out_specs=pl.BlockSpec((ROW_TILE, d), lambda i: (i, 0)),
        out_shape=jax.ShapeDtypeStruct((n, d), out_dtype),
        compiler_params=_params(("arbitrary",)),
        name="norm_mod",
    )(x_lat, x_ctx, gain.reshape(1, d), mod, mod)


def _split_bf16(x):
    hi = x.astype(BF16)
    lo = (x - hi.astype(F32)).astype(BF16)
    return hi, lo


def _residual_body(n_lat_tiles, xl_ref, xc_ref, y_ref, g_ref, gate_ref, g2_ref, sh_ref, sc_ref, rw_ref,
                   xo_ref, hr_ref, lg_ref):
    xn = _stream_rows(n_lat_tiles, xl_ref, xc_ref) + gate_ref[0] * _rms(y_ref[...], g_ref[...])
    xo_ref[...] = xn
    h = _rms(xn, g2_ref[...]) * (1.0 + sc_ref[0]) + sh_ref[0]
    hr_ref[...] = h.reshape(hr_ref.shape)
    h_hi, h_lo = _split_bf16(h)
    w_hi, w_lo = _split_bf16(rw_ref[...])
    lg_ref[...] = (jnp.dot(h_hi, w_hi, preferred_element_type=F32)
                   + jnp.dot(h_lo, w_hi, preferred_element_type=F32)
                   + jnp.dot(h_hi, w_lo, preferred_element_type=F32))


def _residual(x_lat, x_ctx, y, gain, mod, gate_chunk, seq, gain2, shift_chunk, scale_chunk, router_w):
    n, d = y.shape
    n_batch = mod.shape[0] - 1
    pieces = d // LANES
    n_lat_tiles = min(x_lat.shape[0], n) // ROW_TILE
    mk = functools.partial(_mod_row_map, n_lat_tiles, seq // ROW_TILE, n_batch)
    row = pl.BlockSpec((ROW_TILE, d), lambda i: (i, 0))
    vec = pl.BlockSpec((1, d), lambda i: (0, 0))
    e = router_w.shape[1]
    w_pad = jnp.pad(router_w, ((0, 0), (0, LANES - e)))
    return pl.pallas_call(
        functools.partial(_residual_body, n_lat_tiles),
        grid=(n // ROW_TILE,),
        in_specs=_stream_specs(n_lat_tiles, d) + [
            row, vec, pl.BlockSpec((1, 1, d), mk(gate_chunk)), vec,
            pl.BlockSpec((1, 1, d), mk(shift_chunk)), pl.BlockSpec((1, 1, d), mk(scale_chunk)),
            pl.BlockSpec((d, LANES), lambda i: (0, 0))],
        out_specs=[row, pl.BlockSpec((ROW_TILE, pieces, LANES), lambda i: (i, 0, 0)),
                   pl.BlockSpec((ROW_TILE, LANES), lambda i: (i, 0))],
        out_shape=[jax.ShapeDtypeStruct((n, d), F32), jax.ShapeDtypeStruct((n, pieces, LANES), F32),
                   jax.ShapeDtypeStruct((n, LANES), F32)],
        compiler_params=_params(("arbitrary",)),
        name="residual",
    )(x_lat, x_ctx, y, gain.reshape(1, d), mod, gain2.reshape(1, d), mod, mod, w_pad)


def _matmul_body(a_ref, w_ref, s_ref, o_ref, wb_ref):
    @pl.when(pl.program_id(1) == 0)
    def _():
        _convert_rows(wb_ref, w_ref)

    acc = jnp.dot(a_ref[...], wb_ref[...], preferred_element_type=F32)
    o_ref[...] = (acc * s_ref[...]).astype(o_ref.dtype)


def _row_tile(m):
    return next(t for t in (MM_TM, MM_TM // 2, MM_TM // 4) if m % t == 0)


def _matmul(a, w, col_scale, out_dtype):
    m, k = a.shape
    n = w.shape[1]
    tm = _row_tile(m)
    return pl.pallas_call(
        _matmul_body,
        grid=(n // MM_TN, m // tm),
        in_specs=[pl.BlockSpec((tm, k), lambda j, i: (i, 0)),
                  pl.BlockSpec((k, MM_TN), lambda j, i: (0, j)),
                  pl.BlockSpec((1, MM_TN), lambda j, i: (0, j))],
        out_specs=pl.BlockSpec((tm, MM_TN), lambda j, i: (i, j)),
        out_shape=jax.ShapeDtypeStruct((m, n), out_dtype),
        scratch_shapes=[pltpu.VMEM((k, MM_TN), BF16)],
        compiler_params=_params(("arbitrary", "arbitrary")),
        name="matmul",
    )(a, w, col_scale)


def _rope_head(y, cos, sin, even):
    partner = jnp.where(even, pltpu.roll(y, HEAD_DIM - 1, 1), pltpu.roll(y, 1, 1))
    return y * cos + partner * sin


def _qkv_gqa_body(n_qk_tiles, a_ref, w_ref, cos_ref, sin_ref, gp_ref, o_ref, wb_ref):
    j = pl.program_id(0)

    @pl.when(pl.program_id(1) == 0)
    def _():
        _convert_rows(wb_ref, w_ref)

    @pl.when(j < n_qk_tiles)
    def _():
        gain = gp_ref[0, 0:1, :]
        post = gp_ref[0, 1:2, :]
        band = min(ROW_TILE, a_ref.shape[0])
        even = (lax.broadcasted_iota(jnp.int32, (band, HEAD_DIM), 1) % 2) == 0
        for r0 in range(0, a_ref.shape[0], band):
            rows = slice(r0, r0 + band)
            acc = jnp.dot(a_ref[rows, :], wb_ref[...], preferred_element_type=F32)
            cos = cos_ref[rows, :]
            sin = sin_ref[rows, :]
            for h in range(acc.shape[1] // HEAD_DIM):
                cols = slice(h * HEAD_DIM, (h + 1) * HEAD_DIM)
                y = _rope_head(_rms(acc[:, cols], gain), cos, sin, even)
                o_ref[rows, cols] = (y * post).astype(o_ref.dtype)

    @pl.when(j >= n_qk_tiles)
    def _():
        o_ref[...] = jnp.dot(a_ref[...], wb_ref[...], preferred_element_type=F32).astype(o_ref.dtype)


def _qkv_gqa(a, w, cos, sin, gain_post, q_cols, kv_cols):
    m, k = a.shape
    n = w.shape[1]
    tm = _row_tile(m)
    tn = min(MM_TN, kv_cols)
    assert q_cols % tn == 0 and kv_cols % tn == 0
    n_q_tiles = q_cols // tn
    n_qk_tiles = (q_cols + kv_cols) // tn
    return pl.pallas_call(
        functools.partial(_qkv_gqa_body, n_qk_tiles),
        grid=(n // tn, m // tm),
        in_specs=[pl.BlockSpec((tm, k), lambda j, i: (i, 0)),
                  pl.BlockSpec((k, tn), lambda j, i: (0, j)),
                  pl.BlockSpec((tm, HEAD_DIM), lambda j, i: (i, 0)),
                  pl.BlockSpec((tm, HEAD_DIM), lambda j, i: (i, 0)),
                  pl.BlockSpec((1, 2, HEAD_DIM), lambda j, i: (jnp.where(j < n_q_tiles, 0, 1), 0, 0))],
        out_specs=pl.BlockSpec((tm, tn), lambda j, i: (i, j)),
        out_shape=jax.ShapeDtypeStruct((m, n), BF16),
        scratch_shapes=[pltpu.VMEM((k, tn), BF16)],
        compiler_params=_params(("arbitrary", "arbitrary")),
        name="qkv_gqa",
    )(a, w, cos, sin, gain_post)


def _softmax_pv(s_parts, v_parts):
    m = s_parts[0].max(axis=-1, keepdims=True)
    for s in s_parts[1:]:
        m = jnp.maximum(m, s.max(axis=-1, keepdims=True))
    den = None
    acc = None
    for s, v in zip(s_parts, v_parts):
        p = jnp.exp(s - m)
        l = p.sum(axis=-1, keepdims=True)
        o = jnp.dot(p.astype(BF16), v, preferred_element_type=F32)
        den = l if den is None else den + l
        acc = o if acc is None else acc + o
    return acc / den


def _qkt(q, k):
    return lax.dot_general(q, k, (((1,), (1,)), ((), ())), preferred_element_type=F32)


def _na_plan(rows):
    kh = min(NA_WIN_H, rows)
    uw = NA_RB + kh - 1
    uw += uw % 2
    assert rows >= uw and rows % NA_RB == 0
    sigs, cls_of_rb = [], []
    for rb in range(rows // NA_RB):
        rs_blk = min(max(rb * NA_RB - kh // 2, 0), rows - uw)
        offs = []
        for r in range(rb * NA_RB, (rb + 1) * NA_RB):
            rs = min(max(r - kh // 2, 0), rows - kh)
            assert rs_blk <= rs and rs + kh <= rs_blk + uw
            offs.append(rs - r)
        sig = (rs_blk - rb * NA_RB, tuple(offs))
        if sig not in sigs:
            sigs.append(sig)
        cls_of_rb.append(sigs.index(sig))
    return kh, uw, sigs, cls_of_rb


def _na_body(n_row_blocks, rows, q_ref, k_ref, v_ref, kc_ref, vc_ref, b_ref, o_ref):
    rb = pl.program_id(2)
    kh, uw, _, cls_of_rb = _na_plan(rows)
    span = uw * GRID_W

    @pl.when(rb < n_row_blocks)
    def _():
        rs = jnp.clip(rb * NA_RB - kh // 2, 0, rows - uw)
        k0 = pl.multiple_of(rs * GRID_W, GRID_W)
        cls = jnp.int32(0)
        for i, c in enumerate(cls_of_rb):
            cls = jnp.where(rb == i, c, cls)
        for g in range(NA_HG):
            cols = slice(g * HEAD_DIM, (g + 1) * HEAD_DIM)
            q = q_ref[:, cols]
            s_loc = _qkt(q, k_ref[pl.ds(k0, span), cols]) + b_ref[cls, g]
            s_ctx = _qkt(q, kc_ref[:, cols])
            o = _softmax_pv([s_loc, s_ctx], [v_ref[pl.ds(k0, span), cols], vc_ref[:, cols]])
            o_ref[:, cols] = o.astype(o_ref.dtype)

    @pl.when(rb >= n_row_blocks)
    def _():
        for g in range(NA_HG):
            cols = slice(g * HEAD_DIM, (g + 1) * HEAD_DIM)
            s = _qkt(q_ref[:, cols], kc_ref[:, cols])
            o_ref[:, cols] = _softmax_pv([s], [vc_ref[:, cols]]).astype(o_ref.dtype)


def _na_bias_body(rows, band_ref, o_ref):
    kh, uw, sigs, _ = _na_plan(rows)
    neg = jnp.full((GRID_W, GRID_W), NEG_INF, F32)
    for c, (off, band_offs) in enumerate(sigs):
        @pl.when(pl.program_id(0) == c)
        def _():
            for i in range(NA_RB):
                for j in range(uw):
                    inside = 0 <= off + j - i - band_offs[i] < kh
                    tile = band_ref[0, off + j - i + NA_WIN_H - 1] if inside else neg
                    o_ref[0, 0, i * GRID_W:(i + 1) * GRID_W, j * GRID_W:(j + 1) * GRID_W] = tile


def _na_bias_table(rpb, rows):
    kh, uw, sigs, _ = _na_plan(rows)
    heads = rpb.shape[0]
    qcol = jnp.arange(GRID_W)
    col_start = jnp.clip(qcol - NA_WIN_W // 2, 0, GRID_W - NA_WIN_W)
    col_valid = (qcol[None, :] >= col_start[:, None]) & (qcol[None, :] < col_start[:, None] + NA_WIN_W)
    dc_idx = jnp.clip(qcol[None, :] - qcol[:, None], -(NA_WIN_W - 1), NA_WIN_W - 1) + NA_WIN_W - 1
    onehot = (dc_idx[None] == jnp.arange(2 * NA_WIN_W - 1)[:, None, None]).astype(F32)
    band = jnp.einsum('hrc,cqk->hrqk', rpb.astype(F32), onehot, precision=lax.Precision.HIGHEST)
    band = jnp.where(col_valid[None, None], band, NEG_INF)
    n_dr = band.shape[1]
    return pl.pallas_call(
        functools.partial(_na_bias_body, rows),
        grid=(len(sigs), heads),
        in_specs=[pl.BlockSpec((1, n_dr, GRID_W, GRID_W), lambda c, h: (h, 0, 0, 0))],
        out_specs=pl.BlockSpec((1, 1, NA_RB * GRID_W, uw * GRID_W), lambda c, h: (c, h, 0, 0)),
        out_shape=jax.ShapeDtypeStruct((len(sigs), heads, NA_RB * GRID_W, uw * GRID_W), F32),
        compiler_params=_params(("arbitrary", "arbitrary")),
        name="na_bias",
    )(band)


def _na_attention(qkv, bias, n_batch, seq, ctx_len):
    n, d3 = qkv.shape
    d = d3 // 3
    heads = d // HEAD_DIM
    rows = seq // GRID_W
    qb = NA_RB * GRID_W
    assert seq % qb == 0 and ctx_len % qb == 0 and heads % NA_HG == 0
    n_row_blocks = seq // qb
    n_ctx_blocks = ctx_len // qb
    hgs = heads // NA_HG
    w = NA_HG * HEAD_DIM
    lat_blocks = n_batch * n_row_blocks

    def q_map(hg, b, rb):
        blk = jnp.where(rb < n_row_blocks, b * n_row_blocks + rb, lat_blocks + b * n_ctx_blocks + rb - n_row_blocks)
        return (blk, hg)

    ctx_blk0 = (n_batch * seq) // ctx_len
    n_cls, _, bq, bk = bias.shape
    return pl.pallas_call(
        functools.partial(_na_body, n_row_blocks, rows),
        grid=(hgs, n_batch, n_row_blocks + n_ctx_blocks),
        in_specs=[pl.BlockSpec((qb, w), q_map),
                  pl.BlockSpec((seq, w), lambda hg, b, rb: (b, hgs + hg)),
                  pl.BlockSpec((seq, w), lambda hg, b, rb: (b, 2 * hgs + hg)),
                  pl.BlockSpec((ctx_len, w), lambda hg, b, rb: (ctx_blk0 + b, hgs + hg)),
                  pl.BlockSpec((ctx_len, w), lambda hg, b, rb: (ctx_blk0 + b, 2 * hgs + hg)),
                  pl.BlockSpec((n_cls, NA_HG, bq, bk), lambda hg, b, rb: (0, hg, 0, 0),
                               pipeline_mode=pl.Buffered(1))],
        out_specs=pl.BlockSpec((qb, w), q_map),
        out_shape=jax.ShapeDtypeStruct((n, d), BF16),
        compiler_params=_params(("arbitrary", "arbitrary", "arbitrary")),
        name="na_attention",
    )(qkv, qkv, qkv, qkv, qkv, bias)


def _rope_tables(seq, n_rows):
    t = jnp.arange(seq, dtype=jnp.int32)
    row = (t // GRID_W).astype(F32)
    col = (t % GRID_W).astype(F32)
    axis_dim = HEAD_DIM // 2
    inv_freq = 1.0 / (ROPE_THETA ** (jnp.arange(0, axis_dim, 2, dtype=F32) / axis_dim))
    ang = jnp.concatenate([row[:, None] * inv_freq, col[:, None] * inv_freq], axis=-1)
    cos = jnp.repeat(jnp.cos(ang), 2, axis=-1)
    sin = jnp.repeat(jnp.sin(ang), 2, axis=-1) * jnp.tile(jnp.array([-1.0, 1.0], F32), HEAD_DIM // 2)
    reps = n_rows // seq
    return jnp.tile(cos, (reps, 1)), jnp.tile(sin, (reps, 1))


def _gqa_body(q_ref, k_ref, kc_ref, v_ref, vc_ref, o_ref, ka_ref, va_ref):
    seq = k_ref.shape[0]

    @pl.when(pl.program_id(2) == 0)
    def _():
        ka_ref[0:seq, :] = k_ref[...]
        ka_ref[seq:, :] = kc_ref[...]
        va_ref[0:seq, 0:HEAD_DIM] = v_ref[...]
        va_ref[seq:, 0:HEAD_DIM] = vc_ref[...]
        va_ref[:, HEAD_DIM:] = jnp.ones((va_ref.shape[0], va_ref.shape[1] - HEAD_DIM), BF16)

    for r in range(GQA_REP):
        cols = slice(r * HEAD_DIM, (r + 1) * HEAD_DIM)
        for r0 in range(0, q_ref.shape[0], GQA_BAND):
            rows = slice(r0, r0 + GQA_BAND)
            s = _qkt(q_ref[rows, cols], ka_ref[...])
            p = jnp.exp(s - s.max(axis=-1, keepdims=True)).astype(BF16)
            ox = jnp.dot(p, va_ref[...], preferred_element_type=F32)
            o_ref[rows, cols] = (ox[:, 0:HEAD_DIM] / ox[:, HEAD_DIM:HEAD_DIM + 1]).astype(o_ref.dtype)


def _gqa_attention(qk, qkv, n_batch, seq, ctx_len, q_cols):
    kv_heads = q_cols // HEAD_DIM // GQA_REP
    assert seq % GQA_QB == 0 and (n_batch * seq) % ctx_len == 0
    w = GQA_REP * HEAD_DIM
    qblocks = seq // GQA_QB
    k_blk0 = q_cols // HEAD_DIM
    v_blk0 = k_blk0 + kv_heads
    ctx_blk0 = (n_batch * seq) // ctx_len
    return pl.pallas_call(
        _gqa_body,
        grid=(n_batch, kv_heads, qblocks),
        in_specs=[pl.BlockSpec((GQA_QB, w), lambda b, h, i: (b * qblocks + i, h)),
                  pl.BlockSpec((seq, HEAD_DIM), lambda b, h, i: (b, k_blk0 + h)),
                  pl.BlockSpec((ctx_len, HEAD_DIM), lambda b, h, i: (ctx_blk0 + b, k_blk0 + h)),
                  pl.BlockSpec((seq, HEAD_DIM), lambda b, h, i: (b, v_blk0 + h)),
                  pl.BlockSpec((ctx_len, HEAD_DIM), lambda b, h, i: (ctx_blk0 + b, v_blk0 + h))],
        out_specs=pl.BlockSpec((GQA_QB, w), lambda b, h, i: (b * qblocks + i, h)),
        out_shape=jax.ShapeDtypeStruct((n_batch * seq, q_cols), BF16),
        scratch_shapes=[pltpu.VMEM((seq + ctx_len, HEAD_DIM), BF16),
                        pltpu.VMEM((seq + ctx_len, 2 * HEAD_DIM), BF16)],
        compiler_params=_params(("arbitrary", "arbitrary", "arbitrary")),
        name="gqa_attention",
    )(qk, qk, qk, qkv, qkv)


def _router_body(n_experts, lg_ref, b_ref, id_ref, wt_ref):
    lt = lg_ref[...].T[:n_experts] + b_ref[...]
    rows = [lt[e:e + 1] for e in range(n_experts)]
    mx = functools.reduce(jnp.maximum, rows)
    ex = [jnp.exp(r - mx) for r in rows]
    den = functools.reduce(jnp.add, ex)
    probs = [e / den for e in ex]

    epg = n_experts // N_GROUPS
    best = None
    for g in range(N_GROUPS):
        p = probs[g * epg:(g + 1) * epg]
        top1 = functools.reduce(jnp.maximum, p)
        i1 = jnp.full(top1.shape, epg, jnp.int32)
        for j in reversed(range(epg)):
            i1 = jnp.where(p[j] == top1, j, i1)
        rest = [jnp.where(i1 == j, -1.0, p[j]) for j in range(epg)]
        top2 = functools.reduce(jnp.maximum, rest)
        i2 = jnp.full(top1.shape, epg, jnp.int32)
        for j in reversed(range(epg)):
            i2 = jnp.where((rest[j] == top2) & (i1 != j), j, i2)
        cand = (top1 + top2, top1, top2, i1 + g * epg, i2 + g * epg)
        if best is None:
            best = cand
        else:
            take = cand[0] > best[0]
            best = tuple(jnp.where(take, c, b) for c, b in zip(cand, best))
    score, top1, top2, e1, e2 = best
    id_ref[...] = jnp.zeros(id_ref.shape, jnp.int32)
    wt_ref[...] = jnp.zeros(wt_ref.shape, F32)
    id_ref[0:1, :] = e1
    id_ref[1:2, :] = e2
    wt_ref[0:1, :] = top1 / score
    wt_ref[1:2, :] = top2 / score


def _router(logits, router_b):
    n = logits.shape[0]
    e = router_b.shape[0]
    tm = 512
    return pl.pallas_call(
        functools.partial(_router_body, e),
        grid=(n // tm,),
        in_specs=[pl.BlockSpec((tm, LANES), lambda i: (i, 0)),
                  pl.BlockSpec((e, 1), lambda i: (0, 0))],
        out_specs=[pl.BlockSpec((8, tm), lambda i: (0, i)), pl.BlockSpec((8, tm), lambda i: (0, i))],
        out_shape=[jax.ShapeDtypeStruct((8, n), jnp.int32), jax.ShapeDtypeStruct((8, n), F32)],
        compiler_params=_params(("arbitrary",)),
        name="router",
    )(logits, router_b.reshape(e, 1).astype(F32))


def _dispatch_plan(ids, n_experts, n_tiles):
    e0, e1 = ids[0], ids[1]
    n = e0.shape[0]
    ar = jnp.arange(n_experts, dtype=jnp.int32)[:, None]
    oh0 = (e0[None, :] == ar).astype(jnp.int32)
    oh1 = (e1[None, :] == ar).astype(jnp.int32)
    sel = oh0 + oh1
    csum = jnp.cumsum(sel, axis=1)
    counts = csum[:, -1]
    padded = ((counts + MOE_TM - 1) // MOE_TM) * MOE_TM
    ends = jnp.cumsum(padded)
    offs = ends - padded
    slot = offs[:, None] + csum - 1
    pos0 = jnp.sum(oh0 * slot, axis=0)
    pos1 = jnp.sum(oh1 * slot, axis=0)
    tok = jnp.arange(n, dtype=jnp.int32)
    src = jnp.zeros((n_tiles * MOE_TM,), jnp.int32)
    src = src.at[jnp.concatenate([pos0, pos1])].set(jnp.concatenate([tok, tok]), unique_indices=True)
    n_used = (ends[-1] // MOE_TM).astype(jnp.int32)
    tile_start = jnp.minimum(jnp.arange(n_tiles, dtype=jnp.int32), n_used - 1) * MOE_TM
    nonempty = (counts > 0).astype(jnp.int32)
    tile_seg = jnp.sum((ends[None, :] <= tile_start[:, None]).astype(jnp.int32) * nonempty[None, :], axis=1)
    seg_of_expert = jnp.cumsum(nonempty) - 1
    seg_expert = jnp.sum(jnp.where((seg_of_expert[None, :] == ar) & (nonempty[None, :] > 0), ar.T, 0), axis=1)
    meta = jnp.stack([n_used, jnp.sum(nonempty)]).astype(jnp.int32)
    return src, jnp.stack([pos0, pos1]), (tile_seg.astype(jnp.int32), seg_expert.astype(jnp.int32), meta)


def _row_copy(src_hbm, row, dst, k, sem):
    return pltpu.make_async_copy(src_hbm.at[pl.ds(row, 1), :], dst.at[pl.ds(k, 1), :], sem)


def _issue_token_gather(idx_ref, first, h_hbm, tile, sem):
    def one(k, carry):
        pltpu.make_async_copy(h_hbm.at[idx_ref[first + k]], tile.at[k], sem).start()
        return carry
    lax.fori_loop(0, tile.shape[0], one, 0, unroll=8)


def _wait_token_gather(h_hbm, tile, sem):
    def one(k, carry):
        pltpu.make_async_copy(h_hbm.at[0], tile.at[k], sem).wait()
        return carry
    lax.fori_loop(0, tile.shape[0], one, 0, unroll=8)


def _stream_expert_weights(seg_ref, sege_ref, meta_ref, w_hbms, chunk, wf, wb, sem):
    j, t = pl.program_id(0), pl.program_id(1)
    n_j = pl.num_programs(0)
    n_seg = meta_ref[1]
    k = seg_ref[t]
    first = (t == 0) | (k != seg_ref[jnp.maximum(t - 1, 0)])

    def copies(jj, kk):
        e = sege_ref[kk]
        return [pltpu.make_async_copy(w.at[e, :, pl.ds(jj * chunk, chunk)], wf.at[m], sem)
                for m, w in enumerate(w_hbms)]

    @pl.when(first)
    def _():
        @pl.when((j == 0) & (t == 0))
        def _():
            for c in copies(j, k):
                c.start(priority=1)

        for c in copies(j, k):
            c.wait()
        for m in range(len(w_hbms)):
            _convert_rows(wb.at[m], wf.at[m])

        wrap = k + 1 == n_seg
        nj = jnp.where(wrap, j + 1, j)
        nk = jnp.where(wrap, 0, k + 1)

        @pl.when(nj < n_j)
        def _():
            for c in copies(nj, nk):
                c.start(priority=1)


def _moe_up_body(seg_ref, sege_ref, meta_ref, src_ref, h_hbm, wg_hbm, wu_hbm, o_ref, xbuf, gsem, wf, wb, wsem):
    j, t = pl.program_id(0), pl.program_id(1)
    n_j, n_t = pl.num_programs(0), pl.num_programs(1)
    n_used = meta_ref[0]
    slot = jnp.bitwise_and(j * n_t + t, 1)

    @pl.when((j == 0) & (t == 0))
    def _():
        _issue_token_gather(src_ref, 0, h_hbm, xbuf.at[0], gsem.at[0])

    nt = jnp.where(t + 1 < n_t, t + 1, 0)

    @pl.when(((t + 1 < n_t) | (j + 1 < n_j)) & (nt < n_used))
    def _():
        _issue_token_gather(src_ref, nt * MOE_TM, h_hbm, xbuf.at[1 - slot], gsem.at[1 - slot])

    _stream_expert_weights(seg_ref, sege_ref, meta_ref, (wg_hbm, wu_hbm), o_ref.shape[1], wf, wb, wsem)

    @pl.when(t < n_used)
    def _():
        _wait_token_gather(h_hbm, xbuf.at[slot], gsem.at[slot])
        x = xbuf[slot].reshape(MOE_TM, xbuf.shape[2] * LANES).astype(BF16)
        g = jnp.dot(x, wb[0], preferred_element_type=F32)
        u = jnp.dot(x, wb[1], preferred_element_type=F32)
        o_ref[...] = (g * jax.nn.sigmoid(g) * u).astype(o_ref.dtype)

    @pl.when(t >= n_used)
    def _():
        o_ref[...] = jnp.zeros(o_ref.shape, o_ref.dtype)


def _moe_up(h_rows, src, w_gate, w_up, plan):
    _, pieces, _ = h_rows.shape
    d = pieces * LANES
    r = src.shape[0]
    f = w_gate.shape[2]
    n_tiles = r // MOE_TM
    fc = MOE_FC if f % MOE_FC == 0 else f
    any_spec = pl.BlockSpec(memory_space=pl.ANY)
    return pl.pallas_call(
        _moe_up_body,
        grid_spec=pltpu.PrefetchScalarGridSpec(
            num_scalar_prefetch=4,
            grid=(f // fc, n_tiles),
            in_specs=[any_spec, any_spec, any_spec],
            out_specs=pl.BlockSpec((MOE_TM, fc), lambda j, t, sg, se, mt, sr: (t, j)),
            scratch_shapes=[pltpu.VMEM((2, MOE_TM, pieces, LANES), F32), pltpu.SemaphoreType.DMA((2,)),
                            pltpu.VMEM((2, d, fc), F32), pltpu.VMEM((2, d, fc), BF16),
                            pltpu.SemaphoreType.DMA(())]),
        out_shape=jax.ShapeDtypeStruct((r, f), BF16),
        compiler_params=_params(("arbitrary", "arbitrary")),
        name="moe_up",
    )(*plan, src, h_rows, w_gate, w_up)


def _moe_down_body(seg_ref, sege_ref, meta_ref, h_ref, w_hbm, o_ref, wf, wb, sem):
    _stream_expert_weights(seg_ref, sege_ref, meta_ref, (w_hbm,), o_ref.shape[1], wf, wb, sem)
    t = pl.program_id(1)

    @pl.when(t < meta_ref[0])
    def _():
        o_ref[...] = jnp.dot(h_ref[...], wb[0], preferred_element_type=F32)

    @pl.when(t >= meta_ref[0])
    def _():
        o_ref[...] = jnp.zeros(o_ref.shape, o_ref.dtype)


def _moe_down(hs, w_down, plan):
    r, f = hs.shape
    d = w_down.shape[2]
    nc = min(MOE_NC, d)
    n_tiles = r // MOE_TM
    return pl.pallas_call(
        _moe_down_body,
        grid_spec=pltpu.PrefetchScalarGridSpec(
            num_scalar_prefetch=3,
            grid=(d // nc, n_tiles),
            in_specs=[pl.BlockSpec((MOE_TM, f), lambda j, t, sg, se, mt: (jnp.minimum(t, mt[0] - 1), 0)),
                      pl.BlockSpec(memory_space=pl.ANY)],
            out_specs=pl.BlockSpec((MOE_TM, nc), lambda j, t, sg, se, mt: (t, j)),
            scratch_shapes=[pltpu.VMEM((1, f, nc), F32), pltpu.VMEM((1, f, nc), BF16),
                            pltpu.SemaphoreType.DMA(())]),
        out_shape=jax.ShapeDtypeStruct((r, d), F32),
        compiler_params=_params(("arbitrary", "arbitrary")),
        name="moe_down",
    )(*plan, hs, w_down)


def _combine_body(has_next, pos_ref, y_hbm, wt_ref, x_ref, g_ref, gate_ref, *rest):
    if has_next:
        g2_ref, sh_ref, sc_ref, xo_ref, ho_ref, buf, sem = rest
    else:
        xo_ref, buf, sem = rest
    i = pl.program_id(0)
    n = pl.num_programs(0)
    n_tok = n * ROW_TILE

    def issue(step, slot):
        def one(k, carry):
            tok = step * ROW_TILE + k
            _row_copy(y_hbm, pos_ref[tok], buf.at[slot, 0], k, sem.at[slot]).start()
            _row_copy(y_hbm, pos_ref[n_tok + tok], buf.at[slot, 1], k, sem.at[slot]).start()
            return carry
        lax.fori_loop(0, ROW_TILE, one, 0, unroll=8)

    @pl.when(i == 0)
    def _():
        issue(0, 0)

    @pl.when(i + 1 < n)
    def _():
        issue(i + 1, (i + 1) % 2)

    slot = i % 2

    def wait_one(k, carry):
        _row_copy(y_hbm, 0, buf.at[slot, 0], k, sem.at[slot]).wait()
        _row_copy(y_hbm, 0, buf.at[slot, 1], k, sem.at[slot]).wait()
        return carry
    lax.fori_loop(0, ROW_TILE, wait_one, 0, unroll=8)

    wt = wt_ref[...]
    y = wt[:, 0:1] * buf[slot, 0] + wt[:, 1:2] * buf[slot, 1]
    xn = x_ref[...] + gate_ref[0] * _rms(y, g_ref[...])
    xo_ref[...] = xn
    if has_next:
        h = _rms(xn, g2_ref[...])
        ho_ref[...] = (h * (1.0 + sc_ref[0]) + sh_ref[0]).astype(ho_ref.dtype)


def _combine(ys, pos, wts, x, gain, mod, gate_chunk, n_lat, seq, nxt=None):
    n = wts.shape[0]
    d = ys.shape[1]
    n_batch = mod.shape[0] - 1
    n_lat_tiles = min(n_lat, n) // ROW_TILE

    def mk(chunk):
        def index_map(i, pos_ref):
            return (jnp.where(i < n_lat_tiles, i // (seq // ROW_TILE), n_batch), 0, chunk)
        return index_map

    row = pl.BlockSpec((ROW_TILE, d), lambda i, p: (i, 0))
    vec = pl.BlockSpec((1, d), lambda i, p: (0, 0))
    in_specs = [pl.BlockSpec(memory_space=pl.ANY),
                pl.BlockSpec((ROW_TILE, 2), lambda i, p: (i, 0)),
                row, vec, pl.BlockSpec((1, 1, d), mk(gate_chunk))]
    args = [ys, wts, x, gain.reshape(1, d), mod]
    out_specs = [row]
    out_shape = [jax.ShapeDtypeStruct((n, d), F32)]
    if nxt is not None:
        gain2, mod2, shift_chunk, scale_chunk, h_dtype = nxt
        in_specs += [vec, pl.BlockSpec((1, 1, d), mk(shift_chunk)), pl.BlockSpec((1, 1, d), mk(scale_chunk))]
        args += [gain2.reshape(1, d), mod2, mod2]
        out_specs.append(row)
        out_shape.append(jax.ShapeDtypeStruct((n, d), h_dtype))
    out = pl.pallas_call(
        functools.partial(_combine_body, nxt is not None),
        grid_spec=pltpu.PrefetchScalarGridSpec(
            num_scalar_prefetch=1,
            grid=(n // ROW_TILE,),
            in_specs=in_specs,
            out_specs=out_specs,
            scratch_shapes=[pltpu.VMEM((2, 2, ROW_TILE, d), F32), pltpu.SemaphoreType.DMA((2,))]),
        out_shape=out_shape,
        compiler_params=_params(("arbitrary",)),
        name="moe_combine",
    )(pos.reshape(-1), *args)
    return out if nxt is not None else out[0]


def _moe(h_rows, logits, router_b, w_gate, w_up, w_down):
    n = logits.shape[0]
    n_experts = router_b.shape[0]
    n_tiles = (TOP_K * n) // MOE_TM + n_experts
    ids, wts = _router(logits, router_b)
    src, pos, plan = _dispatch_plan(ids, n_experts, n_tiles)
    hs = _moe_up(h_rows, src, w_gate, w_up, plan)
    ys = _moe_down(hs, w_down, plan)
    return ys, pos, wts[:2].T


def kernel(x, c, ctx, c_ctx, router_w, router_b, l0_ada_w, l0_ada_b, l0_norm_pre_mix, l0_norm_post_mix, l0_norm_pre_ffn, l0_norm_post_ffn, l0_na_w_qkv, l0_na_rpb, l0_na_w_o, l0_moe_w_gate, l0_moe_w_up, l0_moe_w_down, l1_ada_w, l1_ada_b, l1_norm_pre_mix, l1_norm_post_mix, l1_norm_pre_ffn, l1_norm_post_ffn, l1_gqa_w_qkv, l1_gqa_q_gain, l1_gqa_k_gain, l1_gqa_w_o, l1_moe_w_gate, l1_moe_w_up, l1_moe_w_down):
    n_batch, seq, d = x.shape
    ctx_len = ctx.shape[1]
    n_lat = n_batch * seq
    n_all = n_lat + n_batch * ctx_len
    scale = HEAD_DIM ** -0.5
    SH_M, SC_M, G_M, SH_F, SC_F, G_F = range(6)

    cvec = jnp.zeros((8, d), F32).at[:n_batch].set(c).at[n_batch].set(c_ctx)
    mod0 = _adaln(cvec, l0_ada_w, l0_ada_b)[:n_batch + 1].reshape(n_batch + 1, 1, 6 * d)
    mod1 = _adaln(cvec, l1_ada_w, l1_ada_b)[:n_batch + 1].reshape(n_batch + 1, 1, 6 * d)

    x_lat = x.reshape(n_lat, d)
    x_ctx = ctx.reshape(n_batch * ctx_len, d)

    h = _norm_mod(x_lat, x_ctx, l0_norm_pre_mix, mod0, SH_M, SC_M, seq, BF16)
    qscale = jnp.concatenate([jnp.full((1, d), scale, F32), jnp.ones((1, 2 * d), F32)], axis=1)
    qkv = _matmul(h, l0_na_w_qkv, qscale, BF16)
    bias = _na_bias_table(l0_na_rpb, seq // GRID_W)
    att = _na_attention(qkv, bias, n_batch, seq, ctx_len)
    y = _matmul(att, l0_na_w_o, jnp.ones((1, d), F32), F32)
    xa, h_rows, logits = _residual(x_lat, x_ctx, y, l0_norm_post_mix, mod0, G_M, seq,
                                   l0_norm_pre_ffn, SH_F, SC_F, router_w)
    ys, pos, wts = _moe(h_rows, logits, router_b, l0_moe_w_gate, l0_moe_w_up, l0_moe_w_down)
    xa, h = _combine(ys, pos, wts, xa, l0_norm_post_ffn, mod0, G_F, n_lat, seq,
                     nxt=(l1_norm_pre_mix, mod1, SH_M, SC_M, BF16))

    q_cols = d
    kv_cols = d // GQA_REP
    cos, sin = _rope_tables(seq, n_lat)
    ident = (jnp.ones((n_all - n_lat, HEAD_DIM), F32), jnp.zeros((n_all - n_lat, HEAD_DIM), F32))
    cos = jnp.concatenate([cos, ident[0]], axis=0)
    sin = jnp.concatenate([sin, ident[1]], axis=0)
    gain_post = jnp.stack([jnp.stack([l1_gqa_q_gain.astype(F32), jnp.full((HEAD_DIM,), scale, F32)]),
                           jnp.stack([l1_gqa_k_gain.astype(F32), jnp.ones((HEAD_DIM,), F32)])])
    qkv = _qkv_gqa(h, l1_gqa_w_qkv, cos, sin, gain_post, q_cols, kv_cols)
    att = _gqa_attention(qkv, qkv, n_batch, seq, ctx_len, q_cols)
    y = _matmul(att, l1_gqa_w_o, jnp.ones((1, d), F32), F32)
    xl, h_rows, logits = _residual(xa, xa, y, l1_norm_post_mix, mod1, G_M, seq,
                                   l1_norm_pre_ffn, SH_F, SC_F, router_w)
    ys, pos, wts = _moe(h_rows, logits, router_b, l1_moe_w_gate, l1_moe_w_up, l1_moe_w_down)
    xl = _combine(ys, pos, wts, xl, l1_norm_post_ffn, mod1, G_F, n_lat, seq)
    return xl.reshape(n_batch, seq, d)
```

```python
import functools

import jax
import jax.numpy as jnp
from jax import lax
from jax.experimental import pallas as pl
from jax.experimental.pallas import tpu as pltpu

LANES = 128
SUBLANES = 8
BF16_SUBLANES = 16
GRID_W = 64
HEAD_DIM = 128
NA_WIN_H = 8
NA_WIN_W = 16
GQA_REP = 4
ROPE_THETA = 10000.0
N_GROUPS = 4
TOP_K = 2
EPS = 1e-6
NEG_INF = -1e30

F32 = jnp.float32
BF16 = jnp.bfloat16

ROW_TILE = 256
MM_TM = 1024
MM_TN = 512
MOE_TM = 256
MOE_FC = 768
MOE_NC = 4096
NA_RB = 4
NA_HG = 8
GQA_QB = 512
GQA_BAND = 128
ADA_TN = 512
VMEM_LIMIT = 56 * 1024 * 1024


def _params(sem):
    return pltpu.CompilerParams(dimension_semantics=sem, vmem_limit_bytes=VMEM_LIMIT)


def _adaln_body(c_ref, w_ref, b_ref, o_ref):
    c = c_ref[...]
    s = (c * jax.nn.sigmoid(c)).astype(BF16)
    o_ref[...] = jnp.dot(s, w_ref[...].astype(BF16), preferred_element_type=F32) + b_ref[...]


def _adaln(cvec, ada_w, ada_b):
    m, d = cvec.shape
    n = ada_w.shape[1]
    return pl.pallas_call(
        _adaln_body,
        grid=(n // ADA_TN,),
        in_specs=[pl.BlockSpec((m, d), lambda j: (0, 0)),
                  pl.BlockSpec((d, ADA_TN), lambda j: (0, j)),
                  pl.BlockSpec((1, ADA_TN), lambda j: (0, j))],
        out_specs=pl.BlockSpec((m, ADA_TN), lambda j: (0, j)),
        out_shape=jax.ShapeDtypeStruct((m, n), F32),
        compiler_params=_params(("arbitrary",)),
        name="adaln",
    )(cvec, ada_w, ada_b.reshape(1, n))


def _convert_rows(dst_ref, src_ref):
    rows = src_ref.shape[0]
    band = next(b for b in (256, 128, 64, 32, 16) if rows % b == 0)

    def one(i, carry):
        r0 = pl.multiple_of(i * band, band)
        dst_ref[pl.ds(r0, band), :] = src_ref[pl.ds(r0, band), :].astype(dst_ref.dtype)
        return carry
    lax.fori_loop(0, rows // band, one, 0)


def _rms(x, gain):
    return x * lax.rsqrt(jnp.mean(x * x, axis=-1, keepdims=True) + EPS) * gain


def _stream_rows(n_lat_tiles, xl_ref, xc_ref):
    return jnp.where(pl.program_id(0) < n_lat_tiles, xl_ref[...], xc_ref[...])


def _stream_specs(n_lat_tiles, d):
    return [pl.BlockSpec((ROW_TILE, d), lambda i: (jnp.minimum(i, n_lat_tiles - 1), 0)),
            pl.BlockSpec((ROW_TILE, d), lambda i: (jnp.maximum(i - n_lat_tiles, 0), 0))]


def _norm_mod_body(n_lat_tiles, xl_ref, xc_ref, g_ref, sh_ref, sc_ref, o_ref):
    y = _rms(_stream_rows(n_lat_tiles, xl_ref, xc_ref), g_ref[...])
    o_ref[...] = (y * (1.0 + sc_ref[0]) + sh_ref[0]).astype(o_ref.dtype)


def _mod_row_map(n_lat_tiles, tiles_per_batch, n_batch, chunk):
    def index_map(i):
        return (jnp.where(i < n_lat_tiles, i // tiles_per_batch, n_batch), 0, chunk)
    return index_map


def _norm_mod(x_lat, x_ctx, gain, mod, shift_chunk, scale_chunk, seq, out_dtype):
    n_lat, d = x_lat.shape
    n = n_lat + x_ctx.shape[0]
    n_batch = mod.shape[0] - 1
    n_lat_tiles = n_lat // ROW_TILE
    mk = functools.partial(_mod_row_map, n_lat_tiles, seq // ROW_TILE, n_batch)
    return pl.pallas_call(
        functools.partial(_norm_mod_body, n_lat_tiles),
        grid=(n // ROW_TILE,),
        in_specs=_stream_specs(n_lat_tiles, d) + [
            pl.BlockSpec((1, d), lambda i: (0, 0)),
            pl.BlockSpec((1, 1, d), mk(shift_chunk)),
            pl.BlockSpec((1, 1, d), mk(scale_chunk))],
        out_specs=pl.BlockSpec((ROW_TILE, d), lambda i: (i, 0)),
        out_shape=jax.ShapeDtypeStruct((n, d), out_dtype),
        compiler_params=_params(("arbitrary",)),
        name="norm_mod",
    )(x_lat, x_ctx, gain.reshape(1, d), mod, mod)


def _split_bf16(x):
    hi = x.astype(BF16)
    lo = (x - hi.astype(F32)).astype(BF16)
    return hi, lo


def _residual_body(n_lat_tiles, xl_ref, xc_ref, y_ref, g_ref, gate_ref, g2_ref, sh_ref, sc_ref, rw_ref,
                   xo_ref, hr_ref, lg_ref):
    xn = _stream_rows(n_lat_tiles, xl_ref, xc_ref) + gate_ref[0] * _rms(y_ref[...], g_ref[...])
    xo_ref[...] = xn
    h = _rms(xn, g2_ref[...]) * (1.0 + sc_ref[0]) + sh_ref[0]
    hr_ref[...] = h.reshape(hr_ref.shape)
    h_hi, h_lo = _split_bf16(h)
    w_hi, w_lo = _split_bf16(rw_ref[...])
    lg_ref[...] = (jnp.dot(h_hi, w_hi, preferred_element_type=F32)
                   + jnp.dot(h_lo, w_hi, preferred_element_type=F32)
                   + jnp.dot(h_hi, w_lo, preferred_element_type=F32))


def _residual(x_lat, x_ctx, y, gain, mod, gate_chunk, seq, gain2, shift_chunk, scale_chunk, router_w):
    n, d = y.shape
    n_batch = mod.shape[0] - 1
    pieces = d // LANES
    n_lat_tiles = min(x_lat.shape[0], n) // ROW_TILE
    mk = functools.partial(_mod_row_map, n_lat_tiles, seq // ROW_TILE, n_batch)
    row = pl.BlockSpec((ROW_TILE, d), lambda i: (i, 0))
    vec = pl.BlockSpec((1, d), lambda i: (0, 0))
    e = router_w.shape[1]
    w_pad = jnp.pad(router_w, ((0, 0), (0, LANES - e)))
    return pl.pallas_call(
        functools.partial(_residual_body, n_lat_tiles),
        grid=(n // ROW_TILE,),
        in_specs=_stream_specs(n_lat_tiles, d) + [
            row, vec, pl.BlockSpec((1, 1, d), mk(gate_chunk)), vec,
            pl.BlockSpec((1, 1, d), mk(shift_chunk)), pl.BlockSpec((1, 1, d), mk(scale_chunk)),
            pl.BlockSpec((d, LANES), lambda i: (0, 0))],
        out_specs=[row, pl.BlockSpec((ROW_TILE, pieces, LANES), lambda i: (i, 0, 0)),
                   pl.BlockSpec((ROW_TILE, LANES), lambda i: (i, 0))],
        out_shape=[jax.ShapeDtypeStruct((n, d), F32), jax.ShapeDtypeStruct((n, pieces, LANES), F32),
                   jax.ShapeDtypeStruct((n, LANES), F32)],
        compiler_params=_params(("arbitrary",)),
        name="residual",
    )(x_lat, x_ctx, y, gain.reshape(1, d), mod, gain2.reshape(1, d), mod, mod, w_pad)


def _matmul_body(a_ref, w_ref, s_ref, o_ref, wb_ref):
    @pl.when(pl.program_id(1) == 0)
    def _():
        _convert_rows(wb_ref, w_ref)

    acc = jnp.dot(a_ref[...], wb_ref[...], preferred_element_type=F32)
    o_ref[...] = (acc * s_ref[...]).astype(o_ref.dtype)


def _row_tile(m):
    return next(t for t in (MM_TM, MM_TM // 2, MM_TM // 4) if m % t == 0)


def _matmul(a, w, col_scale, out_dtype):
    m, k = a.shape
    n = w.shape[1]
    tm = _row_tile(m)
    return pl.pallas_call(
        _matmul_body,
        grid=(n // MM_TN, m // tm),
        in_specs=[pl.BlockSpec((tm, k), lambda j, i: (i, 0)),
                  pl.BlockSpec((k, MM_TN), lambda j, i: (0, j)),
                  pl.BlockSpec((1, MM_TN), lambda j, i: (0, j))],
        out_specs=pl.BlockSpec((tm, MM_TN), lambda j, i: (i, j)),
        out_shape=jax.ShapeDtypeStruct((m, n), out_dtype),
        scratch_shapes=[pltpu.VMEM((k, MM_TN), BF16)],
        compiler_params=_params(("arbitrary", "arbitrary")),
        name="matmul",
    )(a, w, col_scale)


def _rope_head(y, cos, sin, even):
    partner = jnp.where(even, pltpu.roll(y, HEAD_DIM - 1, 1), pltpu.roll(y, 1, 1))
    return y * cos + partner * sin


def _qkv_gqa_body(n_qk_tiles, a_ref, w_ref, cos_ref, sin_ref, gp_ref, o_ref, wb_ref):
    j = pl.program_id(0)

    @pl.when(pl.program_id(1) == 0)
    def _():
        _convert_rows(wb_ref, w_ref)

    @pl.when(j < n_qk_tiles)
    def _():
        gain = gp_ref[0, 0:1, :]
        post = gp_ref[0, 1:2, :]
        band = min(ROW_TILE, a_ref.shape[0])
        even = (lax.broadcasted_iota(jnp.int32, (band, HEAD_DIM), 1) % 2) == 0
        for r0 in range(0, a_ref.shape[0], band):
            rows = slice(r0, r0 + band)
            acc = jnp.dot(a_ref[rows, :], wb_ref[...], preferred_element_type=F32)
            cos = cos_ref[rows, :]
            sin = sin_ref[rows, :]
            for h in range(acc.shape[1] // HEAD_DIM):
                cols = slice(h * HEAD_DIM, (h + 1) * HEAD_DIM)
                y = _rope_head(_rms(acc[:, cols], gain), cos, sin, even)
                o_ref[rows, cols] = (y * post).astype(o_ref.dtype)

    @pl.when(j >= n_qk_tiles)
    def _():
        o_ref[...] = jnp.dot(a_ref[...], wb_ref[...], preferred_element_type=F32).astype(o_ref.dtype)


def _qkv_gqa(a, w, cos, sin, gain_post, q_cols, kv_cols):
    m, k = a.shape
    n = w.shape[1]
    tm = _row_tile(m)
    tn = min(MM_TN, kv_cols)
    assert q_cols % tn == 0 and kv_cols % tn == 0
    n_q_tiles = q_cols // tn
    n_qk_tiles = (q_cols + kv_cols) // tn
    return pl.pallas_call(
        functools.partial(_qkv_gqa_body, n_qk_tiles),
        grid=(n // tn, m // tm),
        in_specs=[pl.BlockSpec((tm, k), lambda j, i: (i, 0)),
                  pl.BlockSpec((k, tn), lambda j, i: (0, j)),
                  pl.BlockSpec((tm, HEAD_DIM), lambda j, i: (i, 0)),
                  pl.BlockSpec((tm, HEAD_DIM), lambda j, i: (i, 0)),
                  pl.BlockSpec((1, 2, HEAD_DIM), lambda j, i: (jnp.where(j < n_q_tiles, 0, 1), 0, 0))],
        out_specs=pl.BlockSpec((tm, tn), lambda j, i: (i, j)),
        out_shape=jax.ShapeDtypeStruct((m, n), BF16),
        scratch_shapes=[pltpu.VMEM((k, tn), BF16)],
        compiler_params=_params(("arbitrary", "arbitrary")),
        name="qkv_gqa",
    )(a, w, cos, sin, gain_post)


def _softmax_pv(s_parts, vx_parts):
    m = s_parts[0].max(axis=-1, keepdims=True)
    for s in s_parts[1:]:
        m = jnp.maximum(m, s.max(axis=-1, keepdims=True))
    acc = None
    for s, vx in zip(s_parts, vx_parts):
        o = jnp.dot(jnp.exp(s - m).astype(BF16), vx, preferred_element_type=F32)
        acc = o if acc is None else acc + o
    return acc[:, 0:HEAD_DIM] / acc[:, HEAD_DIM:HEAD_DIM + 1]


def _qkt(q, k):
    return lax.dot_general(q, k, (((1,), (1,)), ((), ())), preferred_element_type=F32)


def _na_plan(rows):
    kh = min(NA_WIN_H, rows)
    uw = NA_RB + kh - 1
    uw += uw % 2
    assert rows >= uw and rows % NA_RB == 0
    sigs, cls_of_rb = [], []
    for rb in range(rows // NA_RB):
        rs_blk = min(max(rb * NA_RB - kh // 2, 0), rows - uw)
        offs = []
        for r in range(rb * NA_RB, (rb + 1) * NA_RB):
            rs = min(max(r - kh // 2, 0), rows - kh)
            assert rs_blk <= rs and rs + kh <= rs_blk + uw
            offs.append(rs - r)
        sig = (rs_blk - rb * NA_RB, tuple(offs))
        if sig not in sigs:
            sigs.append(sig)
        cls_of_rb.append(sigs.index(sig))
    return kh, uw, sigs, cls_of_rb


def _na_body(n_row_blocks, rows, q_ref, k_ref, v_ref, kc_ref, vc_ref, b_ref, o_ref, vx_ref):
    rb = pl.program_id(2)
    kh, uw, _, cls_of_rb = _na_plan(rows)
    span = uw * GRID_W
    seq = v_ref.shape[0]
    wide = 2 * HEAD_DIM

    @pl.when(rb == 0)
    def _():
        vx_ref[...] = jnp.ones(vx_ref.shape, vx_ref.dtype)
        for g in range(NA_HG):
            vx_ref[0:seq, g * wide:g * wide + HEAD_DIM] = v_ref[:, g * HEAD_DIM:(g + 1) * HEAD_DIM]
            vx_ref[seq:, g * wide:g * wide + HEAD_DIM] = vc_ref[:, g * HEAD_DIM:(g + 1) * HEAD_DIM]

    @pl.when(rb < n_row_blocks)
    def _():
        rs = jnp.clip(rb * NA_RB - kh // 2, 0, rows - uw)
        k0 = pl.multiple_of(rs * GRID_W, GRID_W)
        cls = jnp.int32(0)
        for i, c in enumerate(cls_of_rb):
            cls = jnp.where(rb == i, c, cls)
        for g in range(NA_HG):
            cols = slice(g * HEAD_DIM, (g + 1) * HEAD_DIM)
            xcols = slice(g * wide, (g + 1) * wide)
            q = q_ref[:, cols]
            s_loc = _qkt(q, k_ref[pl.ds(k0, span), cols]) + b_ref[cls, g]
            s_ctx = _qkt(q, kc_ref[:, cols])
            o = _softmax_pv([s_loc, s_ctx], [vx_ref[pl.ds(k0, span), xcols], vx_ref[seq:, xcols]])
            o_ref[:, cols] = o.astype(o_ref.dtype)

    @pl.when(rb >= n_row_blocks)
    def _():
        for g in range(NA_HG):
            cols = slice(g * HEAD_DIM, (g + 1) * HEAD_DIM)
            xcols = slice(g * wide, (g + 1) * wide)
            s = _qkt(q_ref[:, cols], kc_ref[:, cols])
            o_ref[:, cols] = _softmax_pv([s], [vx_ref[seq:, xcols]]).astype(o_ref.dtype)


def _na_bias_body(rows, band_ref, o_ref):
    kh, uw, sigs, _ = _na_plan(rows)
    neg = jnp.full((GRID_W, GRID_W), NEG_INF, F32)
    for c, (off, band_offs) in enumerate(sigs):
        @pl.when(pl.program_id(0) == c)
        def _():
            for i in range(NA_RB):
                for j in range(uw):
                    inside = 0 <= off + j - i - band_offs[i] < kh
                    tile = band_ref[0, off + j - i + NA_WIN_H - 1] if inside else neg
                    o_ref[0, 0, i * GRID_W:(i + 1) * GRID_W, j * GRID_W:(j + 1) * GRID_W] = tile


def _na_bias_table(rpb, rows):
    kh, uw, sigs, _ = _na_plan(rows)
    heads = rpb.shape[0]
    qcol = jnp.arange(GRID_W)
    col_start = jnp.clip(qcol - NA_WIN_W // 2, 0, GRID_W - NA_WIN_W)
    col_valid = (qcol[None, :] >= col_start[:, None]) & (qcol[None, :] < col_start[:, None] + NA_WIN_W)
    dc_idx = jnp.clip(qcol[None, :] - qcol[:, None], -(NA_WIN_W - 1), NA_WIN_W - 1) + NA_WIN_W - 1
    onehot = (dc_idx[None] == jnp.arange(2 * NA_WIN_W - 1)[:, None, None]).astype(F32)
    band = jnp.einsum('hrc,cqk->hrqk', rpb.astype(F32), onehot, precision=lax.Precision.HIGHEST)
    band = jnp.where(col_valid[None, None], band, NEG_INF)
    n_dr = band.shape[1]
    return pl.pallas_call(
        functools.partial(_na_bias_body, rows),
        grid=(len(sigs), heads),
        in_specs=[pl.BlockSpec((1, n_dr, GRID_W, GRID_W), lambda c, h: (h, 0, 0, 0))],
        out_specs=pl.BlockSpec((1, 1, NA_RB * GRID_W, uw * GRID_W), lambda c, h: (c, h, 0, 0)),
        out_shape=jax.ShapeDtypeStruct((len(sigs), heads, NA_RB * GRID_W, uw * GRID_W), F32),
        compiler_params=_params(("arbitrary", "arbitrary")),
        name="na_bias",
    )(band)


def _na_attention(qkv, bias, n_batch, seq, ctx_len):
    n, d3 = qkv.shape
    d = d3 // 3
    heads = d // HEAD_DIM
    rows = seq // GRID_W
    qb = NA_RB * GRID_W
    assert seq % qb == 0 and ctx_len % qb == 0 and heads % NA_HG == 0
    n_row_blocks = seq // qb
    n_ctx_blocks = ctx_len // qb
    hgs = heads // NA_HG
    w = NA_HG * HEAD_DIM
    lat_blocks = n_batch * n_row_blocks

    def q_map(hg, b, rb):
        blk = jnp.where(rb < n_row_blocks, b * n_row_blocks + rb, lat_blocks + b * n_ctx_blocks + rb - n_row_blocks)
        return (blk, hg)

    ctx_blk0 = (n_batch * seq) // ctx_len
    n_cls, _, bq, bk = bias.shape
    return pl.pallas_call(
        functools.partial(_na_body, n_row_blocks, rows),
        grid=(hgs, n_batch, n_row_blocks + n_ctx_blocks),
        in_specs=[pl.BlockSpec((qb, w), q_map),
                  pl.BlockSpec((seq, w), lambda hg, b, rb: (b, hgs + hg)),
                  pl.BlockSpec((seq, w), lambda hg, b, rb: (b, 2 * hgs + hg)),
                  pl.BlockSpec((ctx_len, w), lambda hg, b, rb: (ctx_blk0 + b, hgs + hg)),
                  pl.BlockSpec((ctx_len, w), lambda hg, b, rb: (ctx_blk0 + b, 2 * hgs + hg)),
                  pl.BlockSpec((n_cls, NA_HG, bq, bk), lambda hg, b, rb: (0, hg, 0, 0),
                               pipeline_mode=pl.Buffered(1))],
        out_specs=pl.BlockSpec((qb, w), q_map),
        out_shape=jax.ShapeDtypeStruct((n, d), BF16),
        scratch_shapes=[pltpu.VMEM((seq + ctx_len, 2 * w), BF16)],
        compiler_params=_params(("arbitrary", "arbitrary", "arbitrary")),
        name="na_attention",
    )(qkv, qkv, qkv, qkv, qkv, bias)


def _rope_tables(seq, n_rows):
    t = jnp.arange(seq, dtype=jnp.int32)
    row = (t // GRID_W).astype(F32)
    col = (t % GRID_W).astype(F32)
    axis_dim = HEAD_DIM // 2
    inv_freq = 1.0 / (ROPE_THETA ** (jnp.arange(0, axis_dim, 2, dtype=F32) / axis_dim))
    ang = jnp.concatenate([row[:, None] * inv_freq, col[:, None] * inv_freq], axis=-1)
    cos = jnp.repeat(jnp.cos(ang), 2, axis=-1)
    sin = jnp.repeat(jnp.sin(ang), 2, axis=-1) * jnp.tile(jnp.array([-1.0, 1.0], F32), HEAD_DIM // 2)
    reps = n_rows // seq
    return jnp.tile(cos, (reps, 1)), jnp.tile(sin, (reps, 1))


def _gqa_body(q_ref, k_ref, kc_ref, v_ref, vc_ref, o_ref, ka_ref, va_ref):
    seq = k_ref.shape[0]

    @pl.when(pl.program_id(2) == 0)
    def _():
        ka_ref[0:seq, :] = k_ref[...]
        ka_ref[seq:, :] = kc_ref[...]
        va_ref[0:seq, 0:HEAD_DIM] = v_ref[...]
        va_ref[seq:, 0:HEAD_DIM] = vc_ref[...]
        va_ref[:, HEAD_DIM:] = jnp.ones((va_ref.shape[0], va_ref.shape[1] - HEAD_DIM), BF16)

    for r in range(GQA_REP):
        cols = slice(r * HEAD_DIM, (r + 1) * HEAD_DIM)
        for r0 in range(0, q_ref.shape[0], GQA_BAND):
            rows = slice(r0, r0 + GQA_BAND)
            s = _qkt(q_ref[rows, cols], ka_ref[...])
            p = jnp.exp(s - s.max(axis=-1, keepdims=True)).astype(BF16)
            ox = jnp.dot(p, va_ref[...], preferred_element_type=F32)
            o_ref[rows, cols] = (ox[:, 0:HEAD_DIM] / ox[:, HEAD_DIM:HEAD_DIM + 1]).astype(o_ref.dtype)


def _gqa_attention(qk, qkv, n_batch, seq, ctx_len, q_cols):
    kv_heads = q_cols // HEAD_DIM // GQA_REP
    assert seq % GQA_QB == 0 and (n_batch * seq) % ctx_len == 0
    w = GQA_REP * HEAD_DIM
    qblocks = seq // GQA_QB
    k_blk0 = q_cols // HEAD_DIM
    v_blk0 = k_blk0 + kv_heads
    ctx_blk0 = (n_batch * seq) // ctx_len
    return pl.pallas_call(
        _gqa_body,
        grid=(n_batch, kv_heads, qblocks),
        in_specs=[pl.BlockSpec((GQA_QB, w), lambda b, h, i: (b * qblocks + i, h)),
                  pl.BlockSpec((seq, HEAD_DIM), lambda b, h, i: (b, k_blk0 + h)),
                  pl.BlockSpec((ctx_len, HEAD_DIM), lambda b, h, i: (ctx_blk0 + b, k_blk0 + h)),
                  pl.BlockSpec((seq, HEAD_DIM), lambda b, h, i: (b, v_blk0 + h)),
                  pl.BlockSpec((ctx_len, HEAD_DIM), lambda b, h, i: (ctx_blk0 + b, v_blk0 + h))],
        out_specs=pl.BlockSpec((GQA_QB, w), lambda b, h, i: (b * qblocks + i, h)),
        out_shape=jax.ShapeDtypeStruct((n_batch * seq, q_cols), BF16),
        scratch_shapes=[pltpu.VMEM((seq + ctx_len, HEAD_DIM), BF16),
                        pltpu.VMEM((seq + ctx_len, 2 * HEAD_DIM), BF16)],
        compiler_params=_params(("arbitrary", "arbitrary", "arbitrary")),
        name="gqa_attention",
    )(qk, qk, qk, qkv, qkv)


def _router_body(n_experts, lg_ref, b_ref, id_ref, wt_ref):
    lt = lg_ref[...].T[:n_experts] + b_ref[...]
    rows = [lt[e:e + 1] for e in range(n_experts)]
    mx = functools.reduce(jnp.maximum, rows)
    ex = [jnp.exp(r - mx) for r in rows]
    den = functools.reduce(jnp.add, ex)
    probs = [e / den for e in ex]

    epg = n_experts // N_GROUPS
    best = None
    for g in range(N_GROUPS):
        p = probs[g * epg:(g + 1) * epg]
        top1 = functools.reduce(jnp.maximum, p)
        i1 = jnp.full(top1.shape, epg, jnp.int32)
        for j in reversed(range(epg)):
            i1 = jnp.where(p[j] == top1, j, i1)
        rest = [jnp.where(i1 == j, -1.0, p[j]) for j in range(epg)]
        top2 = functools.reduce(jnp.maximum, rest)
        i2 = jnp.full(top1.shape, epg, jnp.int32)
        for j in reversed(range(epg)):
            i2 = jnp.where((rest[j] == top2) & (i1 != j), j, i2)
        cand = (top1 + top2, top1, top2, i1 + g * epg, i2 + g * epg)
        if best is None:
            best = cand
        else:
            take = cand[0] > best[0]
            best = tuple(jnp.where(take, c, b) for c, b in zip(cand, best))
    score, top1, top2, e1, e2 = best
    id_ref[...] = jnp.zeros(id_ref.shape, jnp.int32)
    wt_ref[...] = jnp.zeros(wt_ref.shape, F32)
    id_ref[0:1, :] = e1
    id_ref[1:2, :] = e2
    wt_ref[0:1, :] = top1 / score
    wt_ref[1:2, :] = top2 / score


def _router(logits, router_b):
    n = logits.shape[0]
    e = router_b.shape[0]
    tm = 512
    return pl.pallas_call(
        functools.partial(_router_body, e),
        grid=(n // tm,),
        in_specs=[pl.BlockSpec((tm, LANES), lambda i: (i, 0)),
                  pl.BlockSpec((e, 1), lambda i: (0, 0))],
        out_specs=[pl.BlockSpec((8, tm), lambda i: (0, i)), pl.BlockSpec((8, tm), lambda i: (0, i))],
        out_shape=[jax.ShapeDtypeStruct((8, n), jnp.int32), jax.ShapeDtypeStruct((8, n), F32)],
        compiler_params=_params(("arbitrary",)),
        name="router",
    )(logits, router_b.reshape(e, 1).astype(F32))


def _dispatch_plan(ids, n_experts, n_tiles):
    e0, e1 = ids[0], ids[1]
    n = e0.shape[0]
    ar = jnp.arange(n_experts, dtype=jnp.int32)[:, None]
    oh0 = (e0[None, :] == ar).astype(jnp.int32)
    oh1 = (e1[None, :] == ar).astype(jnp.int32)
    sel = oh0 + oh1
    csum = jnp.cumsum(sel, axis=1)
    counts = csum[:, -1]
    padded = ((counts + MOE_TM - 1) // MOE_TM) * MOE_TM
    ends = jnp.cumsum(padded)
    offs = ends - padded
    slot = offs[:, None] + csum - 1
    pos0 = jnp.sum(oh0 * slot, axis=0)
    pos1 = jnp.sum(oh1 * slot, axis=0)
    tok = jnp.arange(n, dtype=jnp.int32)
    src = jnp.zeros((n_tiles * MOE_TM,), jnp.int32)
    src = src.at[jnp.concatenate([pos0, pos1])].set(jnp.concatenate([tok, tok]), unique_indices=True)
    n_used = (ends[-1] // MOE_TM).astype(jnp.int32)
    tile_start = jnp.minimum(jnp.arange(n_tiles, dtype=jnp.int32), n_used - 1) * MOE_TM
    nonempty = (counts > 0).astype(jnp.int32)
    tile_seg = jnp.sum((ends[None, :] <= tile_start[:, None]).astype(jnp.int32) * nonempty[None, :], axis=1)
    seg_of_expert = jnp.cumsum(nonempty) - 1
    seg_expert = jnp.sum(jnp.where((seg_of_expert[None, :] == ar) & (nonempty[None, :] > 0), ar.T, 0), axis=1)
    meta = jnp.stack([n_used, jnp.sum(nonempty)]).astype(jnp.int32)
    return src, jnp.stack([pos0, pos1]), (tile_seg.astype(jnp.int32), seg_expert.astype(jnp.int32), meta)


def _row_copy(src_hbm, row, dst, k, sem):
    return pltpu.make_async_copy(src_hbm.at[pl.ds(row, 1), :], dst.at[pl.ds(k, 1), :], sem)


def _issue_token_gather(idx_ref, first, h_hbm, tile, sem):
    def one(k, carry):
        pltpu.make_async_copy(h_hbm.at[idx_ref[first + k]], tile.at[k], sem).start()
        return carry
    lax.fori_loop(0, tile.shape[0], one, 0, unroll=8)


def _wait_token_gather(h_hbm, tile, sem):
    def one(k, carry):
        pltpu.make_async_copy(h_hbm.at[0], tile.at[k], sem).wait()
        return carry
    lax.fori_loop(0, tile.shape[0], one, 0, unroll=8)


def _stream_expert_weights(seg_ref, sege_ref, meta_ref, w_hbms, chunk, wf, wb, sem):
    j, t = pl.program_id(0), pl.program_id(1)
    n_j = pl.num_programs(0)
    n_seg = meta_ref[1]
    k = seg_ref[t]
    first = (t == 0) | (k != seg_ref[jnp.maximum(t - 1, 0)])

    def copies(jj, kk):
        e = sege_ref[kk]
        return [pltpu.make_async_copy(w.at[e, :, pl.ds(jj * chunk, chunk)], wf.at[m], sem)
                for m, w in enumerate(w_hbms)]

    @pl.when(first)
    def _():
        @pl.when((j == 0) & (t == 0))
        def _():
            for c in copies(j, k):
                c.start(priority=1)

        for c in copies(j, k):
            c.wait()
        for m in range(len(w_hbms)):
            _convert_rows(wb.at[m], wf.at[m])

        wrap = k + 1 == n_seg
        nj = jnp.where(wrap, j + 1, j)
        nk = jnp.where(wrap, 0, k + 1)

        @pl.when(nj < n_j)
        def _():
            for c in copies(nj, nk):
                c.start(priority=1)


def _moe_up_body(seg_ref, sege_ref, meta_ref, src_ref, h_hbm, wg_hbm, wu_hbm, o_ref,
                 raw0, raw1, xb0, xb1, gsem, wf, wb, wsem):
    j, t = pl.program_id(0), pl.program_id(1)
    n_j = pl.num_programs(0)
    n_used = meta_ref[0]
    total = n_j * n_used
    a = j * n_used + t
    raws, xbs = (raw0, raw1), (xb0, xb1)

    def tile_of(idx):
        return lax.rem(idx, n_used) * MOE_TM

    def relayout(p):
        xbs[p][...] = raws[p][...].reshape(xbs[p].shape).astype(xbs[p].dtype)

    def multiply(p):
        x = xbs[p][...]
        g = jnp.dot(x, wb[0], preferred_element_type=F32)
        u = jnp.dot(x, wb[1], preferred_element_type=F32)
        o_ref[...] = (g * jax.nn.sigmoid(g) * u).astype(o_ref.dtype)

    _stream_expert_weights(seg_ref, sege_ref, meta_ref, (wg_hbm, wu_hbm), o_ref.shape[1], wf, wb, wsem)

    @pl.when(a == 0)
    def _():
        _issue_token_gather(src_ref, 0, h_hbm, raw0, gsem.at[0])
        _wait_token_gather(h_hbm, raw0, gsem.at[0])
        relayout(0)

        @pl.when(total > 1)
        def _():
            _issue_token_gather(src_ref, tile_of(1), h_hbm, raw1, gsem.at[1])

    active = t < n_used
    for p in range(2):
        mine = active & (jnp.bitwise_and(a, 1) == p)

        @pl.when(mine & (a + 2 < total))
        def _():
            _issue_token_gather(src_ref, tile_of(a + 2), h_hbm, raws[p], gsem.at[p])

        @pl.when(mine & (a + 1 < total))
        def _():
            _wait_token_gather(h_hbm, raws[1 - p], gsem.at[1 - p])
            relayout(1 - p)
            multiply(p)

        @pl.when(mine & (a + 1 >= total))
        def _():
            multiply(p)

    @pl.when(t >= n_used)
    def _():
        o_ref[...] = jnp.zeros(o_ref.shape, o_ref.dtype)


def _moe_up(h_rows, src, w_gate, w_up, plan):
    _, pieces, _ = h_rows.shape
    d = pieces * LANES
    r = src.shape[0]
    f = w_gate.shape[2]
    n_tiles = r // MOE_TM
    fc = MOE_FC if f % MOE_FC == 0 else f
    any_spec = pl.BlockSpec(memory_space=pl.ANY)
    return pl.pallas_call(
        _moe_up_body,
        grid_spec=pltpu.PrefetchScalarGridSpec(
            num_scalar_prefetch=4,
            grid=(f // fc, n_tiles),
            in_specs=[any_spec, any_spec, any_spec],
            out_specs=pl.BlockSpec((MOE_TM, fc), lambda j, t, sg, se, mt, sr: (t, j)),
            scratch_shapes=[pltpu.VMEM((MOE_TM, pieces, LANES), F32), pltpu.VMEM((MOE_TM, pieces, LANES), F32),
                            pltpu.VMEM((MOE_TM, d), BF16), pltpu.VMEM((MOE_TM, d), BF16),
                            pltpu.SemaphoreType.DMA((2,)),
                            pltpu.VMEM((2, d, fc), F32), pltpu.VMEM((2, d, fc), BF16),
                            pltpu.SemaphoreType.DMA(())]),
        out_shape=jax.ShapeDtypeStruct((r, f), BF16),
        compiler_params=_params(("arbitrary", "arbitrary")),
        name="moe_up",
    )(*plan, src, h_rows, w_gate, w_up)


def _moe_down_body(seg_ref, sege_ref, meta_ref, h_ref, w_hbm, o_ref, wf, wb, sem):
    _stream_expert_weights(seg_ref, sege_ref, meta_ref, (w_hbm,), o_ref.shape[1], wf, wb, sem)
    t = pl.program_id(1)

    @pl.when(t < meta_ref[0])
    def _():
        o_ref[...] = jnp.dot(h_ref[...], wb[0], preferred_element_type=F32)

    @pl.when(t >= meta_ref[0])
    def _():
        o_ref[...] = jnp.zeros(o_ref.shape, o_ref.dtype)


def _moe_down(hs, w_down, plan):
    r, f = hs.shape
    d = w_down.shape[2]
    nc = min(MOE_NC, d)
    n_tiles = r // MOE_TM
    return pl.pallas_call(
        _moe_down_body,
        grid_spec=pltpu.PrefetchScalarGridSpec(
            num_scalar_prefetch=3,
            grid=(d // nc, n_tiles),
            in_specs=[pl.BlockSpec((MOE_TM, f), lambda j, t, sg, se, mt: (jnp.minimum(t, mt[0] - 1), 0)),
                      pl.BlockSpec(memory_space=pl.ANY)],
            out_specs=pl.BlockSpec((MOE_TM, nc), lambda j, t, sg, se, mt: (t, j)),
            scratch_shapes=[pltpu.VMEM((1, f, nc), F32), pltpu.VMEM((1, f, nc), BF16),
                            pltpu.SemaphoreType.DMA(())]),
        out_shape=jax.ShapeDtypeStruct((r, d), F32),
        compiler_params=_params(("arbitrary", "arbitrary")),
        name="moe_down",
    )(*plan, hs, w_down)


def _combine_body(has_next, pos_ref, y_hbm, wt_ref, x_ref, g_ref, gate_ref, *rest):
    if has_next:
        g2_ref, sh_ref, sc_ref, xo_ref, ho_ref, buf, sem = rest
    else:
        xo_ref, buf, sem = rest
    i = pl.program_id(0)
    n = pl.num_programs(0)
    n_tok = n * ROW_TILE

    def issue(step, slot):
        def one(k, carry):
            tok = step * ROW_TILE + k
            _row_copy(y_hbm, pos_ref[tok], buf.at[slot, 0], k, sem.at[slot]).start()
            _row_copy(y_hbm, pos_ref[n_tok + tok], buf.at[slot, 1], k, sem.at[slot]).start()
            return carry
        lax.fori_loop(0, ROW_TILE, one, 0, unroll=8)

    @pl.when(i == 0)
    def _():
        issue(0, 0)

    @pl.when(i + 1 < n)
    def _():
        issue(i + 1, (i + 1) % 2)

    slot = i % 2

    def wait_one(k, carry):
        _row_copy(y_hbm, 0, buf.at[slot, 0], k, sem.at[slot]).wait()
        _row_copy(y_hbm, 0, buf.at[slot, 1], k, sem.at[slot]).wait()
        return carry
    lax.fori_loop(0, ROW_TILE, wait_one, 0, unroll=8)

    wt = wt_ref[...]
    y = wt[:, 0:1] * buf[slot, 0] + wt[:, 1:2] * buf[slot, 1]
    xn = x_ref[...] + gate_ref[0] * _rms(y, g_ref[...])
    xo_ref[...] = xn
    if has_next:
        h = _rms(xn, g2_ref[...])
        ho_ref[...] = (h * (1.0 + sc_ref[0]) + sh_ref[0]).astype(ho_ref.dtype)


def _combine(ys, pos, wts, x, gain, mod, gate_chunk, n_lat, seq, nxt=None):
    n = wts.shape[0]
    d = ys.shape[1]
    n_batch = mod.shape[0] - 1
    n_lat_tiles = min(n_lat, n) // ROW_TILE

    def mk(chunk):
        def index_map(i, pos_ref):
            return (jnp.where(i < n_lat_tiles, i // (seq // ROW_TILE), n_batch), 0, chunk)
        return index_map

    row = pl.BlockSpec((ROW_TILE, d), lambda i, p: (i, 0))
    vec = pl.BlockSpec((1, d), lambda i, p: (0, 0))
    in_specs = [pl.BlockSpec(memory_space=pl.ANY),
                pl.BlockSpec((ROW_TILE, 2), lambda i, p: (i, 0)),
                row, vec, pl.BlockSpec((1, 1, d), mk(gate_chunk))]
    args = [ys, wts, x, gain.reshape(1, d), mod]
    out_specs = [row]
    out_shape = [jax.ShapeDtypeStruct((n, d), F32)]
    if nxt is not None:
        gain2, mod2, shift_chunk, scale_chunk, h_dtype = nxt
        in_specs += [vec, pl.BlockSpec((1, 1, d), mk(shift_chunk)), pl.BlockSpec((1, 1, d), mk(scale_chunk))]
        args += [gain2.reshape(1, d), mod2, mod2]
        out_specs.append(row)
        out_shape.append(jax.ShapeDtypeStruct((n, d), h_dtype))
    out = pl.pallas_call(
        functools.partial(_combine_body, nxt is not None),
        grid_spec=pltpu.PrefetchScalarGridSpec(
            num_scalar_prefetch=1,
            grid=(n // ROW_TILE,),
            in_specs=in_specs,
            out_specs=out_specs,
            scratch_shapes=[pltpu.VMEM((2, 2, ROW_TILE, d), F32), pltpu.SemaphoreType.DMA((2,))]),
        out_shape=out_shape,
        compiler_params=_params(("arbitrary",)),
        name="moe_combine",
    )(pos.reshape(-1), *args)
    return out if nxt is not None else out[0]


def _moe(h_rows, logits, router_b, w_gate, w_up, w_down):
    n = logits.shape[0]
    n_experts = router_b.shape[0]
    n_tiles = (TOP_K * n) // MOE_TM + n_experts
    ids, wts = _router(logits, router_b)
    src, pos, plan = _dispatch_plan(ids, n_experts, n_tiles)
    hs = _moe_up(h_rows, src, w_gate, w_up, plan)
    ys = _moe_down(hs, w_down, plan)
    return ys, pos, wts[:2].T


def kernel(x, c, ctx, c_ctx, router_w, router_b, l0_ada_w, l0_ada_b, l0_norm_pre_mix, l0_norm_post_mix, l0_norm_pre_ffn, l0_norm_post_ffn, l0_na_w_qkv, l0_na_rpb, l0_na_w_o, l0_moe_w_gate, l0_moe_w_up, l0_moe_w_down, l1_ada_w, l1_ada_b, l1_norm_pre_mix, l1_norm_post_mix, l1_norm_pre_ffn, l1_norm_post_ffn, l1_gqa_w_qkv, l1_gqa_q_gain, l1_gqa_k_gain, l1_gqa_w_o, l1_moe_w_gate, l1_moe_w_up, l1_moe_w_down):
    n_batch, seq, d = x.shape
    ctx_len = ctx.shape[1]
    n_lat = n_batch * seq
    n_all = n_lat + n_batch * ctx_len
    scale = HEAD_DIM ** -0.5
    SH_M, SC_M, G_M, SH_F, SC_F, G_F = range(6)

    cvec = jnp.zeros((8, d), F32).at[:n_batch].set(c).at[n_batch].set(c_ctx)
    mod0 = _adaln(cvec, l0_ada_w, l0_ada_b)[:n_batch + 1].reshape(n_batch + 1, 1, 6 * d)
    mod1 = _adaln(cvec, l1_ada_w, l1_ada_b)[:n_batch + 1].reshape(n_batch + 1, 1, 6 * d)

    x_lat = x.reshape(n_lat, d)
    x_ctx = ctx.reshape(n_batch * ctx_len, d)

    h = _norm_mod(x_lat, x_ctx, l0_norm_pre_mix, mod0, SH_M, SC_M, seq, BF16)
    qscale = jnp.concatenate([jnp.full((1, d), scale, F32), jnp.ones((1, 2 * d), F32)], axis=1)
    qkv = _matmul(h, l0_na_w_qkv, qscale, BF16)
    bias = _na_bias_table(l0_na_rpb, seq // GRID_W)
    att = _na_attention(qkv, bias, n_batch, seq, ctx_len)
    y = _matmul(att, l0_na_w_o, jnp.ones((1, d), F32), F32)
    xa, h_rows, logits = _residual(x_lat, x_ctx, y, l0_norm_post_mix, mod0, G_M, seq,
                                   l0_norm_pre_ffn, SH_F, SC_F, router_w)
    ys, pos, wts = _moe(h_rows, logits, router_b, l0_moe_w_gate, l0_moe_w_up, l0_moe_w_down)
    xa, h = _combine(ys, pos, wts, xa, l0_norm_post_ffn, mod0, G_F, n_lat, seq,
                     nxt=(l1_norm_pre_mix, mod1, SH_M, SC_M, BF16))

    q_cols = d
    kv_cols = d // GQA_REP
    cos, sin = _rope_tables(seq, n_lat)
    ident = (jnp.ones((n_all - n_lat, HEAD_DIM), F32), jnp.zeros((n_all - n_lat, HEAD_DIM), F32))
    cos = jnp.concatenate([cos, ident[0]], axis=0)
    sin = jnp.concatenate([sin, ident[1]], axis=0)
    gain_post = jnp.stack([jnp.stack([l1_gqa_q_gain.astype(F32), jnp.full((HEAD_DIM,), scale, F32)]),
                           jnp.stack([l1_gqa_k_gain.astype(F32), jnp.ones((HEAD_DIM,), F32)])])
    qkv = _qkv_gqa(h, l1_gqa_w_qkv, cos, sin, gain_post, q_cols, kv_cols)
    att = _gqa_attention(qkv, qkv, n_batch, seq, ctx_len, q_cols)
    y = _matmul(att, l1_gqa_w_o, jnp.ones((1, d), F32), F32)
    xl, h_rows, logits = _residual(xa, xa, y, l1_norm_post_mix, mod1, G_M, seq,
                                   l1_norm_pre_ffn, SH_F, SC_F, router_w)
    ys, pos, wts = _moe(h_rows, logits, router_b, l1_moe_w_gate, l1_moe_w_up, l1_moe_w_down)
    xl = _combine(ys, pos, wts, xl, l1_norm_post_ffn, mod1, G_F, n_lat, seq)
    return xl.reshape(n_batch, seq, d)
```

```python
import functools

import jax
import jax.numpy as jnp
from jax import lax
from jax.experimental import pallas as pl
from jax.experimental.pallas import tpu as pltpu

LANES = 128
SUBLANES = 8
BF16_SUBLANES = 16
GRID_W = 64
HEAD_DIM = 128
NA_WIN_H = 8
NA_WIN_W = 16
GQA_REP = 4
ROPE_THETA = 10000.0
N_GROUPS = 4
TOP_K = 2
EPS = 1e-6
NEG_INF = -1e30

F32 = jnp.float32
BF16 = jnp.bfloat16

ROW_TILE = 256
MM_TM = 1024
MM_TN = 512
MOE_TM = 256
MOE_FC = 768
MOE_NC = 4096
NA_RB = 4
NA_HG = 8
GQA_QB = 512
GQA_BAND = 128
ADA_TN = 512
GATHER_UNROLL = 8
VMEM_LIMIT = 56 * 1024 * 1024


def _params(sem):
    return pltpu.CompilerParams(dimension_semantics=sem, vmem_limit_bytes=VMEM_LIMIT)


def _adaln_body(c_ref, w_ref, b_ref, o_ref):
    c = c_ref[...]
    s = (c * jax.nn.sigmoid(c)).astype(BF16)
    o_ref[...] = jnp.dot(s, w_ref[...].astype(BF16), preferred_element_type=F32) + b_ref[...]


def _adaln(cvec, ada_w, ada_b):
    m, d = cvec.shape
    n = ada_w.shape[1]
    return pl.pallas_call(
        _adaln_body,
        grid=(n // ADA_TN,),
        in_specs=[pl.BlockSpec((m, d), lambda j: (0, 0)),
                  pl.BlockSpec((d, ADA_TN), lambda j: (0, j)),
                  pl.BlockSpec((1, ADA_TN), lambda j: (0, j))],
        out_specs=pl.BlockSpec((m, ADA_TN), lambda j: (0, j)),
        out_shape=jax.ShapeDtypeStruct((m, n), F32),
        compiler_params=_params(("arbitrary",)),
        name="adaln",
    )(cvec, ada_w, ada_b.reshape(1, n))


def _convert_rows(dst_ref, src_ref):
    rows = src_ref.shape[0]
    band = next(b for b in (256, 128, 64, 32, 16) if rows % b == 0)

    def one(i, carry):
        r0 = pl.multiple_of(i * band, band)
        dst_ref[pl.ds(r0, band), :] = src_ref[pl.ds(r0, band), :].astype(dst_ref.dtype)
        return carry
    lax.fori_loop(0, rows // band, one, 0)


def _rms(x, gain):
    return x * lax.rsqrt(jnp.mean(x * x, axis=-1, keepdims=True) + EPS) * gain


def _stream_rows(n_lat_tiles, xl_ref, xc_ref):
    return jnp.where(pl.program_id(0) < n_lat_tiles, xl_ref[...], xc_ref[...])


def _stream_specs(n_lat_tiles, d):
    return [pl.BlockSpec((ROW_TILE, d), lambda i: (jnp.minimum(i, n_lat_tiles - 1), 0)),
            pl.BlockSpec((ROW_TILE, d), lambda i: (jnp.maximum(i - n_lat_tiles, 0), 0))]


def _norm_mod_body(n_lat_tiles, xl_ref, xc_ref, g_ref, sh_ref, sc_ref, o_ref):
    y = _rms(_stream_rows(n_lat_tiles, xl_ref, xc_ref), g_ref[...])
    o_ref[...] = (y * (1.0 + sc_ref[0]) + sh_ref[0]).astype(o_ref.dtype)


def _mod_row_map(n_lat_tiles, tiles_per_batch, n_batch, chunk):
    def index_map(i):
        return (jnp.where(i < n_lat_tiles, i // tiles_per_batch, n_batch), 0, chunk)
    return index_map


def _norm_mod(x_lat, x_ctx, gain, mod, shift_chunk, scale_chunk, seq, out_dtype):
    n_lat, d = x_lat.shape
    n = n_lat + x_ctx.shape[0]
    n_batch = mod.shape[0] - 1
    n_lat_tiles = n_lat // ROW_TILE
    mk = functools.partial(_mod_row_map, n_lat_tiles, seq // ROW_TILE, n_batch)
    return pl.pallas_call(
        functools.partial(_norm_mod_body, n_lat_tiles),
        grid=(n // ROW_TILE,),
        in_specs=_stream_specs(n_lat_tiles, d) + [
            pl.BlockSpec((1, d), lambda i: (0, 0)),
            pl.BlockSpec((1, 1, d), mk(shift_chunk)),
            pl.BlockSpec((1, 1, d), mk(scale_chunk))],
        out_specs=pl.BlockSpec((ROW_TILE, d), lambda i: (i, 0)),
        out_shape=jax.ShapeDtypeStruct((n, d), out_dtype),
        compiler_params=_params(("arbitrary",)),
        name="norm_mod",
    )(x_lat, x_ctx, gain.reshape(1, d), mod, mod)


def _split_bf16(x):
    hi = x.astype(BF16)
    lo = (x - hi.astype(F32)).astype(BF16)
    return hi, lo


def _residual_body(n_lat_tiles, xl_ref, xc_ref, y_ref, g_ref, gate_ref, g2_ref, sh_ref, sc_ref, rw_ref,
                   xo_ref, hr_ref, lg_ref):
    xn = _stream_rows(n_lat_tiles, xl_ref, xc_ref) + gate_ref[0] * _rms(y_ref[...], g_ref[...])
    xo_ref[...] = xn
    h = _rms(xn, g2_ref[...]) * (1.0 + sc_ref[0]) + sh_ref[0]
    hr_ref[...] = h.reshape(hr_ref.shape).astype(hr_ref.dtype)
    h_hi, h_lo = _split_bf16(h)
    w_hi, w_lo = _split_bf16(rw_ref[...])
    lg_ref[...] = (jnp.dot(h_hi, w_hi, preferred_element_type=F32)
                   + jnp.dot(h_lo, w_hi, preferred_element_type=F32)
                   + jnp.dot(h_hi, w_lo, preferred_element_type=F32))


def _residual(x_lat, x_ctx, y, gain, mod, gate_chunk, seq, gain2, shift_chunk, scale_chunk, router_w):
    n, d = y.shape
    n_batch = mod.shape[0] - 1
    pieces = d // LANES
    n_lat_tiles = min(x_lat.shape[0], n) // ROW_TILE
    mk = functools.partial(_mod_row_map, n_lat_tiles, seq // ROW_TILE, n_batch)
    row = pl.BlockSpec((ROW_TILE, d), lambda i: (i, 0))
    vec = pl.BlockSpec((1, d), lambda i: (0, 0))
    e = router_w.shape[1]
    w_pad = jnp.pad(router_w, ((0, 0), (0, LANES - e)))
    return pl.pallas_call(
        functools.partial(_residual_body, n_lat_tiles),
        grid=(n // ROW_TILE,),
        in_specs=_stream_specs(n_lat_tiles, d) + [
            row, vec, pl.BlockSpec((1, 1, d), mk(gate_chunk)), vec,
            pl.BlockSpec((1, 1, d), mk(shift_chunk)), pl.BlockSpec((1, 1, d), mk(scale_chunk)),
            pl.BlockSpec((d, LANES), lambda i: (0, 0))],
        out_specs=[row, pl.BlockSpec((ROW_TILE, pieces, LANES), lambda i: (i, 0, 0)),
                   pl.BlockSpec((ROW_TILE, LANES), lambda i: (i, 0))],
        out_shape=[jax.ShapeDtypeStruct((n, d), F32), jax.ShapeDtypeStruct((n, pieces, LANES), BF16),
                   jax.ShapeDtypeStruct((n, LANES), F32)],
        compiler_params=_params(("arbitrary",)),
        name="residual",
    )(x_lat, x_ctx, y, gain.reshape(1, d), mod, gain2.reshape(1, d), mod, mod, w_pad)


def _matmul_body(a_ref, w_ref, s_ref, o_ref, wb_ref):
    @pl.when(pl.program_id(1) == 0)
    def _():
        _convert_rows(wb_ref, w_ref)

    acc = jnp.dot(a_ref[...], wb_ref[...], preferred_element_type=F32)
    o_ref[...] = (acc * s_ref[...]).astype(o_ref.dtype)


def _row_tile(m):
    return next(t for t in (MM_TM, MM_TM // 2, MM_TM // 4) if m % t == 0)


def _matmul(a, w, col_scale, out_dtype):
    m, k = a.shape
    n = w.shape[1]
    tm = _row_tile(m)
    return pl.pallas_call(
        _matmul_body,
        grid=(n // MM_TN, m // tm),
        in_specs=[pl.BlockSpec((tm, k), lambda j, i: (i, 0)),
                  pl.BlockSpec((k, MM_TN), lambda j, i: (0, j)),
                  pl.BlockSpec((1, MM_TN), lambda j, i: (0, j))],
        out_specs=pl.BlockSpec((tm, MM_TN), lambda j, i: (i, j)),
        out_shape=jax.ShapeDtypeStruct((m, n), out_dtype),
        scratch_shapes=[pltpu.VMEM((k, MM_TN), BF16)],
        compiler_params=_params(("arbitrary", "arbitrary")),
        name="matmul",
    )(a, w, col_scale)


def _rope_head(y, cos, sin, even):
    partner = jnp.where(even, pltpu.roll(y, HEAD_DIM - 1, 1), pltpu.roll(y, 1, 1))
    return y * cos + partner * sin


def _qkv_gqa_body(n_qk_tiles, a_ref, w_ref, cos_ref, sin_ref, gp_ref, o_ref, wb_ref):
    j = pl.program_id(0)

    @pl.when(pl.program_id(1) == 0)
    def _():
        _convert_rows(wb_ref, w_ref)

    @pl.when(j < n_qk_tiles)
    def _():
        gain = gp_ref[0, 0:1, :]
        post = gp_ref[0, 1:2, :]
        band = min(ROW_TILE, a_ref.shape[0])
        even = (lax.broadcasted_iota(jnp.int32, (band, HEAD_DIM), 1) % 2) == 0
        for r0 in range(0, a_ref.shape[0], band):
            rows = slice(r0, r0 + band)
            acc = jnp.dot(a_ref[rows, :], wb_ref[...], preferred_element_type=F32)
            cos = cos_ref[rows, :]
            sin = sin_ref[rows, :]
            for h in range(acc.shape[1] // HEAD_DIM):
                cols = slice(h * HEAD_DIM, (h + 1) * HEAD_DIM)
                y = _rope_head(_rms(acc[:, cols], gain), cos, sin, even)
                o_ref[rows, cols] = (y * post).astype(o_ref.dtype)

    @pl.when(j >= n_qk_tiles)
    def _():
        o_ref[...] = jnp.dot(a_ref[...], wb_ref[...], preferred_element_type=F32).astype(o_ref.dtype)


def _qkv_gqa(a, w, cos, sin, gain_post, q_cols, kv_cols):
    m, k = a.shape
    n = w.shape[1]
    tm = _row_tile(m)
    tn = min(MM_TN, kv_cols)
    assert q_cols % tn == 0 and kv_cols % tn == 0
    n_q_tiles = q_cols // tn
    n_qk_tiles = (q_cols + kv_cols) // tn
    return pl.pallas_call(
        functools.partial(_qkv_gqa_body, n_qk_tiles),
        grid=(n // tn, m // tm),
        in_specs=[pl.BlockSpec((tm, k), lambda j, i: (i, 0)),
                  pl.BlockSpec((k, tn), lambda j, i: (0, j)),
                  pl.BlockSpec((tm, HEAD_DIM), lambda j, i: (i, 0)),
                  pl.BlockSpec((tm, HEAD_DIM), lambda j, i: (i, 0)),
                  pl.BlockSpec((1, 2, HEAD_DIM), lambda j, i: (jnp.where(j < n_q_tiles, 0, 1), 0, 0))],
        out_specs=pl.BlockSpec((tm, tn), lambda j, i: (i, j)),
        out_shape=jax.ShapeDtypeStruct((m, n), BF16),
        scratch_shapes=[pltpu.VMEM((k, tn), BF16)],
        compiler_params=_params(("arbitrary", "arbitrary")),
        name="qkv_gqa",
    )(a, w, cos, sin, gain_post)


def _softmax_pv(s_parts, vx_parts):
    m = s_parts[0].max(axis=-1, keepdims=True)
    for s in s_parts[1:]:
        m = jnp.maximum(m, s.max(axis=-1, keepdims=True))
    acc = None
    for s, vx in zip(s_parts, vx_parts):
        o = jnp.dot(jnp.exp(s - m).astype(BF16), vx, preferred_element_type=F32)
        acc = o if acc is None else acc + o
    return acc[:, 0:HEAD_DIM] / acc[:, HEAD_DIM:HEAD_DIM + 1]


def _qkt(q, k):
    return lax.dot_general(q, k, (((1,), (1,)), ((), ())), preferred_element_type=F32)


def _na_plan(rows):
    kh = min(NA_WIN_H, rows)
    uw = NA_RB + kh - 1
    uw += uw % 2
    assert rows >= uw and rows % NA_RB == 0
    sigs, cls_of_rb = [], []
    for rb in range(rows // NA_RB):
        rs_blk = min(max(rb * NA_RB - kh // 2, 0), rows - uw)
        offs = []
        for r in range(rb * NA_RB, (rb + 1) * NA_RB):
            rs = min(max(r - kh // 2, 0), rows - kh)
            assert rs_blk <= rs and rs + kh <= rs_blk + uw
            offs.append(rs - r)
        sig = (rs_blk - rb * NA_RB, tuple(offs))
        if sig not in sigs:
            sigs.append(sig)
        cls_of_rb.append(sigs.index(sig))
    return kh, uw, sigs, cls_of_rb


def _na_body(n_row_blocks, rows, q_ref, k_ref, v_ref, kc_ref, vc_ref, b_ref, o_ref, vx_ref):
    rb = pl.program_id(2)
    kh, uw, _, cls_of_rb = _na_plan(rows)
    span = uw * GRID_W
    seq = v_ref.shape[0]
    wide = 2 * HEAD_DIM

    @pl.when(rb == 0)
    def _():
        vx_ref[...] = jnp.ones(vx_ref.shape, vx_ref.dtype)
        for g in range(NA_HG):
            vx_ref[0:seq, g * wide:g * wide + HEAD_DIM] = v_ref[:, g * HEAD_DIM:(g + 1) * HEAD_DIM]
            vx_ref[seq:, g * wide:g * wide + HEAD_DIM] = vc_ref[:, g * HEAD_DIM:(g + 1) * HEAD_DIM]

    @pl.when(rb < n_row_blocks)
    def _():
        rs = jnp.clip(rb * NA_RB - kh // 2, 0, rows - uw)
        k0 = pl.multiple_of(rs * GRID_W, GRID_W)
        cls = jnp.int32(0)
        for i, c in enumerate(cls_of_rb):
            cls = jnp.where(rb == i, c, cls)
        for g in range(NA_HG):
            cols = slice(g * HEAD_DIM, (g + 1) * HEAD_DIM)
            xcols = slice(g * wide, (g + 1) * wide)
            q = q_ref[:, cols]
            s_loc = _qkt(q, k_ref[pl.ds(k0, span), cols]) + b_ref[cls, g]
            s_ctx = _qkt(q, kc_ref[:, cols])
            o = _softmax_pv([s_loc, s_ctx], [vx_ref[pl.ds(k0, span), xcols], vx_ref[seq:, xcols]])
            o_ref[:, cols] = o.astype(o_ref.dtype)

    @pl.when(rb >= n_row_blocks)
    def _():
        for g in range(NA_HG):
            cols = slice(g * HEAD_DIM, (g + 1) * HEAD_DIM)
            xcols = slice(g * wide, (g + 1) * wide)
            s = _qkt(q_ref[:, cols], kc_ref[:, cols])
            o_ref[:, cols] = _softmax_pv([s], [vx_ref[seq:, xcols]]).astype(o_ref.dtype)


def _na_bias_body(rows, band_ref, o_ref):
    kh, uw, sigs, _ = _na_plan(rows)
    neg = jnp.full((GRID_W, GRID_W), NEG_INF, F32)
    for c, (off, band_offs) in enumerate(sigs):
        @pl.when(pl.program_id(0) == c)
        def _():
            for i in range(NA_RB):
                for j in range(uw):
                    inside = 0 <= off + j - i - band_offs[i] < kh
                    tile = band_ref[0, off + j - i + NA_WIN_H - 1] if inside else neg
                    o_ref[0, 0, i * GRID_W:(i + 1) * GRID_W, j * GRID_W:(j + 1) * GRID_W] = tile


def _na_bias_table(rpb, rows):
    kh, uw, sigs, _ = _na_plan(rows)
    heads = rpb.shape[0]
    qcol = jnp.arange(GRID_W)
    col_start = jnp.clip(qcol - NA_WIN_W // 2, 0, GRID_W - NA_WIN_W)
    col_valid = (qcol[None, :] >= col_start[:, None]) & (qcol[None, :] < col_start[:, None] + NA_WIN_W)
    dc_idx = jnp.clip(qcol[None, :] - qcol[:, None], -(NA_WIN_W - 1), NA_WIN_W - 1) + NA_WIN_W - 1
    onehot = (dc_idx[None] == jnp.arange(2 * NA_WIN_W - 1)[:, None, None]).astype(F32)
    band = jnp.einsum('hrc,cqk->hrqk', rpb.astype(F32), onehot, precision=lax.Precision.HIGHEST)
    band = jnp.where(col_valid[None, None], band, NEG_INF)
    n_dr = band.shape[1]
    return pl.pallas_call(
        functools.partial(_na_bias_body, rows),
        grid=(len(sigs), heads),
        in_specs=[pl.BlockSpec((1, n_dr, GRID_W, GRID_W), lambda c, h: (h, 0, 0, 0))],
        out_specs=pl.BlockSpec((1, 1, NA_RB * GRID_W, uw * GRID_W), lambda c, h: (c, h, 0, 0)),
        out_shape=jax.ShapeDtypeStruct((len(sigs), heads, NA_RB * GRID_W, uw * GRID_W), F32),
        compiler_params=_params(("arbitrary", "arbitrary")),
        name="na_bias",
    )(band)


def _na_attention(qkv, bias, n_batch, seq, ctx_len):
    n, d3 = qkv.shape
    d = d3 // 3
    heads = d // HEAD_DIM
    rows = seq // GRID_W
    qb = NA_RB * GRID_W
    assert seq % qb == 0 and ctx_len % qb == 0 and heads % NA_HG == 0
    n_row_blocks = seq // qb
    n_ctx_blocks = ctx_len // qb
    hgs = heads // NA_HG
    w = NA_HG * HEAD_DIM
    lat_blocks = n_batch * n_row_blocks

    def q_map(hg, b, rb):
        blk = jnp.where(rb < n_row_blocks, b * n_row_blocks + rb, lat_blocks + b * n_ctx_blocks + rb - n_row_blocks)
        return (blk, hg)

    ctx_blk0 = (n_batch * seq) // ctx_len
    n_cls, _, bq, bk = bias.shape
    return pl.pallas_call(
        functools.partial(_na_body, n_row_blocks, rows),
        grid=(hgs, n_batch, n_row_blocks + n_ctx_blocks),
        in_specs=[pl.BlockSpec((qb, w), q_map),
                  pl.BlockSpec((seq, w), lambda hg, b, rb: (b, hgs + hg)),
                  pl.BlockSpec((seq, w), lambda hg, b, rb: (b, 2 * hgs + hg)),
                  pl.BlockSpec((ctx_len, w), lambda hg, b, rb: (ctx_blk0 + b, hgs + hg)),
                  pl.BlockSpec((ctx_len, w), lambda hg, b, rb: (ctx_blk0 + b, 2 * hgs + hg)),
                  pl.BlockSpec((n_cls, NA_HG, bq, bk), lambda hg, b, rb: (0, hg, 0, 0),
                               pipeline_mode=pl.Buffered(1))],
        out_specs=pl.BlockSpec((qb, w), q_map),
        out_shape=jax.ShapeDtypeStruct((n, d), BF16),
        scratch_shapes=[pltpu.VMEM((seq + ctx_len, 2 * w), BF16)],
        compiler_params=_params(("arbitrary", "arbitrary", "arbitrary")),
        name="na_attention",
    )(qkv, qkv, qkv, qkv, qkv, bias)


def _rope_tables(seq, n_rows):
    t = jnp.arange(seq, dtype=jnp.int32)
    row = (t // GRID_W).astype(F32)
    col = (t % GRID_W).astype(F32)
    axis_dim = HEAD_DIM // 2
    inv_freq = 1.0 / (ROPE_THETA ** (jnp.arange(0, axis_dim, 2, dtype=F32) / axis_dim))
    ang = jnp.concatenate([row[:, None] * inv_freq, col[:, None] * inv_freq], axis=-1)
    cos = jnp.repeat(jnp.cos(ang), 2, axis=-1)
    sin = jnp.repeat(jnp.sin(ang), 2, axis=-1) * jnp.tile(jnp.array([-1.0, 1.0], F32), HEAD_DIM // 2)
    reps = n_rows // seq
    return jnp.tile(cos, (reps, 1)), jnp.tile(sin, (reps, 1))


def _gqa_body(q_ref, k_ref, kc_ref, v_ref, vc_ref, o_ref, ka_ref, va_ref):
    seq = k_ref.shape[0]

    @pl.when(pl.program_id(2) == 0)
    def _():
        ka_ref[0:seq, :] = k_ref[...]
        ka_ref[seq:, :] = kc_ref[...]
        va_ref[0:seq, 0:HEAD_DIM] = v_ref[...]
        va_ref[seq:, 0:HEAD_DIM] = vc_ref[...]
        va_ref[:, HEAD_DIM:] = jnp.ones((va_ref.shape[0], va_ref.shape[1] - HEAD_DIM), BF16)

    for r in range(GQA_REP):
        cols = slice(r * HEAD_DIM, (r + 1) * HEAD_DIM)
        for r0 in range(0, q_ref.shape[0], GQA_BAND):
            rows = slice(r0, r0 + GQA_BAND)
            s = _qkt(q_ref[rows, cols], ka_ref[...])
            p = jnp.exp(s - s.max(axis=-1, keepdims=True)).astype(BF16)
            ox = jnp.dot(p, va_ref[...], preferred_element_type=F32)
            o_ref[rows, cols] = (ox[:, 0:HEAD_DIM] / ox[:, HEAD_DIM:HEAD_DIM + 1]).astype(o_ref.dtype)


def _gqa_attention(qk, qkv, n_batch, seq, ctx_len, q_cols):
    kv_heads = q_cols // HEAD_DIM // GQA_REP
    assert seq % GQA_QB == 0 and (n_batch * seq) % ctx_len == 0
    w = GQA_REP * HEAD_DIM
    qblocks = seq // GQA_QB
    k_blk0 = q_cols // HEAD_DIM
    v_blk0 = k_blk0 + kv_heads
    ctx_blk0 = (n_batch * seq) // ctx_len
    return pl.pallas_call(
        _gqa_body,
        grid=(n_batch, kv_heads, qblocks),
        in_specs=[pl.BlockSpec((GQA_QB, w), lambda b, h, i: (b * qblocks + i, h)),
                  pl.BlockSpec((seq, HEAD_DIM), lambda b, h, i: (b, k_blk0 + h)),
                  pl.BlockSpec((ctx_len, HEAD_DIM), lambda b, h, i: (ctx_blk0 + b, k_blk0 + h)),
                  pl.BlockSpec((seq, HEAD_DIM), lambda b, h, i: (b, v_blk0 + h)),
                  pl.BlockSpec((ctx_len, HEAD_DIM), lambda b, h, i: (ctx_blk0 + b, v_blk0 + h))],
        out_specs=pl.BlockSpec((GQA_QB, w), lambda b, h, i: (b * qblocks + i, h)),
        out_shape=jax.ShapeDtypeStruct((n_batch * seq, q_cols), BF16),
        scratch_shapes=[pltpu.VMEM((seq + ctx_len, HEAD_DIM), BF16),
                        pltpu.VMEM((seq + ctx_len, 2 * HEAD_DIM), BF16)],
        compiler_params=_params(("arbitrary", "arbitrary", "arbitrary")),
        name="gqa_attention",
    )(qk, qk, qk, qkv, qkv)


def _router_body(n_experts, lg_ref, b_ref, id_ref, wt_ref):
    lt = lg_ref[...].T[:n_experts] + b_ref[...]
    rows = [lt[e:e + 1] for e in range(n_experts)]
    mx = functools.reduce(jnp.maximum, rows)
    ex = [jnp.exp(r - mx) for r in rows]
    den = functools.reduce(jnp.add, ex)
    probs = [e / den for e in ex]

    epg = n_experts // N_GROUPS
    best = None
    for g in range(N_GROUPS):
        p = probs[g * epg:(g + 1) * epg]
        top1 = functools.reduce(jnp.maximum, p)
        i1 = jnp.full(top1.shape, epg, jnp.int32)
        for j in reversed(range(epg)):
            i1 = jnp.where(p[j] == top1, j, i1)
        rest = [jnp.where(i1 == j, -1.0, p[j]) for j in range(epg)]
        top2 = functools.reduce(jnp.maximum, rest)
        i2 = jnp.full(top1.shape, epg, jnp.int32)
        for j in reversed(range(epg)):
            i2 = jnp.where((rest[j] == top2) & (i1 != j), j, i2)
        cand = (top1 + top2, top1, top2, i1 + g * epg, i2 + g * epg)
        if best is None:
            best = cand
        else:
            take = cand[0] > best[0]
            best = tuple(jnp.where(take, c, b) for c, b in zip(cand, best))
    score, top1, top2, e1, e2 = best
    id_ref[...] = jnp.zeros(id_ref.shape, jnp.int32)
    wt_ref[...] = jnp.zeros(wt_ref.shape, F32)
    id_ref[0:1, :] = e1
    id_ref[1:2, :] = e2
    wt_ref[0:1, :] = top1 / score
    wt_ref[1:2, :] = top2 / score


def _router(logits, router_b):
    n = logits.shape[0]
    e = router_b.shape[0]
    tm = 512
    return pl.pallas_call(
        functools.partial(_router_body, e),
        grid=(n // tm,),
        in_specs=[pl.BlockSpec((tm, LANES), lambda i: (i, 0)),
                  pl.BlockSpec((e, 1), lambda i: (0, 0))],
        out_specs=[pl.BlockSpec((8, tm), lambda i: (0, i)), pl.BlockSpec((8, tm), lambda i: (0, i))],
        out_shape=[jax.ShapeDtypeStruct((8, n), jnp.int32), jax.ShapeDtypeStruct((8, n), F32)],
        compiler_params=_params(("arbitrary",)),
        name="router",
    )(logits, router_b.reshape(e, 1).astype(F32))


def _dispatch_plan(ids, n_experts, n_tiles):
    e0, e1 = ids[0], ids[1]
    n = e0.shape[0]
    ar = jnp.arange(n_experts, dtype=jnp.int32)[:, None]
    oh0 = (e0[None, :] == ar).astype(jnp.int32)
    oh1 = (e1[None, :] == ar).astype(jnp.int32)
    sel = oh0 + oh1
    csum = jnp.cumsum(sel, axis=1)
    counts = csum[:, -1]
    padded = ((counts + MOE_TM - 1) // MOE_TM) * MOE_TM
    ends = jnp.cumsum(padded)
    offs = ends - padded
    slot = offs[:, None] + csum - 1
    pos0 = jnp.sum(oh0 * slot, axis=0)
    pos1 = jnp.sum(oh1 * slot, axis=0)
    tok = jnp.arange(n, dtype=jnp.int32)
    src = jnp.zeros((n_tiles * MOE_TM,), jnp.int32)
    src = src.at[jnp.concatenate([pos0, pos1])].set(jnp.concatenate([tok, tok]), unique_indices=True)
    n_used = (ends[-1] // MOE_TM).astype(jnp.int32)
    tile_start = jnp.minimum(jnp.arange(n_tiles, dtype=jnp.int32), n_used - 1) * MOE_TM
    nonempty = (counts > 0).astype(jnp.int32)
    tile_seg = jnp.sum((ends[None, :] <= tile_start[:, None]).astype(jnp.int32) * nonempty[None, :], axis=1)
    seg_of_expert = jnp.cumsum(nonempty) - 1
    seg_expert = jnp.sum(jnp.where((seg_of_expert[None, :] == ar) & (nonempty[None, :] > 0), ar.T, 0), axis=1)
    meta = jnp.stack([n_used, jnp.sum(nonempty)]).astype(jnp.int32)
    return src, jnp.stack([pos0, pos1]), (tile_seg.astype(jnp.int32), seg_expert.astype(jnp.int32), meta)


def _row_copy(src_hbm, row, dst, k, sem):
    return pltpu.make_async_copy(src_hbm.at[pl.ds(row, 1), :], dst.at[pl.ds(k, 1), :], sem)


def _issue_token_gather(idx_ref, first, h_hbm, tile, sem):
    def one(g, carry):
        for r in range(GATHER_UNROLL):
            k = g * GATHER_UNROLL + r
            pltpu.make_async_copy(h_hbm.at[idx_ref[first + k]], tile.at[k], sem).start(priority=r % 2)
        return carry
    lax.fori_loop(0, tile.shape[0] // GATHER_UNROLL, one, 0)


def _wait_token_gather(h_hbm, tile, sem):
    def one(k, carry):
        pltpu.make_async_copy(h_hbm.at[0], tile.at[k], sem).wait()
        return carry
    lax.fori_loop(0, tile.shape[0], one, 0, unroll=8)


def _stream_expert_weights(seg_ref, sege_ref, meta_ref, w_hbms, chunk, wf, wb, sem, priority):
    j, t = pl.program_id(0), pl.program_id(1)
    n_j = pl.num_programs(0)
    n_seg = meta_ref[1]
    k = seg_ref[t]
    first = (t == 0) | (k != seg_ref[jnp.maximum(t - 1, 0)])

    def copies(jj, kk):
        e = sege_ref[kk]
        return [pltpu.make_async_copy(w.at[e, :, pl.ds(jj * chunk, chunk)], wf.at[m], sem)
                for m, w in enumerate(w_hbms)]

    @pl.when(first)
    def _():
        @pl.when((j == 0) & (t == 0))
        def _():
            for c in copies(j, k):
                c.start(priority=priority)

        for c in copies(j, k):
            c.wait()
        for m in range(len(w_hbms)):
            _convert_rows(wb.at[m], wf.at[m])

        wrap = k + 1 == n_seg
        nj = jnp.where(wrap, j + 1, j)
        nk = jnp.where(wrap, 0, k + 1)

        @pl.when(nj < n_j)
        def _():
            for c in copies(nj, nk):
                c.start(priority=priority)


def _moe_up_body(seg_ref, sege_ref, meta_ref, src_ref, h_hbm, wg_hbm, wu_hbm, o_ref,
                 raw0, raw1, xb0, xb1, gsem, wf, wb, wsem):
    j, t = pl.program_id(0), pl.program_id(1)
    n_j = pl.num_programs(0)
    n_used = meta_ref[0]
    total = n_j * n_used
    a = j * n_used + t
    raws, xbs = (raw0, raw1), (xb0, xb1)

    def tile_of(idx):
        return lax.rem(idx, n_used) * MOE_TM

    def relayout(p):
        xbs[p][...] = raws[p][...].astype(F32).reshape(xbs[p].shape).astype(xbs[p].dtype)

    def multiply(p):
        x = xbs[p][...]
        g = jnp.dot(x, wb[0], preferred_element_type=F32)
        u = jnp.dot(x, wb[1], preferred_element_type=F32)
        o_ref[...] = (g * jax.nn.sigmoid(g) * u).astype(o_ref.dtype)

    @pl.when(a == 0)
    def _():
        _issue_token_gather(src_ref, 0, h_hbm, raw0, gsem.at[0])
        _wait_token_gather(h_hbm, raw0, gsem.at[0])
        relayout(0)

        @pl.when(total > 1)
        def _():
            _issue_token_gather(src_ref, tile_of(1), h_hbm, raw1, gsem.at[1])

    active = t < n_used
    for p in range(2):
        @pl.when(active & (jnp.bitwise_and(a, 1) == p) & (a + 2 < total))
        def _():
            _issue_token_gather(src_ref, tile_of(a + 2), h_hbm, raws[p], gsem.at[p])

    _stream_expert_weights(seg_ref, sege_ref, meta_ref, (wg_hbm, wu_hbm), o_ref.shape[1], wf, wb, wsem, 1)

    for p in range(2):
        mine = active & (jnp.bitwise_and(a, 1) == p)

        @pl.when(mine & (a + 1 < total))
        def _():
            _wait_token_gather(h_hbm, raws[1 - p], gsem.at[1 - p])
            relayout(1 - p)
            multiply(p)

        @pl.when(mine & (a + 1 >= total))
        def _():
            multiply(p)

    @pl.when(t >= n_used)
    def _():
        o_ref[...] = jnp.zeros(o_ref.shape, o_ref.dtype)


def _moe_up(h_rows, src, w_gate, w_up, plan):
    _, pieces, _ = h_rows.shape
    d = pieces * LANES
    r = src.shape[0]
    f = w_gate.shape[2]
    n_tiles = r // MOE_TM
    fc = MOE_FC if f % MOE_FC == 0 else f
    any_spec = pl.BlockSpec(memory_space=pl.ANY)
    return pl.pallas_call(
        _moe_up_body,
        grid_spec=pltpu.PrefetchScalarGridSpec(
            num_scalar_prefetch=4,
            grid=(f // fc, n_tiles),
            in_specs=[any_spec, any_spec, any_spec],
            out_specs=pl.BlockSpec((MOE_TM, fc), lambda j, t, sg, se, mt, sr: (t, j)),
            scratch_shapes=[pltpu.VMEM((MOE_TM, pieces, LANES), BF16), pltpu.VMEM((MOE_TM, pieces, LANES), BF16),
                            pltpu.VMEM((MOE_TM, d), BF16), pltpu.VMEM((MOE_TM, d), BF16),
                            pltpu.SemaphoreType.DMA((2,)),
                            pltpu.VMEM((2, d, fc), F32), pltpu.VMEM((2, d, fc), BF16),
                            pltpu.SemaphoreType.DMA(())]),
        out_shape=jax.ShapeDtypeStruct((r, f), BF16),
        compiler_params=_params(("arbitrary", "arbitrary")),
        name="moe_up",
    )(*plan, src, h_rows, w_gate, w_up)


def _moe_down_body(seg_ref, sege_ref, meta_ref, h_ref, w_hbm, o_ref, wf, wb, sem):
    _stream_expert_weights(seg_ref, sege_ref, meta_ref, (w_hbm,), o_ref.shape[1], wf, wb, sem, 1)
    t = pl.program_id(1)

    @pl.when(t < meta_ref[0])
    def _():
        o_ref[...] = jnp.dot(h_ref[...], wb[0], preferred_element_type=F32)

    @pl.when(t >= meta_ref[0])
    def _():
        o_ref[...] = jnp.zeros(o_ref.shape, o_ref.dtype)


def _moe_down(hs, w_down, plan):
    r, f = hs.shape
    d = w_down.shape[2]
    nc = min(MOE_NC, d)
    n_tiles = r // MOE_TM
    return pl.pallas_call(
        _moe_down_body,
        grid_spec=pltpu.PrefetchScalarGridSpec(
            num_scalar_prefetch=3,
            grid=(d // nc, n_tiles),
            in_specs=[pl.BlockSpec((MOE_TM, f), lambda j, t, sg, se, mt: (jnp.minimum(t, mt[0] - 1), 0)),
                      pl.BlockSpec(memory_space=pl.ANY)],
            out_specs=pl.BlockSpec((MOE_TM, nc), lambda j, t, sg, se, mt: (t, j)),
            scratch_shapes=[pltpu.VMEM((1, f, nc), F32), pltpu.VMEM((1, f, nc), BF16),
                            pltpu.SemaphoreType.DMA(())]),
        out_shape=jax.ShapeDtypeStruct((r, d), F32),
        compiler_params=_params(("arbitrary", "arbitrary")),
        name="moe_down",
    )(*plan, hs, w_down)


def _combine_body(has_next, pos_ref, y_hbm, wt_ref, x_ref, g_ref, gate_ref, *rest):
    if has_next:
        g2_ref, sh_ref, sc_ref, xo_ref, ho_ref, buf, sem = rest
    else:
        xo_ref, buf, sem = rest
    i = pl.program_id(0)
    n = pl.num_programs(0)
    n_tok = n * ROW_TILE

    def issue(step, slot):
        def one(k, carry):
            tok = step * ROW_TILE + k
            _row_copy(y_hbm, pos_ref[tok], buf.at[slot, 0], k, sem.at[slot]).start()
            _row_copy(y_hbm, pos_ref[n_tok + tok], buf.at[slot, 1], k, sem.at[slot]).start()
            return carry
        lax.fori_loop(0, ROW_TILE, one, 0, unroll=8)

    @pl.when(i == 0)
    def _():
        issue(0, 0)

    @pl.when(i + 1 < n)
    def _():
        issue(i + 1, (i + 1) % 2)

    slot = i % 2

    def wait_one(k, carry):
        _row_copy(y_hbm, 0, buf.at[slot, 0], k, sem.at[slot]).wait()
        _row_copy(y_hbm, 0, buf.at[slot, 1], k, sem.at[slot]).wait()
        return carry
    lax.fori_loop(0, ROW_TILE, wait_one, 0, unroll=8)

    wt = wt_ref[...]
    y = wt[:, 0:1] * buf[slot, 0] + wt[:, 1:2] * buf[slot, 1]
    xn = x_ref[...] + gate_ref[0] * _rms(y, g_ref[...])
    xo_ref[...] = xn
    if has_next:
        h = _rms(xn, g2_ref[...])
        ho_ref[...] = (h * (1.0 + sc_ref[0]) + sh_ref[0]).astype(ho_ref.dtype)


def _combine(ys, pos, wts, x, gain, mod, gate_chunk, n_lat, seq, nxt=None):
    n = wts.shape[0]
    d = ys.shape[1]
    n_batch = mod.shape[0] - 1
    n_lat_tiles = min(n_lat, n) // ROW_TILE

    def mk(chunk):
        def index_map(i, pos_ref):
            return (jnp.where(i < n_lat_tiles, i // (seq // ROW_TILE), n_batch), 0, chunk)
        return index_map

    row = pl.BlockSpec((ROW_TILE, d), lambda i, p: (i, 0))
    vec = pl.BlockSpec((1, d), lambda i, p: (0, 0))
    in_specs = [pl.BlockSpec(memory_space=pl.ANY),
                pl.BlockSpec((ROW_TILE, 2), lambda i, p: (i, 0)),
                row, vec, pl.BlockSpec((1, 1, d), mk(gate_chunk))]
    args = [ys, wts, x, gain.reshape(1, d), mod]
    out_specs = [row]
    out_shape = [jax.ShapeDtypeStruct((n, d), F32)]
    if nxt is not None:
        gain2, mod2, shift_chunk, scale_chunk, h_dtype = nxt
        in_specs += [vec, pl.BlockSpec((1, 1, d), mk(shift_chunk)), pl.BlockSpec((1, 1, d), mk(scale_chunk))]
        args += [gain2.reshape(1, d), mod2, mod2]
        out_specs.append(row)
        out_shape.append(jax.ShapeDtypeStruct((n, d), h_dtype))
    out = pl.pallas_call(
        functools.partial(_combine_body, nxt is not None),
        grid_spec=pltpu.PrefetchScalarGridSpec(
            num_scalar_prefetch=1,
            grid=(n // ROW_TILE,),
            in_specs=in_specs,
            out_specs=out_specs,
            scratch_shapes=[pltpu.VMEM((2, 2, ROW_TILE, d), F32), pltpu.SemaphoreType.DMA((2,))]),
        out_shape=out_shape,
        compiler_params=_params(("arbitrary",)),
        name="moe_combine",
    )(pos.reshape(-1), *args)
    return out if nxt is not None else out[0]


def _moe(h_rows, logits, router_b, w_gate, w_up, w_down):
    n = logits.shape[0]
    n_experts = router_b.shape[0]
    n_tiles = (TOP_K * n) // MOE_TM + n_experts
    ids, wts = _router(logits, router_b)
    src, pos, plan = _dispatch_plan(ids, n_experts, n_tiles)
    hs = _moe_up(h_rows, src, w_gate, w_up, plan)
    ys = _moe_down(hs, w_down, plan)
    return ys, pos, wts[:2].T


def kernel(x, c, ctx, c_ctx, router_w, router_b, l0_ada_w, l0_ada_b, l0_norm_pre_mix, l0_norm_post_mix, l0_norm_pre_ffn, l0_norm_post_ffn, l0_na_w_qkv, l0_na_rpb, l0_na_w_o, l0_moe_w_gate, l0_moe_w_up, l0_moe_w_down, l1_ada_w, l1_ada_b, l1_norm_pre_mix, l1_norm_post_mix, l1_norm_pre_ffn, l1_norm_post_ffn, l1_gqa_w_qkv, l1_gqa_q_gain, l1_gqa_k_gain, l1_gqa_w_o, l1_moe_w_gate, l1_moe_w_up, l1_moe_w_down):
    n_batch, seq, d = x.shape
    ctx_len = ctx.shape[1]
    n_lat = n_batch * seq
    n_all = n_lat + n_batch * ctx_len
    scale = HEAD_DIM ** -0.5
    SH_M, SC_M, G_M, SH_F, SC_F, G_F = range(6)

    cvec = jnp.zeros((8, d), F32).at[:n_batch].set(c).at[n_batch].set(c_ctx)
    mod0 = _adaln(cvec, l0_ada_w, l0_ada_b)[:n_batch + 1].reshape(n_batch + 1, 1, 6 * d)
    mod1 = _adaln(cvec, l1_ada_w, l1_ada_b)[:n_batch + 1].reshape(n_batch + 1, 1, 6 * d)

    x_lat = x.reshape(n_lat, d)
    x_ctx = ctx.reshape(n_batch * ctx_len, d)

    h = _norm_mod(x_lat, x_ctx, l0_norm_pre_mix, mod0, SH_M, SC_M, seq, BF16)
    qscale = jnp.concatenate([jnp.full((1, d), scale, F32), jnp.ones((1, 2 * d), F32)], axis=1)
    qkv = _matmul(h, l0_na_w_qkv, qscale, BF16)
    bias = _na_bias_table(l0_na_rpb, seq // GRID_W)
    att = _na_attention(qkv, bias, n_batch, seq, ctx_len)
    y = _matmul(att, l0_na_w_o, jnp.ones((1, d), F32), F32)
    xa, h_rows, logits = _residual(x_lat, x_ctx, y, l0_norm_post_mix, mod0, G_M, seq,
                                   l0_norm_pre_ffn, SH_F, SC_F, router_w)
    ys, pos, wts = _moe(h_rows, logits, router_b, l0_moe_w_gate, l0_moe_w_up, l0_moe_w_down)
    xa, h = _combine(ys, pos, wts, xa, l0_norm_post_ffn, mod0, G_F, n_lat, seq,
                     nxt=(l1_norm_pre_mix, mod1, SH_M, SC_M, BF16))

    q_cols = d
    kv_cols = d // GQA_REP
    cos, sin = _rope_tables(seq, n_lat)
    ident = (jnp.ones((n_all - n_lat, HEAD_DIM), F32), jnp.zeros((n_all - n_lat, HEAD_DIM), F32))
    cos = jnp.concatenate([cos, ident[0]], axis=0)
    sin = jnp.concatenate([sin, ident[1]], axis=0)
    gain_post = jnp.stack([jnp.stack([l1_gqa_q_gain.astype(F32), jnp.full((HEAD_DIM,), scale, F32)]),
                           jnp.stack([l1_gqa_k_gain.astype(F32), jnp.ones((HEAD_DIM,), F32)])])
    qkv = _qkv_gqa(h, l1_gqa_w_qkv, cos, sin, gain_post, q_cols, kv_cols)
    att = _gqa_attention(qkv, qkv, n_batch, seq, ctx_len, q_cols)
    y = _matmul(att, l1_gqa_w_o, jnp.ones((1, d), F32), F32)
    xl, h_rows, logits = _residual(xa, xa, y, l1_norm_post_mix, mod1, G_M, seq,
                                   l1_norm_pre_ffn, SH_F, SC_F, router_w)
    ys, pos, wts = _moe(h_rows, logits, router_b, l1_moe_w_gate, l1_moe_w_up, l1_moe_w_down)
    xl = _combine(ys, pos, wts, xl, l1_norm_post_ffn, mod1, G_F, n_lat, seq)
    return xl.reshape(n_batch, seq, d)
```

```python
import functools

import jax
import jax.numpy as jnp
from jax import lax
from jax.experimental import pallas as pl
from jax.experimental.pallas import tpu as pltpu

LANES = 128
SUBLANES = 8
BF16_SUBLANES = 16
GRID_W = 64
HEAD_DIM = 128
NA_WIN_H = 8
NA_WIN_W = 16
GQA_REP = 4
ROPE_THETA = 10000.0
N_GROUPS = 4
TOP_K = 2
EPS = 1e-6
NEG_INF = -1e30

F32 = jnp.float32
BF16 = jnp.bfloat16

ROW_TILE = 256
MM_TM = 1024
MM_TN = 512
MOE_TM = 256
MOE_FC = 768
MOE_NC = 4096
NA_RB = 4
NA_HG = 8
GQA_QB = 512
GQA_BAND = 128
ADA_TN = 512
VMEM_LIMIT = 56 * 1024 * 1024


def _params(sem):
    return pltpu.CompilerParams(dimension_semantics=sem, vmem_limit_bytes=VMEM_LIMIT)


def _adaln_body(c_ref, w_ref, b_ref, o_ref):
    c = c_ref[...]
    s = (c * jax.nn.sigmoid(c)).astype(BF16)
    o_ref[...] = jnp.dot(s, w_ref[...].astype(BF16), preferred_element_type=F32) + b_ref[...]


def _adaln(cvec, ada_w, ada_b):
    m, d = cvec.shape
    n = ada_w.shape[1]
    return pl.pallas_call(
        _adaln_body,
        grid=(n // ADA_TN,),
        in_specs=[pl.BlockSpec((m, d), lambda j: (0, 0)),
                  pl.BlockSpec((d, ADA_TN), lambda j: (0, j)),
                  pl.BlockSpec((1, ADA_TN), lambda j: (0, j))],
        out_specs=pl.BlockSpec((m, ADA_TN), lambda j: (0, j)),
        out_shape=jax.ShapeDtypeStruct((m, n), F32),
        compiler_params=_params(("arbitrary",)),
        name="adaln",
    )(cvec, ada_w, ada_b.reshape(1, n))


def _convert_rows(dst_ref, src_ref):
    rows = src_ref.shape[0]
    band = next(b for b in (256, 128, 64, 32, 16) if rows % b == 0)

    def one(i, carry):
        r0 = pl.multiple_of(i * band, band)
        dst_ref[pl.ds(r0, band), :] = src_ref[pl.ds(r0, band), :].astype(dst_ref.dtype)
        return carry
    lax.fori_loop(0, rows // band, one, 0)


def _rms(x, gain):
    return x * lax.rsqrt(jnp.mean(x * x, axis=-1, keepdims=True) + EPS) * gain


def _stream_rows(n_lat_tiles, xl_ref, xc_ref):
    return jnp.where(pl.program_id(0) < n_lat_tiles, xl_ref[...], xc_ref[...])


def _stream_specs(n_lat_tiles, d):
    return [pl.BlockSpec((ROW_TILE, d), lambda i: (jnp.minimum(i, n_lat_tiles - 1), 0)),
            pl.BlockSpec((ROW_TILE, d), lambda i: (jnp.maximum(i - n_lat_tiles, 0), 0))]


def _norm_mod_body(n_lat_tiles, xl_ref, xc_ref, g_ref, sh_ref, sc_ref, o_ref):
    y = _rms(_stream_rows(n_lat_tiles, xl_ref, xc_ref), g_ref[...])
    o_ref[...] = (y * (1.0 + sc_ref[0]) + sh_ref[0]).astype(o_ref.dtype)


def _mod_row_map(n_lat_tiles, tiles_per_batch, n_batch, chunk):
    def index_map(i):
        return (jnp.where(i < n_lat_tiles, i // tiles_per_batch, n_batch), 0, chunk)
    return index_map


def _norm_mod(x_lat, x_ctx, gain, mod, shift_chunk, scale_chunk, seq, out_dtype):
    n_lat, d = x_lat.shape
    n = n_lat + x_ctx.shape[0]
    n_batch = mod.shape[0] - 1
    n_lat_tiles = n_lat // ROW_TILE
    mk = functools.partial(_mod_row_map, n_lat_tiles, seq // ROW_TILE, n_batch)
    return pl.pallas_call(
        functools.partial(_norm_mod_body, n_lat_tiles),
        grid=(n // ROW_TILE,),
        in_specs=_stream_specs(n_lat_tiles, d) + [
            pl.BlockSpec((1, d), lambda i: (0, 0)),
            pl.BlockSpec((1, 1, d), mk(shift_chunk)),
            pl.BlockSpec((1, 1, d), mk(scale_chunk))],
        out_specs=pl.BlockSpec((ROW_TILE, d), lambda i: (i, 0)),
        out_shape=jax.ShapeDtypeStruct((n, d), out_dtype),
        compiler_params=_params(("arbitrary",)),
        name="norm_mod",
    )(x_lat, x_ctx, gain.reshape(1, d), mod, mod)


def _split_bf16(x):
    hi = x.astype(BF16)
    lo = (x - hi.astype(F32)).astype(BF16)
    return hi, lo


def _residual_body(n_lat_tiles, xl_ref, xc_ref, y_ref, g_ref, gate_ref, g2_ref, sh_ref, sc_ref, rw_ref,
                   xo_ref, hr_ref, lg_ref):
    xn = _stream_rows(n_lat_tiles, xl_ref, xc_ref) + gate_ref[0] * _rms(y_ref[...], g_ref[...])
    xo_ref[...] = xn
    h = _rms(xn, g2_ref[...]) * (1.0 + sc_ref[0]) + sh_ref[0]
    hr_ref[...] = h.reshape(hr_ref.shape)
    h_hi, h_lo = _split_bf16(h)
    w_hi, w_lo = _split_bf16(rw_ref[...])
    lg_ref[...] = (jnp.dot(h_hi, w_hi, preferred_element_type=F32)
                   + jnp.dot(h_lo, w_hi, preferred_element_type=F32)
                   + jnp.dot(h_hi, w_lo, preferred_element_type=F32))


def _residual(x_lat, x_ctx, y, gain, mod, gate_chunk, seq, gain2, shift_chunk, scale_chunk, router_w):
    n, d = y.shape
    n_batch = mod.shape[0] - 1
    pieces = d // LANES
    n_lat_tiles = min(x_lat.shape[0], n) // ROW_TILE
    mk = functools.partial(_mod_row_map, n_lat_tiles, seq // ROW_TILE, n_batch)
    row = pl.BlockSpec((ROW_TILE, d), lambda i: (i, 0))
    vec = pl.BlockSpec((1, d), lambda i: (0, 0))
    e = router_w.shape[1]
    w_pad = jnp.pad(router_w, ((0, 0), (0, LANES - e)))
    return pl.pallas_call(
        functools.partial(_residual_body, n_lat_tiles),
        grid=(n // ROW_TILE,),
        in_specs=_stream_specs(n_lat_tiles, d) + [
            row, vec, pl.BlockSpec((1, 1, d), mk(gate_chunk)), vec,
            pl.BlockSpec((1, 1, d), mk(shift_chunk)), pl.BlockSpec((1, 1, d), mk(scale_chunk)),
            pl.BlockSpec((d, LANES), lambda i: (0, 0))],
        out_specs=[row, pl.BlockSpec((ROW_TILE, pieces, LANES), lambda i: (i, 0, 0)),
                   pl.BlockSpec((ROW_TILE, LANES), lambda i: (i, 0))],
        out_shape=[jax.ShapeDtypeStruct((n, d), F32), jax.ShapeDtypeStruct((n, pieces, LANES), F32),
                   jax.ShapeDtypeStruct((n, LANES), F32)],
        compiler_params=_params(("arbitrary",)),
        name="residual",
    )(x_lat, x_ctx, y, gain.reshape(1, d), mod, gain2.reshape(1, d), mod, mod, w_pad)


def _matmul_body(a_ref, w_ref, s_ref, o_ref, wb_ref):
    @pl.when(pl.program_id(1) == 0)
    def _():
        _convert_rows(wb_ref, w_ref)

    acc = jnp.dot(a_ref[...], wb_ref[...], preferred_element_type=F32)
    o_ref[...] = (acc * s_ref[...]).astype(o_ref.dtype)


def _row_tile(m):
    return next(t for t in (MM_TM, MM_TM // 2, MM_TM // 4) if m % t == 0)


def _matmul(a, w, col_scale, out_dtype):
    m, k = a.shape
    n = w.shape[1]
    tm = _row_tile(m)
    return pl.pallas_call(
        _matmul_body,
        grid=(n // MM_TN, m // tm),
        in_specs=[pl.BlockSpec((tm, k), lambda j, i: (i, 0)),
                  pl.BlockSpec((k, MM_TN), lambda j, i: (0, j)),
                  pl.BlockSpec((1, MM_TN), lambda j, i: (0, j))],
        out_specs=pl.BlockSpec((tm, MM_TN), lambda j, i: (i, j)),
        out_shape=jax.ShapeDtypeStruct((m, n), out_dtype),
        scratch_shapes=[pltpu.VMEM((k, MM_TN), BF16)],
        compiler_params=_params(("arbitrary", "arbitrary")),
        name="matmul",
    )(a, w, col_scale)


def _rope_head(y, cos, sin, even):
    partner = jnp.where(even, pltpu.roll(y, HEAD_DIM - 1, 1), pltpu.roll(y, 1, 1))
    return y * cos + partner * sin


def _qkv_gqa_body(n_qk_tiles, a_ref, w_ref, cos_ref, sin_ref, gp_ref, o_ref, wb_ref):
    j = pl.program_id(0)

    @pl.when(pl.program_id(1) == 0)
    def _():
        _convert_rows(wb_ref, w_ref)

    @pl.when(j < n_qk_tiles)
    def _():
        gain = gp_ref[0, 0:1, :]
        post = gp_ref[0, 1:2, :]
        band = min(ROW_TILE, a_ref.shape[0])
        even = (lax.broadcasted_iota(jnp.int32, (band, HEAD_DIM), 1) % 2) == 0
        for r0 in range(0, a_ref.shape[0], band):
            rows = slice(r0, r0 + band)
            acc = jnp.dot(a_ref[rows, :], wb_ref[...], preferred_element_type=F32)
            cos = cos_ref[rows, :]
            sin = sin_ref[rows, :]
            for h in range(acc.shape[1] // HEAD_DIM):
                cols = slice(h * HEAD_DIM, (h + 1) * HEAD_DIM)
                y = _rope_head(_rms(acc[:, cols], gain), cos, sin, even)
                o_ref[rows, cols] = (y * post).astype(o_ref.dtype)

    @pl.when(j >= n_qk_tiles)
    def _():
        o_ref[...] = jnp.dot(a_ref[...], wb_ref[...], preferred_element_type=F32).astype(o_ref.dtype)


def _qkv_gqa(a, w, cos, sin, gain_post, q_cols, kv_cols):
    m, k = a.shape
    n = w.shape[1]
    tm = _row_tile(m)
    tn = min(MM_TN, kv_cols)
    assert q_cols % tn == 0 and kv_cols % tn == 0
    n_q_tiles = q_cols // tn
    n_qk_tiles = (q_cols + kv_cols) // tn
    return pl.pallas_call(
        functools.partial(_qkv_gqa_body, n_qk_tiles),
        grid=(n // tn, m // tm),
        in_specs=[pl.BlockSpec((tm, k), lambda j, i: (i, 0)),
                  pl.BlockSpec((k, tn), lambda j, i: (0, j)),
                  pl.BlockSpec((tm, HEAD_DIM), lambda j, i: (i, 0)),
                  pl.BlockSpec((tm, HEAD_DIM), lambda j, i: (i, 0)),
                  pl.BlockSpec((1, 2, HEAD_DIM), lambda j, i: (jnp.where(j < n_q_tiles, 0, 1), 0, 0))],
        out_specs=pl.BlockSpec((tm, tn), lambda j, i: (i, j)),
        out_shape=jax.ShapeDtypeStruct((m, n), BF16),
        scratch_shapes=[pltpu.VMEM((k, tn), BF16)],
        compiler_params=_params(("arbitrary", "arbitrary")),
        name="qkv_gqa",
    )(a, w, cos, sin, gain_post)


def _softmax_pv(s_parts, vx_parts):
    m = s_parts[0].max(axis=-1, keepdims=True)
    for s in s_parts[1:]:
        m = jnp.maximum(m, s.max(axis=-1, keepdims=True))
    acc = None
    for s, vx in zip(s_parts, vx_parts):
        o = jnp.dot(jnp.exp(s - m).astype(BF16), vx, preferred_element_type=F32)
        acc = o if acc is None else acc + o
    return acc[:, 0:HEAD_DIM] / acc[:, HEAD_DIM:HEAD_DIM + 1]


def _qkt(q, k):
    return lax.dot_general(q, k, (((1,), (1,)), ((), ())), preferred_element_type=F32)


def _na_plan(rows):
    kh = min(NA_WIN_H, rows)
    uw = NA_RB + kh - 1
    uw += uw % 2
    assert rows >= uw and rows % NA_RB == 0
    sigs, cls_of_rb = [], []
    for rb in range(rows // NA_RB):
        rs_blk = min(max(rb * NA_RB - kh // 2, 0), rows - uw)
        offs = []
        for r in range(rb * NA_RB, (rb + 1) * NA_RB):
            rs = min(max(r - kh // 2, 0), rows - kh)
            assert rs_blk <= rs and rs + kh <= rs_blk + uw
            offs.append(rs - r)
        sig = (rs_blk - rb * NA_RB, tuple(offs))
        if sig not in sigs:
            sigs.append(sig)
        cls_of_rb.append(sigs.index(sig))
    return kh, uw, sigs, cls_of_rb


def _na_body(n_row_blocks, rows, q_ref, k_ref, v_ref, kc_ref, vc_ref, b_ref, o_ref, vx_ref):
    rb = pl.program_id(2)
    kh, uw, _, cls_of_rb = _na_plan(rows)
    span = uw * GRID_W
    seq = v_ref.shape[0]
    wide = 2 * HEAD_DIM

    @pl.when(rb == 0)
    def _():
        vx_ref[...] = jnp.ones(vx_ref.shape, vx_ref.dtype)
        for g in range(NA_HG):
            vx_ref[0:seq, g * wide:g * wide + HEAD_DIM] = v_ref[:, g * HEAD_DIM:(g + 1) * HEAD_DIM]
            vx_ref[seq:, g * wide:g * wide + HEAD_DIM] = vc_ref[:, g * HEAD_DIM:(g + 1) * HEAD_DIM]

    @pl.when(rb < n_row_blocks)
    def _():
        rs = jnp.clip(rb * NA_RB - kh // 2, 0, rows - uw)
        k0 = pl.multiple_of(rs * GRID_W, GRID_W)
        cls = jnp.int32(0)
        for i, c in enumerate(cls_of_rb):
            cls = jnp.where(rb == i, c, cls)
        for g in range(NA_HG):
            cols = slice(g * HEAD_DIM, (g + 1) * HEAD_DIM)
            xcols = slice(g * wide, (g + 1) * wide)
            q = q_ref[:, cols]
            s_loc = _qkt(q, k_ref[pl.ds(k0, span), cols]) + b_ref[cls, g]
            s_ctx = _qkt(q, kc_ref[:, cols])
            o = _softmax_pv([s_loc, s_ctx], [vx_ref[pl.ds(k0, span), xcols], vx_ref[seq:, xcols]])
            o_ref[:, cols] = o.astype(o_ref.dtype)

    @pl.when(rb >= n_row_blocks)
    def _():
        for g in range(NA_HG):
            cols = slice(g * HEAD_DIM, (g + 1) * HEAD_DIM)
            xcols = slice(g * wide, (g + 1) * wide)
            s = _qkt(q_ref[:, cols], kc_ref[:, cols])
            o_ref[:, cols] = _softmax_pv([s], [vx_ref[seq:, xcols]]).astype(o_ref.dtype)


def _na_bias_body(rows, band_ref, o_ref):
    kh, uw, sigs, _ = _na_plan(rows)
    neg = jnp.full((GRID_W, GRID_W), NEG_INF, F32)
    for c, (off, band_offs) in enumerate(sigs):
        @pl.when(pl.program_id(0) == c)
        def _():
            for i in range(NA_RB):
                for j in range(uw):
                    inside = 0 <= off + j - i - band_offs[i] < kh
                    tile = band_ref[0, off + j - i + NA_WIN_H - 1] if inside else neg
                    o_ref[0, 0, i * GRID_W:(i + 1) * GRID_W, j * GRID_W:(j + 1) * GRID_W] = tile


def _na_bias_table(rpb, rows):
    kh, uw, sigs, _ = _na_plan(rows)
    heads = rpb.shape[0]
    qcol = jnp.arange(GRID_W)
    col_start = jnp.clip(qcol - NA_WIN_W // 2, 0, GRID_W - NA_WIN_W)
    col_valid = (qcol[None, :] >= col_start[:, None]) & (qcol[None, :] < col_start[:, None] + NA_WIN_W)
    dc_idx = jnp.clip(qcol[None, :] - qcol[:, None], -(NA_WIN_W - 1), NA_WIN_W - 1) + NA_WIN_W - 1
    onehot = (dc_idx[None] == jnp.arange(2 * NA_WIN_W - 1)[:, None, None]).astype(F32)
    band = jnp.einsum('hrc,cqk->hrqk', rpb.astype(F32), onehot, precision=lax.Precision.HIGHEST)
    band = jnp.where(col_valid[None, None], band, NEG_INF)
    n_dr = band.shape[1]
    return pl.pallas_call(
        functools.partial(_na_bias_body, rows),
        grid=(len(sigs), heads),
        in_specs=[pl.BlockSpec((1, n_dr, GRID_W, GRID_W), lambda c, h: (h, 0, 0, 0))],
        out_specs=pl.BlockSpec((1, 1, NA_RB * GRID_W, uw * GRID_W), lambda c, h: (c, h, 0, 0)),
        out_shape=jax.ShapeDtypeStruct((len(sigs), heads, NA_RB * GRID_W, uw * GRID_W), F32),
        compiler_params=_params(("arbitrary", "arbitrary")),
        name="na_bias",
    )(band)


def _na_attention(qkv, bias, n_batch, seq, ctx_len):
    n, d3 = qkv.shape
    d = d3 // 3
    heads = d // HEAD_DIM
    rows = seq // GRID_W
    qb = NA_RB * GRID_W
    assert seq % qb == 0 and ctx_len % qb == 0 and heads % NA_HG == 0
    n_row_blocks = seq // qb
    n_ctx_blocks = ctx_len // qb
    hgs = heads // NA_HG
    w = NA_HG * HEAD_DIM
    lat_blocks = n_batch * n_row_blocks

    def q_map(hg, b, rb):
        blk = jnp.where(rb < n_row_blocks, b * n_row_blocks + rb, lat_blocks + b * n_ctx_blocks + rb - n_row_blocks)
        return (blk, hg)

    ctx_blk0 = (n_batch * seq) // ctx_len
    n_cls, _, bq, bk = bias.shape
    return pl.pallas_call(
        functools.partial(_na_body, n_row_blocks, rows),
        grid=(hgs, n_batch, n_row_blocks + n_ctx_blocks),
        in_specs=[pl.BlockSpec((qb, w), q_map),
                  pl.BlockSpec((seq, w), lambda hg, b, rb: (b, hgs + hg)),
                  pl.BlockSpec((seq, w), lambda hg, b, rb: (b, 2 * hgs + hg)),
                  pl.BlockSpec((ctx_len, w), lambda hg, b, rb: (ctx_blk0 + b, hgs + hg)),
                  pl.BlockSpec((ctx_len, w), lambda hg, b, rb: (ctx_blk0 + b, 2 * hgs + hg)),
                  pl.BlockSpec((n_cls, NA_HG, bq, bk), lambda hg, b, rb: (0, hg, 0, 0),
                               pipeline_mode=pl.Buffered(1))],
        out_specs=pl.BlockSpec((qb, w), q_map),
        out_shape=jax.ShapeDtypeStruct((n, d), BF16),
        scratch_shapes=[pltpu.VMEM((seq + ctx_len, 2 * w), BF16)],
        compiler_params=_params(("arbitrary", "arbitrary", "arbitrary")),
        name="na_attention",
    )(qkv, qkv, qkv, qkv, qkv, bias)


def _rope_tables(seq, n_rows):
    t = jnp.arange(seq, dtype=jnp.int32)
    row = (t // GRID_W).astype(F32)
    col = (t % GRID_W).astype(F32)
    axis_dim = HEAD_DIM // 2
    inv_freq = 1.0 / (ROPE_THETA ** (jnp.arange(0, axis_dim, 2, dtype=F32) / axis_dim))
    ang = jnp.concatenate([row[:, None] * inv_freq, col[:, None] * inv_freq], axis=-1)
    cos = jnp.repeat(jnp.cos(ang), 2, axis=-1)
    sin = jnp.repeat(jnp.sin(ang), 2, axis=-1) * jnp.tile(jnp.array([-1.0, 1.0], F32), HEAD_DIM // 2)
    reps = n_rows // seq
    return jnp.tile(cos, (reps, 1)), jnp.tile(sin, (reps, 1))


def _gqa_body(q_ref, k_ref, kc_ref, v_ref, vc_ref, o_ref, ka_ref, va_ref):
    seq = k_ref.shape[0]

    @pl.when(pl.program_id(2) == 0)
    def _():
        ka_ref[0:seq, :] = k_ref[...]
        ka_ref[seq:, :] = kc_ref[...]
        va_ref[0:seq, 0:HEAD_DIM] = v_ref[...]
        va_ref[seq:, 0:HEAD_DIM] = vc_ref[...]
        va_ref[:, HEAD_DIM:] = jnp.ones((va_ref.shape[0], va_ref.shape[1] - HEAD_DIM), BF16)

    for r in range(GQA_REP):
        cols = slice(r * HEAD_DIM, (r + 1) * HEAD_DIM)
        for r0 in range(0, q_ref.shape[0], GQA_BAND):
            rows = slice(r0, r0 + GQA_BAND)
            s = _qkt(q_ref[rows, cols], ka_ref[...])
            p = jnp.exp(s - s.max(axis=-1, keepdims=True)).astype(BF16)
            ox = jnp.dot(p, va_ref[...], preferred_element_type=F32)
            o_ref[rows, cols] = (ox[:, 0:HEAD_DIM] / ox[:, HEAD_DIM:HEAD_DIM + 1]).astype(o_ref.dtype)


def _gqa_attention(qk, qkv, n_batch, seq, ctx_len, q_cols):
    kv_heads = q_cols // HEAD_DIM // GQA_REP
    assert seq % GQA_QB == 0 and (n_batch * seq) % ctx_len == 0
    w = GQA_REP * HEAD_DIM
    qblocks = seq // GQA_QB
    k_blk0 = q_cols // HEAD_DIM
    v_blk0 = k_blk0 + kv_heads
    ctx_blk0 = (n_batch * seq) // ctx_len
    return pl.pallas_call(
        _gqa_body,
        grid=(n_batch, kv_heads, qblocks),
        in_specs=[pl.BlockSpec((GQA_QB, w), lambda b, h, i: (b * qblocks + i, h)),
                  pl.BlockSpec((seq, HEAD_DIM), lambda b, h, i: (b, k_blk0 + h)),
                  pl.BlockSpec((ctx_len, HEAD_DIM), lambda b, h, i: (ctx_blk0 + b, k_blk0 + h)),
                  pl.BlockSpec((seq, HEAD_DIM), lambda b, h, i: (b, v_blk0 + h)),
                  pl.BlockSpec((ctx_len, HEAD_DIM), lambda b, h, i: (ctx_blk0 + b, v_blk0 + h))],
        out_specs=pl.BlockSpec((GQA_QB, w), lambda b, h, i: (b * qblocks + i, h)),
        out_shape=jax.ShapeDtypeStruct((n_batch * seq, q_cols), BF16),
        scratch_shapes=[pltpu.VMEM((seq + ctx_len, HEAD_DIM), BF16),
                        pltpu.VMEM((seq + ctx_len, 2 * HEAD_DIM), BF16)],
        compiler_params=_params(("arbitrary", "arbitrary", "arbitrary")),
        name="gqa_attention",
    )(qk, qk, qk, qkv, qkv)


def _router_body(n_experts, lg_ref, b_ref, id_ref, wt_ref):
    lt = lg_ref[...].T[:n_experts] + b_ref[...]
    rows = [lt[e:e + 1] for e in range(n_experts)]
    mx = functools.reduce(jnp.maximum, rows)
    ex = [jnp.exp(r - mx) for r in rows]
    den = functools.reduce(jnp.add, ex)
    probs = [e / den for e in ex]

    epg = n_experts // N_GROUPS
    best = None
    for g in range(N_GROUPS):
        p = probs[g * epg:(g + 1) * epg]
        top1 = functools.reduce(jnp.maximum, p)
        i1 = jnp.full(top1.shape, epg, jnp.int32)
        for j in reversed(range(epg)):
            i1 = jnp.where(p[j] == top1, j, i1)
        rest = [jnp.where(i1 == j, -1.0, p[j]) for j in range(epg)]
        top2 = functools.reduce(jnp.maximum, rest)
        i2 = jnp.full(top1.shape, epg, jnp.int32)
        for j in reversed(range(epg)):
            i2 = jnp.where((rest[j] == top2) & (i1 != j), j, i2)
        cand = (top1 + top2, top1, top2, i1 + g * epg, i2 + g * epg)
        if best is None:
            best = cand
        else:
            take = cand[0] > best[0]
            best = tuple(jnp.where(take, c, b) for c, b in zip(cand, best))
    score, top1, top2, e1, e2 = best
    id_ref[...] = jnp.zeros(id_ref.shape, jnp.int32)
    wt_ref[...] = jnp.zeros(wt_ref.shape, F32)
    id_ref[0:1, :] = e1
    id_ref[1:2, :] = e2
    wt_ref[0:1, :] = top1 / score
    wt_ref[1:2, :] = top2 / score


def _router(logits, router_b):
    n = logits.shape[0]
    e = router_b.shape[0]
    tm = 512
    return pl.pallas_call(
        functools.partial(_router_body, e),
        grid=(n // tm,),
        in_specs=[pl.BlockSpec((tm, LANES), lambda i: (i, 0)),
                  pl.BlockSpec((e, 1), lambda i: (0, 0))],
        out_specs=[pl.BlockSpec((8, tm), lambda i: (0, i)), pl.BlockSpec((8, tm), lambda i: (0, i))],
        out_shape=[jax.ShapeDtypeStruct((8, n), jnp.int32), jax.ShapeDtypeStruct((8, n), F32)],
        compiler_params=_params(("arbitrary",)),
        name="router",
    )(logits, router_b.reshape(e, 1).astype(F32))


def _dispatch_plan(ids, n_experts, n_tiles):
    e0, e1 = ids[0], ids[1]
    n = e0.shape[0]
    ar = jnp.arange(n_experts, dtype=jnp.int32)[:, None]
    oh0 = (e0[None, :] == ar).astype(jnp.int32)
    oh1 = (e1[None, :] == ar).astype(jnp.int32)
    sel = oh0 + oh1
    csum = jnp.cumsum(sel, axis=1)
    counts = csum[:, -1]
    padded = ((counts + MOE_TM - 1) // MOE_TM) * MOE_TM
    ends = jnp.cumsum(padded)
    offs = ends - padded
    slot = offs[:, None] + csum - 1
    pos0 = jnp.sum(oh0 * slot, axis=0)
    pos1 = jnp.sum(oh1 * slot, axis=0)
    tok = jnp.arange(n, dtype=jnp.int32)
    src = jnp.zeros((n_tiles * MOE_TM,), jnp.int32)
    src = src.at[jnp.concatenate([pos0, pos1])].set(jnp.concatenate([tok, tok]), unique_indices=True)
    n_used = (ends[-1] // MOE_TM).astype(jnp.int32)
    tile_start = jnp.minimum(jnp.arange(n_tiles, dtype=jnp.int32), n_used - 1) * MOE_TM
    nonempty = (counts > 0).astype(jnp.int32)
    tile_seg = jnp.sum((ends[None, :] <= tile_start[:, None]).astype(jnp.int32) * nonempty[None, :], axis=1)
    seg_of_expert = jnp.cumsum(nonempty) - 1
    seg_expert = jnp.sum(jnp.where((seg_of_expert[None, :] == ar) & (nonempty[None, :] > 0), ar.T, 0), axis=1)
    meta = jnp.stack([n_used, jnp.sum(nonempty)]).astype(jnp.int32)
    return src, jnp.stack([pos0, pos1]), (tile_seg.astype(jnp.int32), seg_expert.astype(jnp.int32), meta)


def _row_copy(src_hbm, row, dst, k, sem):
    return pltpu.make_async_copy(src_hbm.at[pl.ds(row, 1), :], dst.at[pl.ds(k, 1), :], sem)


def _token_copy(src_hbm, tok, tile, g, r, sem):
    return pltpu.make_async_copy(src_hbm.at[tok], tile.at[g, :, r, :], sem)


def _issue_token_gather(idx_ref, first, h_hbm, tile, sem):
    def one(g, carry):
        for r in range(SUBLANES):
            _token_copy(h_hbm, idx_ref[first + g * SUBLANES + r], tile, g, r, sem).start(priority=r % 2)
        return carry
    lax.fori_loop(0, tile.shape[0], one, 0)


def _wait_token_gather(h_hbm, tile, sem):
    def one(g, carry):
        for r in range(SUBLANES):
            _token_copy(h_hbm, 0, tile, g, r, sem).wait()
        return carry
    lax.fori_loop(0, tile.shape[0], one, 0)


def _stream_expert_weights(seg_ref, sege_ref, meta_ref, w_hbms, chunk, wf, wb, sem, priority):
    j, t = pl.program_id(0), pl.program_id(1)
    n_j = pl.num_programs(0)
    n_seg = meta_ref[1]
    k = seg_ref[t]
    first = (t == 0) | (k != seg_ref[jnp.maximum(t - 1, 0)])

    def copies(jj, kk):
        e = sege_ref[kk]
        return [pltpu.make_async_copy(w.at[e, :, pl.ds(jj * chunk, chunk)], wf.at[m], sem)
                for m, w in enumerate(w_hbms)]

    @pl.when(first)
    def _():
        @pl.when((j == 0) & (t == 0))
        def _():
            for c in copies(j, k):
                c.start(priority=priority)

        for c in copies(j, k):
            c.wait()
        for m in range(len(w_hbms)):
            _convert_rows(wb.at[m], wf.at[m])

        wrap = k + 1 == n_seg
        nj = jnp.where(wrap, j + 1, j)
        nk = jnp.where(wrap, 0, k + 1)

        @pl.when(nj < n_j)
        def _():
            for c in copies(nj, nk):
                c.start(priority=priority)


def _moe_up_body(seg_ref, sege_ref, meta_ref, src_ref, h_hbm, wg_hbm, wu_hbm, o_ref, xbuf, xb, gsem, wf, wb, wsem):
    j, t = pl.program_id(0), pl.program_id(1)
    n_j, n_t = pl.num_programs(0), pl.num_programs(1)
    n_used = meta_ref[0]
    slot = jnp.bitwise_and(j * n_t + t, 1)

    @pl.when((j == 0) & (t == 0))
    def _():
        _issue_token_gather(src_ref, 0, h_hbm, xbuf.at[0], gsem.at[0])

    nt = jnp.where(t + 1 < n_t, t + 1, 0)

    @pl.when(((t + 1 < n_t) | (j + 1 < n_j)) & (nt < n_used))
    def _():
        _issue_token_gather(src_ref, nt * MOE_TM, h_hbm, xbuf.at[1 - slot], gsem.at[1 - slot])

    _stream_expert_weights(seg_ref, sege_ref, meta_ref, (wg_hbm, wu_hbm), o_ref.shape[1], wf, wb, wsem, 0)

    @pl.when(t < n_used)
    def _():
        _wait_token_gather(h_hbm, xbuf.at[slot], gsem.at[slot])
        for s in range(xbuf.shape[2]):
            xb[:, s * LANES:(s + 1) * LANES] = xbuf[slot, :, s].reshape(MOE_TM, LANES).astype(xb.dtype)
        x = xb[...]
        g = jnp.dot(x, wb[0], preferred_element_type=F32)
        u = jnp.dot(x, wb[1], preferred_element_type=F32)
        o_ref[...] = (g * jax.nn.sigmoid(g) * u).astype(o_ref.dtype)

    @pl.when(t >= n_used)
    def _():
        o_ref[...] = jnp.zeros(o_ref.shape, o_ref.dtype)


def _moe_up(h_rows, src, w_gate, w_up, plan):
    _, pieces, _ = h_rows.shape
    d = pieces * LANES
    r = src.shape[0]
    f = w_gate.shape[2]
    n_tiles = r // MOE_TM
    fc = MOE_FC if f % MOE_FC == 0 else f
    any_spec = pl.BlockSpec(memory_space=pl.ANY)
    return pl.pallas_call(
        _moe_up_body,
        grid_spec=pltpu.PrefetchScalarGridSpec(
            num_scalar_prefetch=4,
            grid=(f // fc, n_tiles),
            in_specs=[any_spec, any_spec, any_spec],
            out_specs=pl.BlockSpec((MOE_TM, fc), lambda j, t, sg, se, mt, sr: (t, j)),
            scratch_shapes=[pltpu.VMEM((2, MOE_TM // SUBLANES, pieces, SUBLANES, LANES), F32),
                            pltpu.VMEM((MOE_TM, d), BF16), pltpu.SemaphoreType.DMA((2,)),
                            pltpu.VMEM((2, d, fc), F32), pltpu.VMEM((2, d, fc), BF16),
                            pltpu.SemaphoreType.DMA(())]),
        out_shape=jax.ShapeDtypeStruct((r, f), BF16),
        compiler_params=_params(("arbitrary", "arbitrary")),
        name="moe_up",
    )(*plan, src, h_rows, w_gate, w_up)


def _moe_down_body(seg_ref, sege_ref, meta_ref, h_ref, w_hbm, o_ref, wf, wb, sem):
    _stream_expert_weights(seg_ref, sege_ref, meta_ref, (w_hbm,), o_ref.shape[1], wf, wb, sem, 1)
    t = pl.program_id(1)

    @pl.when(t < meta_ref[0])
    def _():
        o_ref[...] = jnp.dot(h_ref[...], wb[0], preferred_element_type=F32)

    @pl.when(t >= meta_ref[0])
    def _():
        o_ref[...] = jnp.zeros(o_ref.shape, o_ref.dtype)


def _moe_down(hs, w_down, plan):
    r, f = hs.shape
    d = w_down.shape[2]
    nc = min(MOE_NC, d)
    n_tiles = r // MOE_TM
    return pl.pallas_call(
        _moe_down_body,
        grid_spec=pltpu.PrefetchScalarGridSpec(
            num_scalar_prefetch=3,
            grid=(d // nc, n_tiles),
            in_specs=[pl.BlockSpec((MOE_TM, f), lambda j, t, sg, se, mt: (jnp.minimum(t, mt[0] - 1), 0)),
                      pl.BlockSpec(memory_space=pl.ANY)],
            out_specs=pl.BlockSpec((MOE_TM, nc), lambda j, t, sg, se, mt: (t, j)),
            scratch_shapes=[pltpu.VMEM((1, f, nc), F32), pltpu.VMEM((1, f, nc), BF16),
                            pltpu.SemaphoreType.DMA(())]),
        out_shape=jax.ShapeDtypeStruct((r, d), F32),
        compiler_params=_params(("arbitrary", "arbitrary")),
        name="moe_down",
    )(*plan, hs, w_down)


def _combine_body(has_next, pos_ref, y_hbm, wt_ref, x_ref, g_ref, gate_ref, *rest):
    if has_next:
        g2_ref, sh_ref, sc_ref, xo_ref, ho_ref, buf, sem = rest
    else:
        xo_ref, buf, sem = rest
    i = pl.program_id(0)
    n = pl.num_programs(0)
    n_tok = n * ROW_TILE

    def issue(step, slot):
        def one(k, carry):
            tok = step * ROW_TILE + k
            _row_copy(y_hbm, pos_ref[tok], buf.at[slot, 0], k, sem.at[slot]).start()
            _row_copy(y_hbm, pos_ref[n_tok + tok], buf.at[slot, 1], k, sem.at[slot]).start()
            return carry
        lax.fori_loop(0, ROW_TILE, one, 0, unroll=8)

    @pl.when(i == 0)
    def _():
        issue(0, 0)

    @pl.when(i + 1 < n)
    def _():
        issue(i + 1, (i + 1) % 2)

    slot = i % 2

    def wait_one(k, carry):
        _row_copy(y_hbm, 0, buf.at[slot, 0], k, sem.at[slot]).wait()
        _row_copy(y_hbm, 0, buf.at[slot, 1], k, sem.at[slot]).wait()
        return carry
    lax.fori_loop(0, ROW_TILE, wait_one, 0, unroll=8)

    wt = wt_ref[...]
    y = wt[:, 0:1] * buf[slot, 0] + wt[:, 1:2] * buf[slot, 1]
    xn = x_ref[...] + gate_ref[0] * _rms(y, g_ref[...])
    xo_ref[...] = xn
    if has_next:
        h = _rms(xn, g2_ref[...])
        ho_ref[...] = (h * (1.0 + sc_ref[0]) + sh_ref[0]).astype(ho_ref.dtype)


def _combine(ys, pos, wts, x, gain, mod, gate_chunk, n_lat, seq, nxt=None):
    n = wts.shape[0]
    d = ys.shape[1]
    n_batch = mod.shape[0] - 1
    n_lat_tiles = min(n_lat, n) // ROW_TILE

    def mk(chunk):
        def index_map(i, pos_ref):
            return (jnp.where(i < n_lat_tiles, i // (seq // ROW_TILE), n_batch), 0, chunk)
        return index_map

    row = pl.BlockSpec((ROW_TILE, d), lambda i, p: (i, 0))
    vec = pl.BlockSpec((1, d), lambda i, p: (0, 0))
    in_specs = [pl.BlockSpec(memory_space=pl.ANY),
                pl.BlockSpec((ROW_TILE, 2), lambda i, p: (i, 0)),
                row, vec, pl.BlockSpec((1, 1, d), mk(gate_chunk))]
    args = [ys, wts, x, gain.reshape(1, d), mod]
    out_specs = [row]
    out_shape = [jax.ShapeDtypeStruct((n, d), F32)]
    if nxt is not None:
        gain2, mod2, shift_chunk, scale_chunk, h_dtype = nxt
        in_specs += [vec, pl.BlockSpec((1, 1, d), mk(shift_chunk)), pl.BlockSpec((1, 1, d), mk(scale_chunk))]
        args += [gain2.reshape(1, d), mod2, mod2]
        out_specs.append(row)
        out_shape.append(jax.ShapeDtypeStruct((n, d), h_dtype))
    out = pl.pallas_call(
        functools.partial(_combine_body, nxt is not None),
        grid_spec=pltpu.PrefetchScalarGridSpec(
            num_scalar_prefetch=1,
            grid=(n // ROW_TILE,),
            in_specs=in_specs,
            out_specs=out_specs,
            scratch_shapes=[pltpu.VMEM((2, 2, ROW_TILE, d), F32), pltpu.SemaphoreType.DMA((2,))]),
        out_shape=out_shape,
        compiler_params=_params(("arbitrary",)),
        name="moe_combine",
    )(pos.reshape(-1), *args)
    return out if nxt is not None else out[0]


def _moe(h_rows, logits, router_b, w_gate, w_up, w_down):
    n = logits.shape[0]
    n_experts = router_b.shape[0]
    n_tiles = (TOP_K * n) // MOE_TM + n_experts
    ids, wts = _router(logits, router_b)
    src, pos, plan = _dispatch_plan(ids, n_experts, n_tiles)
    hs = _moe_up(h_rows, src, w_gate, w_up, plan)
    ys = _moe_down(hs, w_down, plan)
    return ys, pos, wts[:2].T


def kernel(x, c, ctx, c_ctx, router_w, router_b, l0_ada_w, l0_ada_b, l0_norm_pre_mix, l0_norm_post_mix, l0_norm_pre_ffn, l0_norm_post_ffn, l0_na_w_qkv, l0_na_rpb, l0_na_w_o, l0_moe_w_gate, l0_moe_w_up, l0_moe_w_down, l1_ada_w, l1_ada_b, l1_norm_pre_mix, l1_norm_post_mix, l1_norm_pre_ffn, l1_norm_post_ffn, l1_gqa_w_qkv, l1_gqa_q_gain, l1_gqa_k_gain, l1_gqa_w_o, l1_moe_w_gate, l1_moe_w_up, l1_moe_w_down):
    n_batch, seq, d = x.shape
    ctx_len = ctx.shape[1]
    n_lat = n_batch * seq
    n_all = n_lat + n_batch * ctx_len
    scale = HEAD_DIM ** -0.5
    SH_M, SC_M, G_M, SH_F, SC_F, G_F = range(6)

    cvec = jnp.zeros((8, d), F32).at[:n_batch].set(c).at[n_batch].set(c_ctx)
    mod0 = _adaln(cvec, l0_ada_w, l0_ada_b)[:n_batch + 1].reshape(n_batch + 1, 1, 6 * d)
    mod1 = _adaln(cvec, l1_ada_w, l1_ada_b)[:n_batch + 1].reshape(n_batch + 1, 1, 6 * d)

    x_lat = x.reshape(n_lat, d)
    x_ctx = ctx.reshape(n_batch * ctx_len, d)

    h = _norm_mod(x_lat, x_ctx, l0_norm_pre_mix, mod0, SH_M, SC_M, seq, BF16)
    qscale = jnp.concatenate([jnp.full((1, d), scale, F32), jnp.ones((1, 2 * d), F32)], axis=1)
    qkv = _matmul(h, l0_na_w_qkv, qscale, BF16)
    bias = _na_bias_table(l0_na_rpb, seq // GRID_W)
    att = _na_attention(qkv, bias, n_batch, seq, ctx_len)
    y = _matmul(att, l0_na_w_o, jnp.ones((1, d), F32), F32)
    xa, h_rows, logits = _residual(x_lat, x_ctx, y, l0_norm_post_mix, mod0, G_M, seq,
                                   l0_norm_pre_ffn, SH_F, SC_F, router_w)
    ys, pos, wts = _moe(h_rows, logits, router_b, l0_moe_w_gate, l0_moe_w_up, l0_moe_w_down)
    xa, h = _combine(ys, pos, wts, xa, l0_norm_post_ffn, mod0, G_F, n_lat, seq,
                     nxt=(l1_norm_pre_mix, mod1, SH_M, SC_M, BF16))

    q_cols = d
    kv_cols = d // GQA_REP
    cos, sin = _rope_tables(seq, n_lat)
    ident = (jnp.ones((n_all - n_lat, HEAD_DIM), F32), jnp.zeros((n_all - n_lat, HEAD_DIM), F32))
    cos = jnp.concatenate([cos, ident[0]], axis=0)
    sin = jnp.concatenate([sin, ident[1]], axis=0)
    gain_post = jnp.stack([jnp.stack([l1_gqa_q_gain.astype(F32), jnp.full((HEAD_DIM,), scale, F32)]),
                           jnp.stack([l1_gqa_k_gain.astype(F32), jnp.ones((HEAD_DIM,), F32)])])
    qkv = _qkv_gqa(h, l1_gqa_w_qkv, cos, sin, gain_post, q_cols, kv_cols)
    att = _gqa_attention(qkv, qkv, n_batch, seq, ctx_len, q_cols)
    y = _matmul(att, l1_gqa_w_o, jnp.ones((1, d), F32), F32)
    xl, h_rows, logits = _residual(xa, xa, y, l1_norm_post_mix, mod1, G_M, seq,
                                   l1_norm_pre_ffn, SH_F, SC_F, router_w)
    ys, pos, wts = _moe(h_rows, logits, router_b, l1_moe_w_gate, l1_moe_w_up, l1_moe_w_down)
    xl = _combine(ys, pos, wts, xl, l1_norm_post_ffn, mod1, G_F, n_lat, seq)
    return xl.reshape(n_batch, seq, d)
```

```python
import functools

import jax
import jax.numpy as jnp
from jax import lax
from jax.experimental import pallas as pl
from jax.experimental.pallas import tpu as pltpu

LANES = 128
SUBLANES = 8
BF16_SUBLANES = 16
GRID_W = 64
HEAD_DIM = 128
NA_WIN_H = 8
NA_WIN_W = 16
GQA_REP = 4
ROPE_THETA = 10000.0
N_GROUPS = 4
TOP_K = 2
EPS = 1e-6
NEG_INF = -1e30

F32 = jnp.float32
BF16 = jnp.bfloat16

ROW_TILE = 256
MM_TM = 1024
MM_TN = 512
MOE_TM = 256
MOE_FC = 768
MOE_NC = 4096
NA_RB = 4
NA_HG = 8
GQA_QB = 1024
GQA_BAND = 128
ADA_TN = 512
VMEM_LIMIT = 56 * 1024 * 1024


def _params(sem):
    return pltpu.CompilerParams(dimension_semantics=sem, vmem_limit_bytes=VMEM_LIMIT)


def _adaln_body(c_ref, w_ref, b_ref, o_ref):
    c = c_ref[...]
    s = (c * jax.nn.sigmoid(c)).astype(BF16)
    o_ref[...] = jnp.dot(s, w_ref[...].astype(BF16), preferred_element_type=F32) + b_ref[...]


def _adaln(cvec, ada_w, ada_b):
    m, d = cvec.shape
    n = ada_w.shape[1]
    return pl.pallas_call(
        _adaln_body,
        grid=(n // ADA_TN,),
        in_specs=[pl.BlockSpec((m, d), lambda j: (0, 0)),
                  pl.BlockSpec((d, ADA_TN), lambda j: (0, j)),
                  pl.BlockSpec((1, ADA_TN), lambda j: (0, j))],
        out_specs=pl.BlockSpec((m, ADA_TN), lambda j: (0, j)),
        out_shape=jax.ShapeDtypeStruct((m, n), F32),
        compiler_params=_params(("arbitrary",)),
        name="adaln",
    )(cvec, ada_w, ada_b.reshape(1, n))


def _convert_rows(dst_ref, src_ref):
    rows = src_ref.shape[0]
    band = next(b for b in (256, 128, 64, 32, 16) if rows % b == 0)

    def one(i, carry):
        r0 = pl.multiple_of(i * band, band)
        dst_ref[pl.ds(r0, band), :] = src_ref[pl.ds(r0, band), :].astype(dst_ref.dtype)
        return carry
    lax.fori_loop(0, rows // band, one, 0)


def _rms(x, gain):
    return x * lax.rsqrt(jnp.mean(x * x, axis=-1, keepdims=True) + EPS) * gain


def _stream_rows(n_lat_tiles, xl_ref, xc_ref):
    return jnp.where(pl.program_id(0) < n_lat_tiles, xl_ref[...], xc_ref[...])


def _stream_specs(n_lat_tiles, d):
    return [pl.BlockSpec((ROW_TILE, d), lambda i: (jnp.minimum(i, n_lat_tiles - 1), 0)),
            pl.BlockSpec((ROW_TILE, d), lambda i: (jnp.maximum(i - n_lat_tiles, 0), 0))]


def _norm_mod_body(n_lat_tiles, xl_ref, xc_ref, g_ref, sh_ref, sc_ref, o_ref):
    y = _rms(_stream_rows(n_lat_tiles, xl_ref, xc_ref), g_ref[...])
    o_ref[...] = (y * (1.0 + sc_ref[0]) + sh_ref[0]).astype(o_ref.dtype)


def _mod_row_map(n_lat_tiles, tiles_per_batch, n_batch, chunk):
    def index_map(i):
        return (jnp.where(i < n_lat_tiles, i // tiles_per_batch, n_batch), 0, chunk)
    return index_map


def _norm_mod(x_lat, x_ctx, gain, mod, shift_chunk, scale_chunk, seq, out_dtype):
    n_lat, d = x_lat.shape
    n = n_lat + x_ctx.shape[0]
    n_batch = mod.shape[0] - 1
    n_lat_tiles = n_lat // ROW_TILE
    mk = functools.partial(_mod_row_map, n_lat_tiles, seq // ROW_TILE, n_batch)
    return pl.pallas_call(
        functools.partial(_norm_mod_body, n_lat_tiles),
        grid=(n // ROW_TILE,),
        in_specs=_stream_specs(n_lat_tiles, d) + [
            pl.BlockSpec((1, d), lambda i: (0, 0)),
            pl.BlockSpec((1, 1, d), mk(shift_chunk)),
            pl.BlockSpec((1, 1, d), mk(scale_chunk))],
        out_specs=pl.BlockSpec((ROW_TILE, d), lambda i: (i, 0)),
        out_shape=jax.ShapeDtypeStruct((n, d), out_dtype),
        compiler_params=_params(("arbitrary",)),
        name="norm_mod",
    )(x_lat, x_ctx, gain.reshape(1, d), mod, mod)


def _split_bf16(x):
    hi = x.astype(BF16)
    lo = (x - hi.astype(F32)).astype(BF16)
    return hi, lo


def _residual_body(n_lat_tiles, xl_ref, xc_ref, y_ref, g_ref, gate_ref, g2_ref, sh_ref, sc_ref, rw_ref,
                   xo_ref, hr_ref, lg_ref):
    xn = _stream_rows(n_lat_tiles, xl_ref, xc_ref) + gate_ref[0] * _rms(y_ref[...], g_ref[...])
    xo_ref[...] = xn
    h = _rms(xn, g2_ref[...]) * (1.0 + sc_ref[0]) + sh_ref[0]
    hr_ref[...] = h.reshape(hr_ref.shape)
    h_hi, h_lo = _split_bf16(h)
    w_hi, w_lo = _split_bf16(rw_ref[...])
    lg_ref[...] = (jnp.dot(h_hi, w_hi, preferred_element_type=F32)
                   + jnp.dot(h_lo, w_hi, preferred_element_type=F32)
                   + jnp.dot(h_hi, w_lo, preferred_element_type=F32))


def _residual(x_lat, x_ctx, y, gain, mod, gate_chunk, seq, gain2, shift_chunk, scale_chunk, router_w):
    n, d = y.shape
    n_batch = mod.shape[0] - 1
    pieces = d // LANES
    n_lat_tiles = min(x_lat.shape[0], n) // ROW_TILE
    mk = functools.partial(_mod_row_map, n_lat_tiles, seq // ROW_TILE, n_batch)
    row = pl.BlockSpec((ROW_TILE, d), lambda i: (i, 0))
    vec = pl.BlockSpec((1, d), lambda i: (0, 0))
    e = router_w.shape[1]
    w_pad = jnp.pad(router_w, ((0, 0), (0, LANES - e)))
    return pl.pallas_call(
        functools.partial(_residual_body, n_lat_tiles),
        grid=(n // ROW_TILE,),
        in_specs=_stream_specs(n_lat_tiles, d) + [
            row, vec, pl.BlockSpec((1, 1, d), mk(gate_chunk)), vec,
            pl.BlockSpec((1, 1, d), mk(shift_chunk)), pl.BlockSpec((1, 1, d), mk(scale_chunk)),
            pl.BlockSpec((d, LANES), lambda i: (0, 0))],
        out_specs=[row, pl.BlockSpec((ROW_TILE, pieces, LANES), lambda i: (i, 0, 0)),
                   pl.BlockSpec((ROW_TILE, LANES), lambda i: (i, 0))],
        out_shape=[jax.ShapeDtypeStruct((n, d), F32), jax.ShapeDtypeStruct((n, pieces, LANES), F32),
                   jax.ShapeDtypeStruct((n, LANES), F32)],
        compiler_params=_params(("arbitrary",)),
        name="residual",
    )(x_lat, x_ctx, y, gain.reshape(1, d), mod, gain2.reshape(1, d), mod, mod, w_pad)


def _matmul_body(a_ref, w_ref, s_ref, o_ref, wb_ref):
    @pl.when(pl.program_id(1) == 0)
    def _():
        _convert_rows(wb_ref, w_ref)

    acc = jnp.dot(a_ref[...], wb_ref[...], preferred_element_type=F32)
    o_ref[...] = (acc * s_ref[...]).astype(o_ref.dtype)


def _row_tile(m):
    return next(t for t in (MM_TM, MM_TM // 2, MM_TM // 4) if m % t == 0)


def _matmul(a, w, col_scale, out_dtype):
    m, k = a.shape
    n = w.shape[1]
    tm = _row_tile(m)
    return pl.pallas_call(
        _matmul_body,
        grid=(n // MM_TN, m // tm),
        in_specs=[pl.BlockSpec((tm, k), lambda j, i: (i, 0)),
                  pl.BlockSpec((k, MM_TN), lambda j, i: (0, j)),
                  pl.BlockSpec((1, MM_TN), lambda j, i: (0, j))],
        out_specs=pl.BlockSpec((tm, MM_TN), lambda j, i: (i, j)),
        out_shape=jax.ShapeDtypeStruct((m, n), out_dtype),
        scratch_shapes=[pltpu.VMEM((k, MM_TN), BF16)],
        compiler_params=_params(("arbitrary", "arbitrary")),
        name="matmul",
    )(a, w, col_scale)


def _rope_head(y, cos, sin, even):
    partner = jnp.where(even, pltpu.roll(y, HEAD_DIM - 1, 1), pltpu.roll(y, 1, 1))
    return y * cos + partner * sin


def _qkv_gqa_body(n_qk_tiles, a_ref, w_ref, cos_ref, sin_ref, gp_ref, o_ref, wb_ref):
    j = pl.program_id(0)

    @pl.when(pl.program_id(1) == 0)
    def _():
        _convert_rows(wb_ref, w_ref)

    @pl.when(j < n_qk_tiles)
    def _():
        gain = gp_ref[0, 0:1, :]
        post = gp_ref[0, 1:2, :]
        band = min(ROW_TILE, a_ref.shape[0])
        even = (lax.broadcasted_iota(jnp.int32, (band, HEAD_DIM), 1) % 2) == 0
        for r0 in range(0, a_ref.shape[0], band):
            rows = slice(r0, r0 + band)
            acc = jnp.dot(a_ref[rows, :], wb_ref[...], preferred_element_type=F32)
            cos = cos_ref[rows, :]
            sin = sin_ref[rows, :]
            for h in range(acc.shape[1] // HEAD_DIM):
                cols = slice(h * HEAD_DIM, (h + 1) * HEAD_DIM)
                y = _rope_head(_rms(acc[:, cols], gain), cos, sin, even)
                o_ref[rows, cols] = (y * post).astype(o_ref.dtype)

    @pl.when(j >= n_qk_tiles)
    def _():
        o_ref[...] = jnp.dot(a_ref[...], wb_ref[...], preferred_element_type=F32).astype(o_ref.dtype)


def _qkv_gqa(a, w, cos, sin, gain_post, q_cols, kv_cols):
    m, k = a.shape
    n = w.shape[1]
    tm = _row_tile(m)
    tn = min(MM_TN, kv_cols)
    assert q_cols % tn == 0 and kv_cols % tn == 0
    n_q_tiles = q_cols // tn
    n_qk_tiles = (q_cols + kv_cols) // tn
    return pl.pallas_call(
        functools.partial(_qkv_gqa_body, n_qk_tiles),
        grid=(n // tn, m // tm),
        in_specs=[pl.BlockSpec((tm, k), lambda j, i: (i, 0)),
                  pl.BlockSpec((k, tn), lambda j, i: (0, j)),
                  pl.BlockSpec((tm, HEAD_DIM), lambda j, i: (i, 0)),
                  pl.BlockSpec((tm, HEAD_DIM), lambda j, i: (i, 0)),
                  pl.BlockSpec((1, 2, HEAD_DIM), lambda j, i: (jnp.where(j < n_q_tiles, 0, 1), 0, 0))],
        out_specs=pl.BlockSpec((tm, tn), lambda j, i: (i, j)),
        out_shape=jax.ShapeDtypeStruct((m, n), BF16),
        scratch_shapes=[pltpu.VMEM((k, tn), BF16)],
        compiler_params=_params(("arbitrary", "arbitrary")),
        name="qkv_gqa",
    )(a, w, cos, sin, gain_post)


def _softmax_pv(s_parts, vx_parts):
    m = s_parts[0].max(axis=-1, keepdims=True)
    for s in s_parts[1:]:
        m = jnp.maximum(m, s.max(axis=-1, keepdims=True))
    acc = None
    for s, vx in zip(s_parts, vx_parts):
        o = jnp.dot(jnp.exp(s - m).astype(BF16), vx, preferred_element_type=F32)
        acc = o if acc is None else acc + o
    return acc[:, 0:HEAD_DIM] / acc[:, HEAD_DIM:HEAD_DIM + 1]


def _qkt(q, k):
    return lax.dot_general(q, k, (((1,), (1,)), ((), ())), preferred_element_type=F32)


def _na_plan(rows):
    kh = min(NA_WIN_H, rows)
    uw = NA_RB + kh - 1
    uw += uw % 2
    assert rows >= uw and rows % NA_RB == 0
    sigs, cls_of_rb = [], []
    for rb in range(rows // NA_RB):
        rs_blk = min(max(rb * NA_RB - kh // 2, 0), rows - uw)
        offs = []
        for r in range(rb * NA_RB, (rb + 1) * NA_RB):
            rs = min(max(r - kh // 2, 0), rows - kh)
            assert rs_blk <= rs and rs + kh <= rs_blk + uw
            offs.append(rs - r)
        sig = (rs_blk - rb * NA_RB, tuple(offs))
        if sig not in sigs:
            sigs.append(sig)
        cls_of_rb.append(sigs.index(sig))
    return kh, uw, sigs, cls_of_rb


def _na_body(n_row_blocks, rows, q_ref, k_ref, v_ref, kc_ref, vc_ref, b_ref, o_ref, vx_ref):
    rb = pl.program_id(2)
    kh, uw, _, cls_of_rb = _na_plan(rows)
    span = uw * GRID_W
    seq = v_ref.shape[0]
    wide = 2 * HEAD_DIM

    @pl.when(rb == 0)
    def _():
        vx_ref[...] = jnp.ones(vx_ref.shape, vx_ref.dtype)
        for g in range(NA_HG):
            vx_ref[0:seq, g * wide:g * wide + HEAD_DIM] = v_ref[:, g * HEAD_DIM:(g + 1) * HEAD_DIM]
            vx_ref[seq:, g * wide:g * wide + HEAD_DIM] = vc_ref[:, g * HEAD_DIM:(g + 1) * HEAD_DIM]

    @pl.when(rb < n_row_blocks)
    def _():
        rs = jnp.clip(rb * NA_RB - kh // 2, 0, rows - uw)
        k0 = pl.multiple_of(rs * GRID_W, GRID_W)
        cls = jnp.int32(0)
        for i, c in enumerate(cls_of_rb):
            cls = jnp.where(rb == i, c, cls)
        for g in range(NA_HG):
            cols = slice(g * HEAD_DIM, (g + 1) * HEAD_DIM)
            xcols = slice(g * wide, (g + 1) * wide)
            q = q_ref[:, cols]
            s_loc = _qkt(q, k_ref[pl.ds(k0, span), cols]) + b_ref[cls, g]
            s_ctx = _qkt(q, kc_ref[:, cols])
            o = _softmax_pv([s_loc, s_ctx], [vx_ref[pl.ds(k0, span), xcols], vx_ref[seq:, xcols]])
            o_ref[:, cols] = o.astype(o_ref.dtype)

    @pl.when(rb >= n_row_blocks)
    def _():
        for g in range(NA_HG):
            cols = slice(g * HEAD_DIM, (g + 1) * HEAD_DIM)
            xcols = slice(g * wide, (g + 1) * wide)
            s = _qkt(q_ref[:, cols], kc_ref[:, cols])
            o_ref[:, cols] = _softmax_pv([s], [vx_ref[seq:, xcols]]).astype(o_ref.dtype)


def _na_bias_body(rows, band_ref, o_ref):
    kh, uw, sigs, _ = _na_plan(rows)
    neg = jnp.full((GRID_W, GRID_W), NEG_INF, F32)
    for c, (off, band_offs) in enumerate(sigs):
        @pl.when(pl.program_id(0) == c)
        def _():
            for i in range(NA_RB):
                for j in range(uw):
                    inside = 0 <= off + j - i - band_offs[i] < kh
                    tile = band_ref[0, off + j - i + NA_WIN_H - 1] if inside else neg
                    o_ref[0, 0, i * GRID_W:(i + 1) * GRID_W, j * GRID_W:(j + 1) * GRID_W] = tile


def _na_bias_table(rpb, rows):
    kh, uw, sigs, _ = _na_plan(rows)
    heads = rpb.shape[0]
    qcol = jnp.arange(GRID_W)
    col_start = jnp.clip(qcol - NA_WIN_W // 2, 0, GRID_W - NA_WIN_W)
    col_valid = (qcol[None, :] >= col_start[:, None]) & (qcol[None, :] < col_start[:, None] + NA_WIN_W)
    dc_idx = jnp.clip(qcol[None, :] - qcol[:, None], -(NA_WIN_W - 1), NA_WIN_W - 1) + NA_WIN_W - 1
    onehot = (dc_idx[None] == jnp.arange(2 * NA_WIN_W - 1)[:, None, None]).astype(F32)
    band = jnp.einsum('hrc,cqk->hrqk', rpb.astype(F32), onehot, precision=lax.Precision.HIGHEST)
    band = jnp.where(col_valid[None, None], band, NEG_INF)
    n_dr = band.shape[1]
    return pl.pallas_call(
        functools.partial(_na_bias_body, rows),
        grid=(len(sigs), heads),
        in_specs=[pl.BlockSpec((1, n_dr, GRID_W, GRID_W), lambda c, h: (h, 0, 0, 0))],
        out_specs=pl.BlockSpec((1, 1, NA_RB * GRID_W, uw * GRID_W), lambda c, h: (c, h, 0, 0)),
        out_shape=jax.ShapeDtypeStruct((len(sigs), heads, NA_RB * GRID_W, uw * GRID_W), F32),
        compiler_params=_params(("arbitrary", "arbitrary")),
        name="na_bias",
    )(band)


def _na_attention(qkv, bias, n_batch, seq, ctx_len):
    n, d3 = qkv.shape
    d = d3 // 3
    heads = d // HEAD_DIM
    rows = seq // GRID_W
    qb = NA_RB * GRID_W
    assert seq % qb == 0 and ctx_len % qb == 0 and heads % NA_HG == 0
    n_row_blocks = seq // qb
    n_ctx_blocks = ctx_len // qb
    hgs = heads // NA_HG
    w = NA_HG * HEAD_DIM
    lat_blocks = n_batch * n_row_blocks

    def q_map(hg, b, rb):
        blk = jnp.where(rb < n_row_blocks, b * n_row_blocks + rb, lat_blocks + b * n_ctx_blocks + rb - n_row_blocks)
        return (blk, hg)

    ctx_blk0 = (n_batch * seq) // ctx_len
    n_cls, _, bq, bk = bias.shape
    return pl.pallas_call(
        functools.partial(_na_body, n_row_blocks, rows),
        grid=(hgs, n_batch, n_row_blocks + n_ctx_blocks),
        in_specs=[pl.BlockSpec((qb, w), q_map),
                  pl.BlockSpec((seq, w), lambda hg, b, rb: (b, hgs + hg)),
                  pl.BlockSpec((seq, w), lambda hg, b, rb: (b, 2 * hgs + hg)),
                  pl.BlockSpec((ctx_len, w), lambda hg, b, rb: (ctx_blk0 + b, hgs + hg)),
                  pl.BlockSpec((ctx_len, w), lambda hg, b, rb: (ctx_blk0 + b, 2 * hgs + hg)),
                  pl.BlockSpec((n_cls, NA_HG, bq, bk), lambda hg, b, rb: (0, hg, 0, 0),
                               pipeline_mode=pl.Buffered(1))],
        out_specs=pl.BlockSpec((qb, w), q_map),
        out_shape=jax.ShapeDtypeStruct((n, d), BF16),
        scratch_shapes=[pltpu.VMEM((seq + ctx_len, 2 * w), BF16)],
        compiler_params=_params(("arbitrary", "arbitrary", "arbitrary")),
        name="na_attention",
    )(qkv, qkv, qkv, qkv, qkv, bias)


def _rope_tables(seq, n_rows):
    t = jnp.arange(seq, dtype=jnp.int32)
    row = (t // GRID_W).astype(F32)
    col = (t % GRID_W).astype(F32)
    axis_dim = HEAD_DIM // 2
    inv_freq = 1.0 / (ROPE_THETA ** (jnp.arange(0, axis_dim, 2, dtype=F32) / axis_dim))
    ang = jnp.concatenate([row[:, None] * inv_freq, col[:, None] * inv_freq], axis=-1)
    cos = jnp.repeat(jnp.cos(ang), 2, axis=-1)
    sin = jnp.repeat(jnp.sin(ang), 2, axis=-1) * jnp.tile(jnp.array([-1.0, 1.0], F32), HEAD_DIM // 2)
    reps = n_rows // seq
    return jnp.tile(cos, (reps, 1)), jnp.tile(sin, (reps, 1))


def _gqa_body(q_ref, k_ref, kc_ref, v_ref, vc_ref, o_ref, ka_ref, va_ref):
    seq = k_ref.shape[0]

    @pl.when(pl.program_id(2) == 0)
    def _():
        ka_ref[0:seq, :] = k_ref[...]
        ka_ref[seq:, :] = kc_ref[...]
        va_ref[0:seq, 0:HEAD_DIM] = v_ref[...]
        va_ref[seq:, 0:HEAD_DIM] = vc_ref[...]
        va_ref[:, HEAD_DIM:] = jnp.ones((va_ref.shape[0], va_ref.shape[1] - HEAD_DIM), BF16)

    for r in range(GQA_REP):
        cols = slice(r * HEAD_DIM, (r + 1) * HEAD_DIM)
        for r0 in range(0, q_ref.shape[0], GQA_BAND):
            rows = slice(r0, r0 + GQA_BAND)
            s = _qkt(q_ref[rows, cols], ka_ref[...])
            p = jnp.exp(s - s.max(axis=-1, keepdims=True)).astype(BF16)
            ox = jnp.dot(p, va_ref[...], preferred_element_type=F32)
            o_ref[rows, cols] = (ox[:, 0:HEAD_DIM] / ox[:, HEAD_DIM:HEAD_DIM + 1]).astype(o_ref.dtype)


def _gqa_attention(qk, qkv, n_batch, seq, ctx_len, q_cols):
    kv_heads = q_cols // HEAD_DIM // GQA_REP
    assert seq % GQA_QB == 0 and (n_batch * seq) % ctx_len == 0
    w = GQA_REP * HEAD_DIM
    qblocks = seq // GQA_QB
    k_blk0 = q_cols // HEAD_DIM
    v_blk0 = k_blk0 + kv_heads
    ctx_blk0 = (n_batch * seq) // ctx_len
    return pl.pallas_call(
        _gqa_body,
        grid=(n_batch, kv_heads, qblocks),
        in_specs=[pl.BlockSpec((GQA_QB, w), lambda b, h, i: (b * qblocks + i, h)),
                  pl.BlockSpec((seq, HEAD_DIM), lambda b, h, i: (b, k_blk0 + h)),
                  pl.BlockSpec((ctx_len, HEAD_DIM), lambda b, h, i: (ctx_blk0 + b, k_blk0 + h)),
                  pl.BlockSpec((seq, HEAD_DIM), lambda b, h, i: (b, v_blk0 + h)),
                  pl.BlockSpec((ctx_len, HEAD_DIM), lambda b, h, i: (ctx_blk0 + b, v_blk0 + h))],
        out_specs=pl.BlockSpec((GQA_QB, w), lambda b, h, i: (b * qblocks + i, h)),
        out_shape=jax.ShapeDtypeStruct((n_batch * seq, q_cols), BF16),
        scratch_shapes=[pltpu.VMEM((seq + ctx_len, HEAD_DIM), BF16),
                        pltpu.VMEM((seq + ctx_len, 2 * HEAD_DIM), BF16)],
        compiler_params=_params(("arbitrary", "arbitrary", "arbitrary")),
        name="gqa_attention",
    )(qk, qk, qk, qkv, qkv)


def _router_body(n_experts, lg_ref, b_ref, id_ref, wt_ref):
    lt = lg_ref[...].T[:n_experts] + b_ref[...]
    rows = [lt[e:e + 1] for e in range(n_experts)]
    mx = functools.reduce(jnp.maximum, rows)
    ex = [jnp.exp(r - mx) for r in rows]
    den = functools.reduce(jnp.add, ex)
    probs = [e / den for e in ex]

    epg = n_experts // N_GROUPS
    best = None
    for g in range(N_GROUPS):
        p = probs[g * epg:(g + 1) * epg]
        top1 = functools.reduce(jnp.maximum, p)
        i1 = jnp.full(top1.shape, epg, jnp.int32)
        for j in reversed(range(epg)):
            i1 = jnp.where(p[j] == top1, j, i1)
        rest = [jnp.where(i1 == j, -1.0, p[j]) for j in range(epg)]
        top2 = functools.reduce(jnp.maximum, rest)
        i2 = jnp.full(top1.shape, epg, jnp.int32)
        for j in reversed(range(epg)):
            i2 = jnp.where((rest[j] == top2) & (i1 != j), j, i2)
        cand = (top1 + top2, top1, top2, i1 + g * epg, i2 + g * epg)
        if best is None:
            best = cand
        else:
            take = cand[0] > best[0]
            best = tuple(jnp.where(take, c, b) for c, b in zip(cand, best))
    score, top1, top2, e1, e2 = best
    id_ref[...] = jnp.zeros(id_ref.shape, jnp.int32)
    wt_ref[...] = jnp.zeros(wt_ref.shape, F32)
    id_ref[0:1, :] = e1
    id_ref[1:2, :] = e2
    wt_ref[0:1, :] = top1 / score
    wt_ref[1:2, :] = top2 / score


def _router(logits, router_b):
    n = logits.shape[0]
    e = router_b.shape[0]
    tm = 512
    return pl.pallas_call(
        functools.partial(_router_body, e),
        grid=(n // tm,),
        in_specs=[pl.BlockSpec((tm, LANES), lambda i: (i, 0)),
                  pl.BlockSpec((e, 1), lambda i: (0, 0))],
        out_specs=[pl.BlockSpec((8, tm), lambda i: (0, i)), pl.BlockSpec((8, tm), lambda i: (0, i))],
        out_shape=[jax.ShapeDtypeStruct((8, n), jnp.int32), jax.ShapeDtypeStruct((8, n), F32)],
        compiler_params=_params(("arbitrary",)),
        name="router",
    )(logits, router_b.reshape(e, 1).astype(F32))


def _dispatch_plan(ids, n_experts, n_tiles):
    e0, e1 = ids[0], ids[1]
    n = e0.shape[0]
    ar = jnp.arange(n_experts, dtype=jnp.int32)[:, None]
    oh0 = (e0[None, :] == ar).astype(jnp.int32)
    oh1 = (e1[None, :] == ar).astype(jnp.int32)
    sel = oh0 + oh1
    csum = jnp.cumsum(sel, axis=1)
    counts = csum[:, -1]
    padded = ((counts + MOE_TM - 1) // MOE_TM) * MOE_TM
    ends = jnp.cumsum(padded)
    offs = ends - padded
    slot = offs[:, None] + csum - 1
    pos0 = jnp.sum(oh0 * slot, axis=0)
    pos1 = jnp.sum(oh1 * slot, axis=0)
    tok = jnp.arange(n, dtype=jnp.int32)
    src = jnp.zeros((n_tiles * MOE_TM,), jnp.int32)
    src = src.at[jnp.concatenate([pos0, pos1])].set(jnp.concatenate([tok, tok]), unique_indices=True)
    n_used = (ends[-1] // MOE_TM).astype(jnp.int32)
    tile_start = jnp.minimum(jnp.arange(n_tiles, dtype=jnp.int32), n_used - 1) * MOE_TM
    nonempty = (counts > 0).astype(jnp.int32)
    tile_seg = jnp.sum((ends[None, :] <= tile_start[:, None]).astype(jnp.int32) * nonempty[None, :], axis=1)
    seg_of_expert = jnp.cumsum(nonempty) - 1
    seg_expert = jnp.sum(jnp.where((seg_of_expert[None, :] == ar) & (nonempty[None, :] > 0), ar.T, 0), axis=1)
    meta = jnp.stack([n_used, jnp.sum(nonempty)]).astype(jnp.int32)
    return src, jnp.stack([pos0, pos1]), (tile_seg.astype(jnp.int32), seg_expert.astype(jnp.int32), meta)


def _row_copy(src_hbm, row, dst, k, sem):
    return pltpu.make_async_copy(src_hbm.at[pl.ds(row, 1), :], dst.at[pl.ds(k, 1), :], sem)


def _token_copy(src_hbm, tok, tile, g, r, sem):
    return pltpu.make_async_copy(src_hbm.at[tok], tile.at[g, :, r, :], sem)


def _issue_token_gather(idx_ref, first, h_hbm, tile, sem):
    def one(g, carry):
        for r in range(SUBLANES):
            _token_copy(h_hbm, idx_ref[first + g * SUBLANES + r], tile, g, r, sem).start(priority=1)
        return carry
    lax.fori_loop(0, tile.shape[0], one, 0)


def _wait_token_gather(h_hbm, tile, sem):
    def one(g, carry):
        for r in range(SUBLANES):
            _token_copy(h_hbm, 0, tile, g, r, sem).wait()
        return carry
    lax.fori_loop(0, tile.shape[0], one, 0)


def _stream_expert_weights(seg_ref, sege_ref, meta_ref, w_hbms, chunk, wf, wb, sem, priority):
    j, t = pl.program_id(0), pl.program_id(1)
    n_j = pl.num_programs(0)
    n_seg = meta_ref[1]
    k = seg_ref[t]
    first = (t == 0) | (k != seg_ref[jnp.maximum(t - 1, 0)])

    def copies(jj, kk):
        e = sege_ref[kk]
        return [pltpu.make_async_copy(w.at[e, :, pl.ds(jj * chunk, chunk)], wf.at[m], sem)
                for m, w in enumerate(w_hbms)]

    @pl.when(first)
    def _():
        @pl.when((j == 0) & (t == 0))
        def _():
            for c in copies(j, k):
                c.start(priority=priority)

        for c in copies(j, k):
            c.wait()
        for m in range(len(w_hbms)):
            _convert_rows(wb.at[m], wf.at[m])

        wrap = k + 1 == n_seg
        nj = jnp.where(wrap, j + 1, j)
        nk = jnp.where(wrap, 0, k + 1)

        @pl.when(nj < n_j)
        def _():
            for c in copies(nj, nk):
                c.start(priority=priority)


def _moe_up_body(seg_ref, sege_ref, meta_ref, src_ref, h_hbm, wg_hbm, wu_hbm, o_ref, xbuf, xb, gsem, wf, wb, wsem):
    j, t = pl.program_id(0), pl.program_id(1)
    n_j, n_t = pl.num_programs(0), pl.num_programs(1)
    n_used = meta_ref[0]
    slot = jnp.bitwise_and(j * n_t + t, 1)

    @pl.when((j == 0) & (t == 0))
    def _():
        _issue_token_gather(src_ref, 0, h_hbm, xbuf.at[0], gsem.at[0])

    nt = jnp.where(t + 1 < n_t, t + 1, 0)

    @pl.when(((t + 1 < n_t) | (j + 1 < n_j)) & (nt < n_used))
    def _():
        _issue_token_gather(src_ref, nt * MOE_TM, h_hbm, xbuf.at[1 - slot], gsem.at[1 - slot])

    _stream_expert_weights(seg_ref, sege_ref, meta_ref, (wg_hbm, wu_hbm), o_ref.shape[1], wf, wb, wsem, 0)

    @pl.when(t < n_used)
    def _():
        _wait_token_gather(h_hbm, xbuf.at[slot], gsem.at[slot])
        for s in range(xbuf.shape[2]):
            xb[:, s * LANES:(s + 1) * LANES] = xbuf[slot, :, s].reshape(MOE_TM, LANES).astype(xb.dtype)
        x = xb[...]
        g = jnp.dot(x, wb[0], preferred_element_type=F32)
        u = jnp.dot(x, wb[1], preferred_element_type=F32)
        o_ref[...] = (g * jax.nn.sigmoid(g) * u).astype(o_ref.dtype)

    @pl.when(t >= n_used)
    def _():
        o_ref[...] = jnp.zeros(o_ref.shape, o_ref.dtype)


def _moe_up(h_rows, src, w_gate, w_up, plan):
    _, pieces, _ = h_rows.shape
    d = pieces * LANES
    r = src.shape[0]
    f = w_gate.shape[2]
    n_tiles = r // MOE_TM
    fc = MOE_FC if f % MOE_FC == 0 else f
    any_spec = pl.BlockSpec(memory_space=pl.ANY)
    return pl.pallas_call(
        _moe_up_body,
        grid_spec=pltpu.PrefetchScalarGridSpec(
            num_scalar_prefetch=4,
            grid=(f // fc, n_tiles),
            in_specs=[any_spec, any_spec, any_spec],
            out_specs=pl.BlockSpec((MOE_TM, fc), lambda j, t, sg, se, mt, sr: (t, j)),
            scratch_shapes=[pltpu.VMEM((2, MOE_TM // SUBLANES, pieces, SUBLANES, LANES), F32),
                            pltpu.VMEM((MOE_TM, d), BF16), pltpu.SemaphoreType.DMA((2,)),
                            pltpu.VMEM((2, d, fc), F32), pltpu.VMEM((2, d, fc), BF16),
                            pltpu.SemaphoreType.DMA(())]),
        out_shape=jax.ShapeDtypeStruct((r, f), BF16),
        compiler_params=_params(("arbitrary", "arbitrary")),
        name="moe_up",
    )(*plan, src, h_rows, w_gate, w_up)


def _moe_down_body(seg_ref, sege_ref, meta_ref, h_ref, w_hbm, o_ref, wf, wb, sem):
    _stream_expert_weights(seg_ref, sege_ref, meta_ref, (w_hbm,), o_ref.shape[1], wf, wb, sem, 1)
    t = pl.program_id(1)

    @pl.when(t < meta_ref[0])
    def _():
        o_ref[...] = jnp.dot(h_ref[...], wb[0], preferred_element_type=F32)

    @pl.when(t >= meta_ref[0])
    def _():
        o_ref[...] = jnp.zeros(o_ref.shape, o_ref.dtype)


def _moe_down(hs, w_down, plan):
    r, f = hs.shape
    d = w_down.shape[2]
    nc = min(MOE_NC, d)
    n_tiles = r // MOE_TM
    return pl.pallas_call(
        _moe_down_body,
        grid_spec=pltpu.PrefetchScalarGridSpec(
            num_scalar_prefetch=3,
            grid=(d // nc, n_tiles),
            in_specs=[pl.BlockSpec((MOE_TM, f), lambda j, t, sg, se, mt: (jnp.minimum(t, mt[0] - 1), 0)),
                      pl.BlockSpec(memory_space=pl.ANY)],
            out_specs=pl.BlockSpec((MOE_TM, nc), lambda j, t, sg, se, mt: (t, j)),
            scratch_shapes=[pltpu.VMEM((1, f, nc), F32), pltpu.VMEM((1, f, nc), BF16),
                            pltpu.SemaphoreType.DMA(())]),
        out_shape=jax.ShapeDtypeStruct((r, d), F32),
        compiler_params=_params(("arbitrary", "arbitrary")),
        name="moe_down",
    )(*plan, hs, w_down)


def _combine_body(has_next, pos_ref, y_hbm, wt_ref, x_ref, g_ref, gate_ref, *rest):
    if has_next:
        g2_ref, sh_ref, sc_ref, xo_ref, ho_ref, buf, sem = rest
    else:
        xo_ref, buf, sem = rest
    i = pl.program_id(0)
    n = pl.num_programs(0)
    n_tok = n * ROW_TILE

    def issue(step, slot):
        def one(k, carry):
            tok = step * ROW_TILE + k
            _row_copy(y_hbm, pos_ref[tok], buf.at[slot, 0], k, sem.at[slot]).start()
            _row_copy(y_hbm, pos_ref[n_tok + tok], buf.at[slot, 1], k, sem.at[slot]).start()
            return carry
        lax.fori_loop(0, ROW_TILE, one, 0, unroll=8)

    @pl.when(i == 0)
    def _():
        issue(0, 0)

    @pl.when(i + 1 < n)
    def _():
        issue(i + 1, (i + 1) % 2)

    slot = i % 2

    def wait_one(k, carry):
        _row_copy(y_hbm, 0, buf.at[slot, 0], k, sem.at[slot]).wait()
        _row_copy(y_hbm, 0, buf.at[slot, 1], k, sem.at[slot]).wait()
        return carry
    lax.fori_loop(0, ROW_TILE, wait_one, 0, unroll=8)

    wt = wt_ref[...]
    y = wt[:, 0:1] * buf[slot, 0] + wt[:, 1:2] * buf[slot, 1]
    xn = x_ref[...] + gate_ref[0] * _rms(y, g_ref[...])
    xo_ref[...] = xn
    if has_next:
        h = _rms(xn, g2_ref[...])
        ho_ref[...] = (h * (1.0 + sc_ref[0]) + sh_ref[0]).astype(ho_ref.dtype)


def _combine(ys, pos, wts, x, gain, mod, gate_chunk, n_lat, seq, nxt=None):
    n = wts.shape[0]
    d = ys.shape[1]
    n_batch = mod.shape[0] - 1
    n_lat_tiles = min(n_lat, n) // ROW_TILE

    def mk(chunk):
        def index_map(i, pos_ref):
            return (jnp.where(i < n_lat_tiles, i // (seq // ROW_TILE), n_batch), 0, chunk)
        return index_map

    row = pl.BlockSpec((ROW_TILE, d), lambda i, p: (i, 0))
    vec = pl.BlockSpec((1, d), lambda i, p: (0, 0))
    in_specs = [pl.BlockSpec(memory_space=pl.ANY),
                pl.BlockSpec((ROW_TILE, 2), lambda i, p: (i, 0)),
                row, vec, pl.BlockSpec((1, 1, d), mk(gate_chunk))]
    args = [ys, wts, x, gain.reshape(1, d), mod]
    out_specs = [row]
    out_shape = [jax.ShapeDtypeStruct((n, d), F32)]
    if nxt is not None:
        gain2, mod2, shift_chunk, scale_chunk, h_dtype = nxt
        in_specs += [vec, pl.BlockSpec((1, 1, d), mk(shift_chunk)), pl.BlockSpec((1, 1, d), mk(scale_chunk))]
        args += [gain2.reshape(1, d), mod2, mod2]
        out_specs.append(row)
        out_shape.append(jax.ShapeDtypeStruct((n, d), h_dtype))
    out = pl.pallas_call(
        functools.partial(_combine_body, nxt is not None),
        grid_spec=pltpu.PrefetchScalarGridSpec(
            num_scalar_prefetch=1,
            grid=(n // ROW_TILE,),
            in_specs=in_specs,
            out_specs=out_specs,
            scratch_shapes=[pltpu.VMEM((2, 2, ROW_TILE, d), F32), pltpu.SemaphoreType.DMA((2,))]),
        out_shape=out_shape,
        compiler_params=_params(("arbitrary",)),
        name="moe_combine",
    )(pos.reshape(-1), *args)
    return out if nxt is not None else out[0]


def _moe(h_rows, logits, router_b, w_gate, w_up, w_down):
    n = logits.shape[0]
    n_experts = router_b.shape[0]
    n_tiles = (TOP_K * n) // MOE_TM + n_experts
    ids, wts = _router(logits, router_b)
    src, pos, plan = _dispatch_plan(ids, n_experts, n_tiles)
    hs = _moe_up(h_rows, src, w_gate, w_up, plan)
    ys = _moe_down(hs, w_down, plan)
    return ys, pos, wts[:2].T


def kernel(x, c, ctx, c_ctx, router_w, router_b, l0_ada_w, l0_ada_b, l0_norm_pre_mix, l0_norm_post_mix, l0_norm_pre_ffn, l0_norm_post_ffn, l0_na_w_qkv, l0_na_rpb, l0_na_w_o, l0_moe_w_gate, l0_moe_w_up, l0_moe_w_down, l1_ada_w, l1_ada_b, l1_norm_pre_mix, l1_norm_post_mix, l1_norm_pre_ffn, l1_norm_post_ffn, l1_gqa_w_qkv, l1_gqa_q_gain, l1_gqa_k_gain, l1_gqa_w_o, l1_moe_w_gate, l1_moe_w_up, l1_moe_w_down):
    n_batch, seq, d = x.shape
    ctx_len = ctx.shape[1]
    n_lat = n_batch * seq
    n_all = n_lat + n_batch * ctx_len
    scale = HEAD_DIM ** -0.5
    SH_M, SC_M, G_M, SH_F, SC_F, G_F = range(6)

    cvec = jnp.zeros((8, d), F32).at[:n_batch].set(c).at[n_batch].set(c_ctx)
    mod0 = _adaln(cvec, l0_ada_w, l0_ada_b)[:n_batch + 1].reshape(n_batch + 1, 1, 6 * d)
    mod1 = _adaln(cvec, l1_ada_w, l1_ada_b)[:n_batch + 1].reshape(n_batch + 1, 1, 6 * d)

    x_lat = x.reshape(n_lat, d)
    x_ctx = ctx.reshape(n_batch * ctx_len, d)

    h = _norm_mod(x_lat, x_ctx, l0_norm_pre_mix, mod0, SH_M, SC_M, seq, BF16)
    qscale = jnp.concatenate([jnp.full((1, d), scale, F32), jnp.ones((1, 2 * d), F32)], axis=1)
    qkv = _matmul(h, l0_na_w_qkv, qscale, BF16)
    bias = _na_bias_table(l0_na_rpb, seq // GRID_W)
    att = _na_attention(qkv, bias, n_batch, seq, ctx_len)
    y = _matmul(att, l0_na_w_o, jnp.ones((1, d), F32), F32)
    xa, h_rows, logits = _residual(x_lat, x_ctx, y, l0_norm_post_mix, mod0, G_M, seq,
                                   l0_norm_pre_ffn, SH_F, SC_F, router_w)
    ys, pos, wts = _moe(h_rows, logits, router_b, l0_moe_w_gate, l0_moe_w_up, l0_moe_w_down)
    xa, h = _combine(ys, pos, wts, xa, l0_norm_post_ffn, mod0, G_F, n_lat, seq,
                     nxt=(l1_norm_pre_mix, mod1, SH_M, SC_M, BF16))

    q_cols = d
    kv_cols = d // GQA_REP
    cos, sin = _rope_tables(seq, n_lat)
    ident = (jnp.ones((n_all - n_lat, HEAD_DIM), F32), jnp.zeros((n_all - n_lat, HEAD_DIM), F32))
    cos = jnp.concatenate([cos, ident[0]], axis=0)
    sin = jnp.concatenate([sin, ident[1]], axis=0)
    gain_post = jnp.stack([jnp.stack([l1_gqa_q_gain.astype(F32), jnp.full((HEAD_DIM,), scale, F32)]),
                           jnp.stack([l1_gqa_k_gain.astype(F32), jnp.ones((HEAD_DIM,), F32)])])
    qkv = _qkv_gqa(h, l1_gqa_w_qkv, cos, sin, gain_post, q_cols, kv_cols)
    att = _gqa_attention(qkv, qkv, n_batch, seq, ctx_len, q_cols)
    y = _matmul(att, l1_gqa_w_o, jnp.ones((1, d), F32), F32)
    xl, h_rows, logits = _residual(xa, xa, y, l1_norm_post_mix, mod1, G_M, seq,
                                   l1_norm_pre_ffn, SH_F, SC_F, router_w)
    ys, pos, wts = _moe(h_rows, logits, router_b, l1_moe_w_gate, l1_moe_w_up, l1_moe_w_down)
    xl = _combine(ys, pos, wts, xl, l1_norm_post_ffn, mod1, G_F, n_lat, seq)
    return xl.reshape(n_batch, seq, d)
```
